```python
import math
import jax, jax.numpy as jnp
from jax import lax
import numpy as np

D_MODEL = 2048
BATCH = 16
SEQ = 2048
DEPTH = 2

N_META = 16
ATTN_WIDTH = D_MODEL // 2
POOL_WIDTH = D_MODEL // 2
HEAD_DIM = 64
N_HEADS = ATTN_WIDTH // (2 * HEAD_DIM)
POOL_WINDOWS = (2, 4, 8, 16)
N_POOL_GROUPS = len(POOL_WINDOWS)
POOL_GROUP = POOL_WIDTH // N_POOL_GROUPS
D_FF = 256 * ((8 * D_MODEL // 3 + 255) // 256)
N_EXPERTS = 8
TOP_K = 2
ROPE_THETA = 10000.0
EPS = 1e-5
Q_BLOCK = 128
N_DENSE = (DEPTH + 1) // 2
N_MOE = DEPTH // 2
IN_COLS = 3 * ATTN_WIDTH + POOL_WIDTH + 2 * D_MODEL

kernel_name = "hybrid_diffattn_pool_moe_encoder"


def rmsnorm(x, g):
    xf = x.astype(jnp.float32)
    y = xf * lax.rsqrt(jnp.mean(xf * xf, axis=-1, keepdims=True) + EPS)
    return (y * g.astype(jnp.float32)).astype(x.dtype)


def rope(t, pos):
    half = HEAD_DIM // 2
    inv = ROPE_THETA ** (-2.0 * jnp.arange(half, dtype=jnp.float32) / HEAD_DIM)
    ang = pos.astype(jnp.float32)[:, None] * inv[None, :]
    cos = jnp.cos(ang)[None, :, None, :]
    sin = jnp.sin(ang)[None, :, None, :]
    tf = t.astype(jnp.float32)
    t1, t2 = tf[..., :half], tf[..., half:]
    return jnp.concatenate([t1 * cos - t2 * sin, t2 * cos + t1 * sin], axis=-1).astype(t.dtype)


def diff_attention(q, k, v, lam_full, lambda_init, subln_g):
    B, L = q.shape[0], q.shape[1]
    n_blocks = -(-L // Q_BLOCK)
    Lp = n_blocks * Q_BLOCK
    q = q * (HEAD_DIM ** -0.5)
    qp = jnp.pad(q, ((0, 0), (0, Lp - L), (0, 0), (0, 0)))
    qb = qp.reshape(B, n_blocks, Q_BLOCK, 2 * N_HEADS, HEAD_DIM).transpose(1, 0, 2, 3, 4)

    def one_block(qblk):
        s = jnp.einsum('bqhd,bkhd->bhqk', qblk, k, preferred_element_type=jnp.float32)
        p = jax.nn.softmax(s, axis=-1).reshape(B, N_HEADS, 2, Q_BLOCK, L)
        w = p[:, :, 0] - lam_full * p[:, :, 1]
        return jnp.einsum('bhqk,bkhe->bqhe', w.astype(v.dtype), v)

    o = lax.map(one_block, qb)
    o = o.transpose(1, 0, 2, 3, 4).reshape(B, Lp, N_HEADS, 2 * HEAD_DIM)[:, :L]
    o = rmsnorm(o, subln_g) * (1.0 - lambda_init)
    return o.reshape(B, L, ATTN_WIDTH)


def pool_mixer(u, pool_w, pool_scale):
    B, L, C = u.shape
    uf = u.astype(jnp.float32)
    cs = jnp.concatenate([jnp.zeros((B, 1, C), jnp.float32), jnp.cumsum(uf, axis=1)], axis=1)
    pos = jnp.arange(L)
    outs = []
    for gi, w in enumerate(POOL_WINDOWS):
        lo = jnp.clip(pos - w // 2, 0, L)
        hi = jnp.clip(pos + w // 2, 0, L)
        cs_g = cs[:, :, gi * POOL_GROUP:(gi + 1) * POOL_GROUP]
        win_sum = jnp.take(cs_g, hi, axis=1) - jnp.take(cs_g, lo, axis=1)
        cnt = (hi - lo).astype(jnp.float32)[None, :, None]
        outs.append(win_sum / cnt - uf[:, :, gi * POOL_GROUP:(gi + 1) * POOL_GROUP])
    m = jnp.stack(outs, axis=2)
    y = jnp.einsum('blgc,gce->blge', m, pool_w.astype(jnp.float32)).reshape(B, L, C)
    return (y * pool_scale.astype(jnp.float32)).astype(u.dtype)


def token_mixer(h, pos, layer_idx, w_in, b_gate, lam_q1, lam_k1, lam_q2, lam_k2, subln_g,
                pool_w, pool_scale, w_br_attn, w_br_pool, w_out):
    B, L, _ = h.shape
    z = h @ w_in
    A, P, D = ATTN_WIDTH, POOL_WIDTH, D_MODEL
    q, k, v, u, gates = jnp.split(z, [A, 2 * A, 3 * A, 3 * A + P], axis=-1)
    q = rope(q.reshape(B, L, 2 * N_HEADS, HEAD_DIM), pos)
    k = rope(k.reshape(B, L, 2 * N_HEADS, HEAD_DIM), pos)
    v = v.reshape(B, L, N_HEADS, 2 * HEAD_DIM)
    lambda_init = 0.8 - 0.6 * math.exp(-0.3 * layer_idx)
    lam_full = (jnp.exp(jnp.sum(lam_q1.astype(jnp.float32) * lam_k1.astype(jnp.float32)))
                - jnp.exp(jnp.sum(lam_q2.astype(jnp.float32) * lam_k2.astype(jnp.float32)))
                + lambda_init)
    a = diff_attention(q, k, v, lam_full, lambda_init, subln_g)
    p = pool_mixer(u, pool_w, pool_scale)
    g = jax.nn.sigmoid(gates + b_gate)
    y = g[..., :D] * (a @ w_br_attn) + g[..., D:] * (p @ w_br_pool)
    return y @ w_out


def swiglu(t, wg, wu, wd):
    return (jax.nn.silu(t @ wg) * (t @ wu)) @ wd


def moe_swiglu(h, router, wg, wu, wd):
    B, L, D = h.shape
    t = h.reshape(B * L, D)
    logits = (t @ router).astype(jnp.float32)
    top_v, top_i = lax.top_k(logits, TOP_K)
    top_w = jax.nn.softmax(top_v, axis=-1)
    combine = jnp.sum(jax.nn.one_hot(top_i, N_EXPERTS, dtype=jnp.float32) * top_w[..., None], axis=1)
    out = jnp.zeros_like(t)
    for e in range(N_EXPERTS):
        out = out + combine[:, e:e + 1].astype(t.dtype) * swiglu(t, wg[e], wu[e], wd[e])
    return out.reshape(B, L, D)


def setup_inputs(seed: int = 0) -> dict:
    key = jax.random.key(seed)
    ks = jax.random.split(key, 24)
    f32 = jnp.float32
    nrm = lambda k, shape, s: jax.random.normal(k, shape, f32) * s
    gain = lambda k, shape: 1.0 + 0.02 * jax.random.normal(k, shape, f32)
    return {
        "x": nrm(ks[0], (BATCH, SEQ, D_MODEL), 1.0),
        "meta_tokens": nrm(ks[1], (N_META, D_MODEL), 1.0),
        "norm_mix": gain(ks[2], (DEPTH, D_MODEL)),
        "w_in": nrm(ks[3], (DEPTH, D_MODEL, IN_COLS), D_MODEL ** -0.5),
        "b_gate": nrm(ks[4], (DEPTH, 2 * D_MODEL), 0.02),
        "lambda_q1": nrm(ks[5], (DEPTH, HEAD_DIM), 0.1),
        "lambda_k1": nrm(ks[6], (DEPTH, HEAD_DIM), 0.1),
        "lambda_q2": nrm(ks[7], (DEPTH, HEAD_DIM), 0.1),
        "lambda_k2": nrm(ks[8], (DEPTH, HEAD_DIM), 0.1),
        "subln": gain(ks[9], (DEPTH, 2 * HEAD_DIM)),
        "pool_w": nrm(ks[10], (DEPTH, N_POOL_GROUPS, POOL_GROUP, POOL_GROUP), POOL_GROUP ** -0.5),
        "pool_scale": gain(ks[11], (DEPTH, POOL_WIDTH)),
        "w_branch_attn": nrm(ks[12], (DEPTH, ATTN_WIDTH, D_MODEL), ATTN_WIDTH ** -0.5),
        "w_branch_pool": nrm(ks[13], (DEPTH, POOL_WIDTH, D_MODEL), POOL_WIDTH ** -0.5),
        "w_out": nrm(ks[14], (DEPTH, D_MODEL, D_MODEL), D_MODEL ** -0.5),
        "norm_ffn": gain(ks[15], (DEPTH, D_MODEL)),
        "dense_w_gate": nrm(ks[16], (N_DENSE, D_MODEL, D_FF), D_MODEL ** -0.5),
        "dense_w_up": nrm(ks[17], (N_DENSE, D_MODEL, D_FF), D_MODEL ** -0.5),
        "dense_w_down": nrm(ks[18], (N_DENSE, D_FF, D_MODEL), D_FF ** -0.5),
        "router": nrm(ks[19], (N_MOE, D_MODEL, N_EXPERTS), D_MODEL ** -0.5),
        "moe_w_gate": nrm(ks[20], (N_MOE, N_EXPERTS, D_MODEL, D_FF), D_MODEL ** -0.5),
        "moe_w_up": nrm(ks[21], (N_MOE, N_EXPERTS, D_MODEL, D_FF), D_MODEL ** -0.5),
        "moe_w_down": nrm(ks[22], (N_MOE, N_EXPERTS, D_FF, D_MODEL), D_FF ** -0.5),
        "norm_final": gain(ks[23], (D_MODEL,)),
    }


def reference(x, meta_tokens, norm_mix, w_in, b_gate, lambda_q1, lambda_k1, lambda_q2, lambda_k2,
              subln, pool_w, pool_scale, w_branch_attn, w_branch_pool, w_out, norm_ffn,
              dense_w_gate, dense_w_up, dense_w_down, router, moe_w_gate, moe_w_up, moe_w_down,
              norm_final):
    B = x.shape[0]
    meta = jnp.broadcast_to(meta_tokens[None].astype(x.dtype), (B, N_META, D_MODEL))
    h_res = jnp.concatenate([meta, x], axis=1)
    L = h_res.shape[1]
    pos = jnp.arange(L)
    for i in range(DEPTH):
        h = rmsnorm(h_res, norm_mix[i])
        h_res = h_res + token_mixer(h, pos, i, w_in[i], b_gate[i], lambda_q1[i], lambda_k1[i],
                                    lambda_q2[i], lambda_k2[i], subln[i], pool_w[i], pool_scale[i],
                                    w_branch_attn[i], w_branch_pool[i], w_out[i])
        h = rmsnorm(h_res, norm_ffn[i])
        if i % 2 == 0:
            j = i // 2
            h_res = h_res + swiglu(h, dense_w_gate[j], dense_w_up[j], dense_w_down[j])
        else:
            j = i // 2
            h_res = h_res + moe_swiglu(h, router[j], moe_w_gate[j], moe_w_up[j], moe_w_down[j])
    out = rmsnorm(h_res, norm_final)
    return out[:, N_META:]
```

```python
import functools
import math

import jax
import jax.numpy as jnp
from jax import lax
from jax.experimental import pallas as pl
from jax.experimental.pallas import tpu as pltpu

N_META = 16
HEAD_DIM = 64
POOL_WINDOWS = (2, 4, 8, 16)
TOP_K = 2
ROPE_THETA = 10000.0
EPS = 1e-5

LANES = 128
SUBLANES_BF16 = 16
VMEM_CAP_BYTES = 64 * 1024 * 1024
VMEM_LIMIT_BYTES = VMEM_CAP_BYTES - 6 * 1024 * 1024

NEG_BIG = -1e30
POOL_PAD = max(POOL_WINDOWS)

F32 = jnp.float32
BF16 = jnp.bfloat16


def _largest_divisor(n, candidates):
    for c in candidates:
        if c <= n and n % c == 0:
            return c
    raise ValueError(f"no block size in {candidates} divides {n}")


def _params(*semantics):
    return pltpu.CompilerParams(dimension_semantics=semantics, vmem_limit_bytes=VMEM_LIMIT_BYTES)


def _rmsnorm(x, g):
    ms = jnp.mean(x * x, axis=-1, keepdims=True)
    return x * lax.rsqrt(ms + EPS) * g


def _dot(a, b):
    return jnp.dot(a, b, preferred_element_type=F32)


def _inproj_body(x_ref, g_ref, w_ref, o_ref, xn_ref):
    @pl.when(pl.program_id(1) == 0)
    def _():
        xn_ref[...] = _rmsnorm(x_ref[...], g_ref[...]).astype(BF16)

    o_ref[...] = _dot(xn_ref[...], w_ref[...]).astype(o_ref.dtype)


def _inproj(h, g, w):
    T, D = h.shape
    N = w.shape[1]
    bm = _largest_divisor(T, (768, 512, 384, 256, 128, 64, 32, 16))
    bn = _largest_divisor(N, (1024, 512, 256, 128))
    return pl.pallas_call(
        _inproj_body,
        grid=(T // bm, N // bn),
        in_specs=[
            pl.BlockSpec((bm, D), lambda i, j: (i, 0)),
            pl.BlockSpec((1, D), lambda i, j: (0, 0)),
            pl.BlockSpec((D, bn), lambda i, j: (0, j)),
        ],
        out_specs=pl.BlockSpec((bm, bn), lambda i, j: (i, j)),
        out_shape=jax.ShapeDtypeStruct((T, N), BF16),
        scratch_shapes=[pltpu.VMEM((bm, D), BF16)],
        compiler_params=_params("parallel", "arbitrary"),
        name="inproj",
    )(h, g, w)


def _attn_body(lam_ref, q_ref, k_ref, v_ref, cos_ref, sin_ref, sg_ref, o_ref, kp_ref, vp_ref, of_ref,
               *, L, l_main, l_keys, bq, lambda_init):
    lane = lax.broadcasted_iota(jnp.int32, (1, LANES), 1)
    first_half = (lane % HEAD_DIM) < (HEAD_DIM // 2)

    def rope(t):
        partner = jnp.where(first_half, pltpu.roll(t, LANES - HEAD_DIM // 2, 1), pltpu.roll(t, HEAD_DIM // 2, 1))
        return t * cos_ref[...] + partner * sin_ref[...]

    q = rope(q_ref[...].astype(F32)) * (HEAD_DIM ** -0.5)
    kp_ref[pl.ds(0, L), :] = rope(k_ref[...].astype(F32)).astype(BF16)
    vp_ref[pl.ds(0, L), :] = v_ref[...]
    if l_keys > L:
        kp_ref[pl.ds(L, l_keys - L), :] = jnp.zeros((l_keys - L, LANES), BF16)
        vp_ref[pl.ds(L, l_keys - L), :] = jnp.zeros((l_keys - L, LANES), BF16)

    lam = lam_ref[...]
    lam_full = (jnp.exp(jnp.sum(lam[0:1] * lam[1:2], axis=-1, keepdims=True))
                - jnp.exp(jnp.sum(lam[2:3] * lam[3:4], axis=-1, keepdims=True)) + lambda_init)

    tail_valid = (lax.broadcasted_iota(jnp.int32, (1, LANES), 1) + l_main) < L
    contract_last = (((1,), (1,)), ((), ()))

    def softmax_parts(s):
        s_main = s[:, :l_main]
        m = jnp.max(s_main, axis=-1, keepdims=True)
        if l_keys > l_main:
            s_tail = jnp.where(tail_valid, s[:, l_main:], NEG_BIG)
            m = jnp.maximum(m, jnp.max(s_tail, axis=-1, keepdims=True))
            e_tail = jnp.exp(s_tail - m)
        e_main = jnp.exp(s_main - m)
        denom = jnp.sum(e_main, axis=-1, keepdims=True)
        if l_keys > l_main:
            denom = denom + jnp.sum(e_tail, axis=-1, keepdims=True)
            return e_main, e_tail, denom
        return e_main, None, denom

    for i in range(L // bq):
        qb = q[i * bq:(i + 1) * bq]
        q0 = jnp.where(lane < HEAD_DIM, qb, 0.0).astype(BF16)
        q1 = jnp.where(lane >= HEAD_DIM, qb, 0.0).astype(BF16)
        kp = kp_ref[...]
        e0m, e0t, d0 = softmax_parts(lax.dot_general(q0, kp, contract_last, preferred_element_type=F32))
        e1m, e1t, d1 = softmax_parts(lax.dot_general(q1, kp, contract_last, preferred_element_type=F32))
        r0 = 1.0 / d0
        r1 = lam_full / d1
        o = _dot((e0m * r0 - e1m * r1).astype(BF16), vp_ref[pl.ds(0, l_main), :])
        if l_keys > l_main:
            o = o + _dot((e0t * r0 - e1t * r1).astype(BF16), vp_ref[pl.ds(l_main, l_keys - l_main), :])
        of_ref[pl.ds(i * bq, bq), :] = _rmsnorm(o, sg_ref[...]) * (1.0 - lambda_init)
    o_ref[...] = of_ref[...].astype(o_ref.dtype)


def _attention(z3, lam, cos, sin, subln_g, *, n_heads, lambda_init):
    B, L, _ = z3.shape
    l_main = (L // LANES) * LANES
    l_keys = l_main if l_main == L else l_main + LANES
    bq = _largest_divisor(L, tuple(b for b in range(400, 7, -8)))
    body = functools.partial(_attn_body, L=L, l_main=l_main, l_keys=l_keys, bq=bq, lambda_init=lambda_init)
    blk = lambda off: pl.BlockSpec((None, L, LANES), lambda b, h: (b, 0, off + h))
    const = lambda shape: pl.BlockSpec(shape, lambda b, h: (0,) * len(shape))
    return pl.pallas_call(
        body,
        grid=(B, n_heads),
        in_specs=[const(lam.shape), blk(0), blk(n_heads), blk(2 * n_heads),
                  const(cos.shape), const(sin.shape), const(subln_g.shape)],
        out_specs=pl.BlockSpec((None, L, LANES), lambda b, h: (b, 0, h)),
        out_shape=jax.ShapeDtypeStruct((B, L, n_heads * LANES), BF16),
        scratch_shapes=[pltpu.VMEM((l_keys, LANES), BF16), pltpu.VMEM((l_keys, LANES), BF16),
                        pltpu.VMEM((L, LANES), F32)],
        compiler_params=_params("parallel", "parallel"),
        name="diff_attention",
    )(lam, z3, z3, z3, cos, sin, subln_g)


def _pool_body(u_ref, pw_ref, ps_ref, o_ref, pad_ref, *, L, cg):
    t = lax.broadcasted_iota(jnp.int32, (L, 1), 0)
    zeros = jnp.zeros((POOL_PAD, cg), F32)
    pad_ref[pl.ds(0, POOL_PAD), :] = zeros
    pad_ref[pl.ds(POOL_PAD + L, POOL_PAD), :] = zeros
    for g, w in enumerate(POOL_WINDOWS):
        cols = slice(g * cg, (g + 1) * cg)
        ug = u_ref[:, cols].astype(F32)
        pad_ref[pl.ds(POOL_PAD, L), :] = ug
        start = POOL_PAD - w // 2
        win = pad_ref[pl.ds(start, L), :]
        for j in range(1, w):
            win = win + pad_ref[pl.ds(start + j, L), :]
        cnt = (jnp.minimum(t + w // 2, L) - jnp.maximum(t - w // 2, 0)).astype(F32)
        m = win / cnt - ug
        y = _dot(m.astype(BF16), pw_ref[g]) * ps_ref[:, cols]
        o_ref[:, cols] = y.astype(o_ref.dtype)


def _pool(z3, pool_w, pool_scale, *, col_block):
    B, L, _ = z3.shape
    G, cg, _ = pool_w.shape
    P = G * cg
    body = functools.partial(_pool_body, L=L, cg=cg)
    return pl.pallas_call(
        body,
        grid=(B,),
        in_specs=[
            pl.BlockSpec((None, L, P), lambda b: (b, 0, col_block)),
            pl.BlockSpec((G, cg, cg), lambda b: (0, 0, 0)),
            pl.BlockSpec((1, P), lambda b: (0, 0)),
        ],
        out_specs=pl.BlockSpec((None, L, P), lambda b: (b, 0, 0)),
        out_shape=jax.ShapeDtypeStruct((B, L, P), BF16),
        scratch_shapes=[pltpu.VMEM((L + 2 * POOL_PAD, cg), F32)],
        compiler_params=_params("parallel"),
        name="pool_mixer",
    )(z3, pool_w, pool_scale)


def _mix_body(a_ref, p_ref, gt_ref, b_ref, wa_ref, wp_ref, wo_ref, h_ref, o_ref, *, D):
    gates = jax.nn.sigmoid(gt_ref[...].astype(F32) + b_ref[...])
    y = gates[:, :D] * _dot(a_ref[...], wa_ref[...]) + gates[:, D:] * _dot(p_ref[...], wp_ref[...])
    o_ref[...] = h_ref[...] + _dot(y.astype(BF16), wo_ref[...])


def _mix(a, p, z, b_gate, wa, wp, wo, h, *, gate_col_block):
    T, D = h.shape
    A, P = a.shape[1], p.shape[1]
    bm = _largest_divisor(T, (256, 128, 64, 32, 16))
    body = functools.partial(_mix_body, D=D)
    const = lambda shape: pl.BlockSpec(shape, lambda i: (0, 0))
    return pl.pallas_call(
        body,
        grid=(T // bm,),
        in_specs=[
            pl.BlockSpec((bm, A), lambda i: (i, 0)),
            pl.BlockSpec((bm, P), lambda i: (i, 0)),
            pl.BlockSpec((bm, 2 * D), lambda i: (i, gate_col_block)),
            const((1, 2 * D)), const((A, D)), const((P, D)), const((D, D)),
            pl.BlockSpec((bm, D), lambda i: (i, 0)),
        ],
        out_specs=pl.BlockSpec((bm, D), lambda i: (i, 0)),
        out_shape=jax.ShapeDtypeStruct((T, D), F32),
        compiler_params=_params("parallel"),
        name="mix_out",
    )(a, p, z, b_gate, wa, wp, wo, h)


def _ffn_body(x_ref, g_ref, wg_ref, wu_ref, wd_ref, o_ref, xn_ref):
    @pl.when(pl.program_id(1) == 0)
    def _():
        x = x_ref[...]
        xn_ref[...] = _rmsnorm(x, g_ref[...]).astype(BF16)
        o_ref[...] = x

    xn = xn_ref[...]
    hid = jax.nn.silu(_dot(xn, wg_ref[...])) * _dot(xn, wu_ref[...])
    o_ref[...] += _dot(hid.astype(BF16), wd_ref[...])


def _dense_ffn(h, g, wg, wu, wd):
    T, D = h.shape
    F = wg.shape[1]
    bm = _largest_divisor(T, (768, 512, 384, 256, 128, 64, 32, 16))
    fc = _largest_divisor(F, (512, 256, 128))
    return pl.pallas_call(
        _ffn_body,
        grid=(T // bm, F // fc),
        in_specs=[
            pl.BlockSpec((bm, D), lambda i, c: (i, 0)),
            pl.BlockSpec((1, D), lambda i, c: (0, 0)),
            pl.BlockSpec((D, fc), lambda i, c: (0, c)),
            pl.BlockSpec((D, fc), lambda i, c: (0, c)),
            pl.BlockSpec((fc, D), lambda i, c: (c, 0)),
        ],
        out_specs=pl.BlockSpec((bm, D), lambda i, c: (i, 0)),
        out_shape=jax.ShapeDtypeStruct((T, D), F32),
        scratch_shapes=[pltpu.VMEM((bm, D), BF16)],
        compiler_params=_params("parallel", "arbitrary"),
        name="dense_ffn",
    )(h, g, wg, wu, wd)


ROUTE_I1, ROUTE_I2, ROUTE_W1, ROUTE_W2, ROUTE_R1, ROUTE_R2 = range(6)


def _router_body(x_ref, g_ref, rhi_ref, rlo_ref, tri_ref, route_ref, cnt_ref, carry_ref, *, n_experts):
    @pl.when(pl.program_id(0) == 0)
    def _():
        carry_ref[...] = jnp.zeros_like(carry_ref)

    xn = _rmsnorm(x_ref[...], g_ref[...])
    x_hi = xn.astype(BF16)
    x_lo = (xn - x_hi.astype(F32)).astype(BF16)
    logits = _dot(x_hi, rhi_ref[...]) + (_dot(x_lo, rhi_ref[...]) + _dot(x_hi, rlo_ref[...]))

    lane = lax.broadcasted_iota(jnp.int32, (1, LANES), 1)
    lane_f = lane.astype(F32)
    lg = jnp.where(lane < n_experts, logits, NEG_BIG)
    v1 = jnp.max(lg, axis=-1, keepdims=True)
    i1 = jnp.min(jnp.where(lg == v1, lane_f, float(LANES)), axis=-1, keepdims=True)
    lg2 = jnp.where(lane_f == i1, NEG_BIG, lg)
    v2 = jnp.max(lg2, axis=-1, keepdims=True)
    i2 = jnp.min(jnp.where(lg2 == v2, lane_f, float(LANES)), axis=-1, keepdims=True)
    e2 = jnp.exp(v2 - v1)
    w1 = 1.0 / (1.0 + e2)
    w2 = e2 / (1.0 + e2)

    sel1 = lane_f == i1
    sel2 = lane_f == i2
    onehot = jnp.where(sel1 | sel2, 1.0, 0.0)
    rank = _dot(tri_ref[...], onehot.astype(BF16)) + carry_ref[...]
    r1 = jnp.sum(jnp.where(sel1, rank, 0.0), axis=-1, keepdims=True)
    r2 = jnp.sum(jnp.where(sel2, rank, 0.0), axis=-1, keepdims=True)
    carry_ref[...] += jnp.sum(onehot, axis=0, keepdims=True)

    route = jnp.zeros(route_ref.shape, F32)
    for col, val in ((ROUTE_I1, i1), (ROUTE_I2, i2), (ROUTE_W1, w1), (ROUTE_W2, w2), (ROUTE_R1, r1), (ROUTE_R2, r2)):
        route = jnp.where(lane == col, val, route)
    route_ref[...] = route
    cnt_ref[...] = jnp.broadcast_to(carry_ref[...], cnt_ref.shape)


def _router(h, g, r_hi, r_lo, *, n_experts):
    T, D = h.shape
    bm = _largest_divisor(T, (256, 128, 64, 32, 16))
    tri = (lax.broadcasted_iota(jnp.int32, (bm, bm), 0) > lax.broadcasted_iota(jnp.int32, (bm, bm), 1)).astype(BF16)
    body = functools.partial(_router_body, n_experts=n_experts)
    const = lambda shape: pl.BlockSpec(shape, lambda i: (0, 0))
    return pl.pallas_call(
        body,
        grid=(T // bm,),
        in_specs=[pl.BlockSpec((bm, D), lambda i: (i, 0)), const((1, D)), const((D, LANES)), const((D, LANES)),
                  const((bm, bm))],
        out_specs=[pl.BlockSpec((bm, LANES), lambda i: (i, 0)), const((8, LANES))],
        out_shape=[jax.ShapeDtypeStruct((T, LANES), F32), jax.ShapeDtypeStruct((8, LANES), F32)],
        scratch_shapes=[pltpu.VMEM((1, LANES), F32)],
        compiler_params=_params("arbitrary"),
        name="moe_router",
    )(h, g, r_hi, r_lo, tri)


ZERO_ROWS = 64


def _dispatch_body(pos_ref, seg_ref, x_ref, g_ref, xs_hbm, xn_ref, zero_ref, sem, *, bm, T, n_experts, be, n_rows):
    i = pl.program_id(0)

    def zero_copy(first_row, c):
        row = pl.multiple_of(first_row + c * ZERO_ROWS, ZERO_ROWS)
        return pltpu.make_async_copy(zero_ref, xs_hbm.at[pl.ds(row, ZERO_ROWS), :], sem)

    def zero_block(first_row):
        for c in range(be // ZERO_ROWS):
            zero_copy(first_row, c).start()
        for c in range(be // ZERO_ROWS):
            zero_copy(first_row, c).wait()

    @pl.when(i == 0)
    def _():
        zero_ref[...] = jnp.zeros_like(zero_ref)
        for e in range(n_experts):
            @pl.when(seg_ref[n_experts + e] > 0)
            def _():
                zero_block(seg_ref[e] - be)

        def clear_unused(j, c):
            zero_block(j * be)
            return c

        lax.fori_loop(seg_ref[n_experts - 1] // be, n_rows // be, clear_unused, 0)

    xn_ref[...] = _rmsnorm(x_ref[...], g_ref[...])

    def row_copy(r, k):
        dst = pos_ref[k * T + i * bm + r]
        return pltpu.make_async_copy(xn_ref.at[pl.ds(r, 1), :], xs_hbm.at[pl.ds(dst, 1), :], sem)

    def issue(r, c):
        row_copy(r, 0).start()
        row_copy(r, 1).start()
        return c

    lax.fori_loop(0, bm, issue, 0)

    def drain(r, c):
        row_copy(r, 0).wait()
        row_copy(r, 1).wait()
        return c

    lax.fori_loop(0, bm, drain, 0)


def _dispatch(pos, seg, h, g, *, n_rows, n_experts, be):
    T, D = h.shape
    bm = _largest_divisor(T, (256, 128, 64, 32, 16))
    body = functools.partial(_dispatch_body, bm=bm, T=T, n_experts=n_experts, be=be, n_rows=n_rows)
    return pl.pallas_call(
        body,
        grid_spec=pltpu.PrefetchScalarGridSpec(
            num_scalar_prefetch=2,
            grid=(T // bm,),
            in_specs=[pl.BlockSpec((bm, D), lambda i, pos, seg: (i, 0)),
                      pl.BlockSpec((1, D), lambda i, pos, seg: (0, 0))],
            out_specs=pl.BlockSpec(memory_space=pl.ANY),
            scratch_shapes=[pltpu.VMEM((bm, D), F32), pltpu.VMEM((ZERO_ROWS, D), F32), pltpu.SemaphoreType.DMA(())],
        ),
        out_shape=jax.ShapeDtypeStruct((n_rows, D), F32),
        compiler_params=_params("arbitrary"),
        name="moe_dispatch",
    )(pos, seg, h, g)


def _expert_body(blk_ref, x_ref, wg_ref, wu_ref, wd_ref, o_ref, xb_ref, *, n_blocks):
    j = pl.program_id(0)

    @pl.when(pl.program_id(1) == 0)
    def _():
        o_ref[...] = jnp.zeros_like(o_ref)

    @pl.when(j < blk_ref[n_blocks])
    def _():
        @pl.when(pl.program_id(1) == 0)
        def _():
            xb_ref[...] = x_ref[...].astype(BF16)

        xb = xb_ref[...]
        hid = jax.nn.silu(_dot(xb, wg_ref[...])) * _dot(xb, wu_ref[...])
        o_ref[...] += _dot(hid.astype(BF16), wd_ref[...])


def _expert_ffn(blk, xs, wg, wu, wd, *, be):
    n_rows, D = xs.shape
    F = wg.shape[2]
    n_blocks = n_rows // be
    fc = _largest_divisor(F, (512, 256, 128))
    nc = F // fc
    body = functools.partial(_expert_body, n_blocks=n_blocks)

    def row_blk(j, blk):
        return jnp.minimum(j, blk[n_blocks] - 1)

    def chunk(j, c, blk):
        return jnp.where(j < blk[n_blocks], c, nc - 1)

    return pl.pallas_call(
        body,
        grid_spec=pltpu.PrefetchScalarGridSpec(
            num_scalar_prefetch=1,
            grid=(n_blocks, nc),
            in_specs=[
                pl.BlockSpec((be, D), lambda j, c, blk: (row_blk(j, blk), 0)),
                pl.BlockSpec((None, D, fc), lambda j, c, blk: (blk[row_blk(j, blk)], 0, chunk(j, c, blk))),
                pl.BlockSpec((None, D, fc), lambda j, c, blk: (blk[row_blk(j, blk)], 0, chunk(j, c, blk))),
                pl.BlockSpec((None, fc, D), lambda j, c, blk: (blk[row_blk(j, blk)], chunk(j, c, blk), 0)),
            ],
            out_specs=pl.BlockSpec((be, D), lambda j, c, blk: (j, 0)),
            scratch_shapes=[pltpu.VMEM((be, D), BF16)],
        ),
        out_shape=jax.ShapeDtypeStruct((n_rows, D), F32),
        compiler_params=_params("arbitrary", "arbitrary"),
        name="moe_experts",
    )(blk, xs, wg, wu, wd)


def _combine_body(pos_ref, h_ref, route_ref, g_ref, y_hbm, o_ref, y1_ref, y2_ref, sem, *, bm, T):
    i = pl.program_id(0)

    def row_copy(r, k):
        src = pos_ref[k * T + i * bm + r]
        dst = y1_ref if k == 0 else y2_ref
        return pltpu.make_async_copy(y_hbm.at[pl.ds(src, 1), :], dst.at[pl.ds(r, 1), :], sem)

    def issue(r, c):
        row_copy(r, 0).start()
        row_copy(r, 1).start()
        return c

    lax.fori_loop(0, bm, issue, 0)

    def drain(r, c):
        row_copy(r, 0).wait()
        row_copy(r, 1).wait()
        return c

    lax.fori_loop(0, bm, drain, 0)

    route = route_ref[...]
    w1 = route[:, ROUTE_W1:ROUTE_W1 + 1]
    w2 = route[:, ROUTE_W2:ROUTE_W2 + 1]
    h = h_ref[...] + (w1 * y1_ref[...] + w2 * y2_ref[...])
    o_ref[...] = _rmsnorm(h, g_ref[...])


def _combine(pos, h, route, g, y):
    T, D = h.shape
    bm = _largest_divisor(T, (256, 128, 64, 32, 16))
    body = functools.partial(_combine_body, bm=bm, T=T)
    return pl.pallas_call(
        body,
        grid_spec=pltpu.PrefetchScalarGridSpec(
            num_scalar_prefetch=1,
            grid=(T // bm,),
            in_specs=[pl.BlockSpec((bm, D), lambda i, pos: (i, 0)),
                      pl.BlockSpec((bm, LANES), lambda i, pos: (i, 0)),
                      pl.BlockSpec((1, D), lambda i, pos: (0, 0)),
                      pl.BlockSpec(memory_space=pl.ANY)],
            out_specs=pl.BlockSpec((bm, D), lambda i, pos: (i, 0)),
            scratch_shapes=[pltpu.VMEM((bm, D), F32), pltpu.VMEM((bm, D), F32), pltpu.SemaphoreType.DMA(())],
        ),
        out_shape=jax.ShapeDtypeStruct((T, D), F32),
        compiler_params=_params("arbitrary"),
        name="moe_combine",
    )(pos, h, route, g, y)


def _rope_tables(L):
    half = HEAD_DIM // 2
    inv = ROPE_THETA ** (-2.0 * jnp.arange(half, dtype=F32) / HEAD_DIM)
    ang = jnp.arange(L, dtype=F32)[:, None] * inv[None, :]
    cos = jnp.tile(jnp.cos(ang), (1, LANES // half))
    sin = jnp.sin(ang)
    sin = jnp.tile(jnp.concatenate([-sin, sin], axis=-1), (1, LANES // HEAD_DIM))
    return cos, sin


def _token_mixer(h, layer, B, L, w_in, b_gate, lam, subln_g, pool_w, pool_scale, w_ba, w_bp, w_out, norm_g, rope):
    T, D = h.shape
    A, P = w_ba.shape[0], w_bp.shape[0]
    n_heads = A // (2 * HEAD_DIM)
    lambda_init = 0.8 - 0.6 * math.exp(-0.3 * layer)
    z = _inproj(h, norm_g, w_in)
    z3 = z.reshape(B, L, z.shape[1])
    a = _attention(z3, lam, rope[0], rope[1], subln_g, n_heads=n_heads, lambda_init=lambda_init)
    p = _pool(z3, pool_w, pool_scale, col_block=(3 * A) // P)
    return _mix(a.reshape(T, A), p.reshape(T, P), z, b_gate, w_ba, w_bp, w_out, h,
                gate_col_block=(3 * A + P) // (2 * D))


def _moe_and_final_norm(h, norm_g, router, wg, wu, wd, final_g):
    T, D = h.shape
    E = router.shape[1]
    be = 512 if T >= 4096 else 64
    r_pad = jnp.zeros((D, LANES), F32).at[:, :E].set(router)
    r_hi = r_pad.astype(BF16)
    r_lo = (r_pad - r_hi.astype(F32)).astype(BF16)
    route, cnt = _router(h, norm_g, r_hi, r_lo, n_experts=E)

    counts = cnt[0, :E].astype(jnp.int32)
    n_blk = (counts + be - 1) // be
    seg_end = jnp.cumsum(n_blk * be)
    seg_start = seg_end - n_blk * be
    n_blocks = (TOP_K * T + E * (be - 1)) // be
    blk_end = jnp.cumsum(n_blk)
    blk_expert = jnp.minimum(jnp.sum(jnp.arange(n_blocks)[:, None] >= blk_end[None, :], axis=1), E - 1)
    blk = jnp.concatenate([blk_expert.astype(jnp.int32), blk_end[-1:].astype(jnp.int32)])
    seg = jnp.concatenate([seg_end, n_blk]).astype(jnp.int32)

    i1 = route[:, ROUTE_I1].astype(jnp.int32)
    i2 = route[:, ROUTE_I2].astype(jnp.int32)
    pos = jnp.concatenate([seg_start[i1] + route[:, ROUTE_R1].astype(jnp.int32),
                           seg_start[i2] + route[:, ROUTE_R2].astype(jnp.int32)])

    xs = _dispatch(pos, seg, h, norm_g, n_rows=n_blocks * be, n_experts=E, be=be)
    y = _expert_ffn(blk, xs, wg, wu, wd, be=be)
    return _combine(pos, h, route, final_g, y)


def kernel(x, meta_tokens, norm_mix, w_in, b_gate, lambda_q1, lambda_k1, lambda_q2, lambda_k2, subln, pool_w, pool_scale, w_branch_attn, w_branch_pool, w_out, norm_ffn, dense_w_gate, dense_w_up, dense_w_down, router, moe_w_gate, moe_w_up, moe_w_down, norm_final):
    B, S, D = x.shape
    depth = w_in.shape[0]
    assert depth == 2, "the final RMSNorm is fused into the routed layer, which must be the last one"
    L = N_META + S
    T = B * L
    meta = jnp.broadcast_to(meta_tokens[None].astype(x.dtype), (B, N_META, D))
    h = jnp.concatenate([meta, x], axis=1).reshape(T, D)
    rope = _rope_tables(L)
    row = lambda v: v.reshape(1, -1)

    for i in range(depth):
        lam = jnp.stack([lambda_q1[i], lambda_k1[i], lambda_q2[i], lambda_k2[i]])
        h = _token_mixer(h, i, B, L, w_in[i].astype(BF16), row(b_gate[i]), lam, row(subln[i]),
                         pool_w[i].astype(BF16), row(pool_scale[i]), w_branch_attn[i].astype(BF16),
                         w_branch_pool[i].astype(BF16), w_out[i].astype(BF16), row(norm_mix[i]), rope)
        j = i // 2
        if i % 2 == 0:
            h = _dense_ffn(h, row(norm_ffn[i]), dense_w_gate[j].astype(BF16), dense_w_up[j].astype(BF16),
                           dense_w_down[j].astype(BF16))
        else:
            h = _moe_and_final_norm(h, row(norm_ffn[i]), router[j], moe_w_gate[j].astype(BF16),
                                    moe_w_up[j].astype(BF16), moe_w_down[j].astype(BF16), row(norm_final))
    return h.reshape(B, L, D)[:, N_META:]
```

```python
import functools
import math

import jax
import jax.numpy as jnp
from jax import lax
from jax.experimental import pallas as pl
from jax.experimental.pallas import tpu as pltpu

N_META = 16
HEAD_DIM = 64
POOL_WINDOWS = (2, 4, 8, 16)
TOP_K = 2
ROPE_THETA = 10000.0
EPS = 1e-5

LANES = 128
SUBLANES_BF16 = 16
VMEM_CAP_BYTES = 64 * 1024 * 1024
VMEM_LIMIT_BYTES = VMEM_CAP_BYTES - 6 * 1024 * 1024

NEG_BIG = -1e30
LOG2_E = 1.4426950408889634
POOL_PAD = max(POOL_WINDOWS)

F32 = jnp.float32
BF16 = jnp.bfloat16


def _largest_divisor(n, candidates):
    for c in candidates:
        if c <= n and n % c == 0:
            return c
    raise ValueError(f"no block size in {candidates} divides {n}")


def _params(*semantics):
    return pltpu.CompilerParams(dimension_semantics=semantics, vmem_limit_bytes=VMEM_LIMIT_BYTES)


def _rmsnorm(x, g):
    ms = jnp.mean(x * x, axis=-1, keepdims=True)
    return x * lax.rsqrt(ms + EPS) * g


def _dot(a, b):
    return jnp.dot(a, b, preferred_element_type=F32)


def _inproj_body(x_ref, g_ref, w_ref, cs_ref, o_ref, xn_ref, *, bn):
    @pl.when(pl.program_id(1) == 0)
    def _():
        xn_ref[...] = _rmsnorm(x_ref[...], g_ref[...]).astype(BF16)

    acc = _dot(xn_ref[...], w_ref[...])
    cos, sin = cs_ref[0], cs_ref[1]
    for c in range(0, bn, 2 * LANES):
        t1 = acc[:, c:c + LANES]
        t2 = acc[:, c + LANES:c + 2 * LANES]
        o_ref[:, c:c + LANES] = (t1 * cos - t2 * sin).astype(o_ref.dtype)
        o_ref[:, c + LANES:c + 2 * LANES] = (t2 * cos + t1 * sin).astype(o_ref.dtype)


def _inproj(h, g, w, rope_cs, *, attn_width):
    T, D = h.shape
    N = w.shape[1]
    bm = _largest_divisor(T, (768, 512, 384, 256, 128, 64, 32, 16))
    bn = _largest_divisor(math.gcd(N, attn_width), (1024, 512, 256))
    qk_blocks = attn_width // bn
    body = functools.partial(_inproj_body, bn=bn)
    return pl.pallas_call(
        body,
        grid=(T // bm, N // bn),
        in_specs=[
            pl.BlockSpec((bm, D), lambda i, j: (i, 0)),
            pl.BlockSpec((1, D), lambda i, j: (0, 0)),
            pl.BlockSpec((D, bn), lambda i, j: (0, j)),
            pl.BlockSpec((None, 2, bm, LANES), lambda i, j: (jnp.minimum(j // qk_blocks, 2), 0, i, 0)),
        ],
        out_specs=pl.BlockSpec((bm, bn), lambda i, j: (i, j)),
        out_shape=jax.ShapeDtypeStruct((T, N), BF16),
        scratch_shapes=[pltpu.VMEM((bm, D), BF16)],
        compiler_params=_params("parallel", "arbitrary"),
        name="inproj",
    )(h, g, w, rope_cs)


PAIR = 2 * LANES


def _attn_body(lam_ref, q_ref, k_ref, v_ref, sg_ref, o_ref, kp_ref, vp_ref, of_ref,
               *, L, l_main, l_keys, bq, lambda_init):
    head_in_pair = pl.program_id(1) % 2
    lane = lax.broadcasted_iota(jnp.int32, (1, PAIR), 1)
    in_head = ((lane % LANES) // HEAD_DIM) == head_in_pair
    map_of_lane = (lane % HEAD_DIM) // (HEAD_DIM // 2)
    map_masks = (in_head & (map_of_lane == 0), in_head & (map_of_lane == 1))

    kp_ref[pl.ds(0, L), :] = k_ref[...]
    vp_ref[pl.ds(0, L), pl.ds(0, LANES)] = v_ref[...]
    vp_ref[:, pl.ds(LANES, LANES)] = jnp.ones((l_keys, LANES), BF16)
    if l_keys > L:
        kp_ref[pl.ds(L, l_keys - L), :] = jnp.zeros((l_keys - L, PAIR), BF16)
        vp_ref[pl.ds(L, l_keys - L), pl.ds(0, LANES)] = jnp.zeros((l_keys - L, LANES), BF16)

    lam = lam_ref[...]
    lam_full = (jnp.exp(jnp.sum(lam[0:1] * lam[1:2], axis=-1, keepdims=True))
                - jnp.exp(jnp.sum(lam[2:3] * lam[3:4], axis=-1, keepdims=True)) + lambda_init)

    tail_valid = (lax.broadcasted_iota(jnp.int32, (1, LANES), 1) + l_main) < L
    contract_last = (((1,), (1,)), ((), ()))
    zero = jnp.zeros((), BF16)
    q_all = q_ref[...]

    for i in range(L // bq):
        qb = q_all[i * bq:(i + 1) * bq]
        q_maps = jnp.concatenate([jnp.where(m, qb, zero) for m in map_masks], axis=0)
        s = lax.dot_general(q_maps, kp_ref[...], contract_last, preferred_element_type=F32)
        s_main = s[:, :l_main]
        m = jnp.max(s_main, axis=-1, keepdims=True)
        if l_keys > l_main:
            s_tail = jnp.where(tail_valid, s[:, l_main:], NEG_BIG)
            m = jnp.maximum(m, jnp.max(s_tail, axis=-1, keepdims=True))
        acc = _dot(jnp.exp2(s_main - m).astype(BF16), vp_ref[pl.ds(0, l_main), :])
        if l_keys > l_main:
            acc = acc + _dot(jnp.exp2(s_tail - m).astype(BF16), vp_ref[pl.ds(l_main, l_keys - l_main), :])
        a0, a1 = acc[:bq], acc[bq:]
        o = a0[:, :LANES] * (1.0 / a0[:, LANES:LANES + 1]) - a1[:, :LANES] * (lam_full / a1[:, LANES:LANES + 1])
        of_ref[pl.ds(i * bq, bq), :] = _rmsnorm(o, sg_ref[...]) * (1.0 - lambda_init)
    o_ref[...] = of_ref[...].astype(o_ref.dtype)


def _attention(z3, lam, subln_g, *, n_heads, lambda_init):
    B, L, _ = z3.shape
    l_main = (L // LANES) * LANES
    l_keys = l_main if l_main == L else l_main + LANES
    bq = _largest_divisor(L, tuple(b for b in range(400, 7, -8)))
    body = functools.partial(_attn_body, L=L, l_main=l_main, l_keys=l_keys, bq=bq, lambda_init=lambda_init)
    pair = lambda off: pl.BlockSpec((None, L, PAIR), lambda b, h: (b, 0, off + h // 2))
    const = lambda shape: pl.BlockSpec(shape, lambda b, h: (0,) * len(shape))
    return pl.pallas_call(
        body,
        grid=(B, n_heads),
        in_specs=[const(lam.shape), pair(0), pair(n_heads // 2),
                  pl.BlockSpec((None, L, LANES), lambda b, h: (b, 0, 2 * n_heads + h)), const(subln_g.shape)],
        out_specs=pl.BlockSpec((None, L, LANES), lambda b, h: (b, 0, h)),
        out_shape=jax.ShapeDtypeStruct((B, L, n_heads * LANES), BF16),
        scratch_shapes=[pltpu.VMEM((l_keys, PAIR), BF16), pltpu.VMEM((l_keys, PAIR), BF16),
                        pltpu.VMEM((L, LANES), F32)],
        compiler_params=_params("parallel", "parallel"),
        name="diff_attention",
    )(lam, z3, z3, z3, subln_g)


def _pool_body(u_ref, pw_ref, ps_ref, o_ref, pad_ref, *, L, cg):
    t = lax.broadcasted_iota(jnp.int32, (L, 1), 0)
    zeros = jnp.zeros((POOL_PAD, cg), F32)
    pad_ref[pl.ds(0, POOL_PAD), :] = zeros
    pad_ref[pl.ds(POOL_PAD + L, POOL_PAD), :] = zeros
    for g, w in enumerate(POOL_WINDOWS):
        cols = slice(g * cg, (g + 1) * cg)
        ug = u_ref[:, cols].astype(F32)
        pad_ref[pl.ds(POOL_PAD, L), :] = ug
        start = POOL_PAD - w // 2
        win = pad_ref[pl.ds(start, L), :]
        for j in range(1, w):
            win = win + pad_ref[pl.ds(start + j, L), :]
        cnt = (jnp.minimum(t + w // 2, L) - jnp.maximum(t - w // 2, 0)).astype(F32)
        m = win / cnt - ug
        y = _dot(m.astype(BF16), pw_ref[g]) * ps_ref[:, cols]
        o_ref[:, cols] = y.astype(o_ref.dtype)


def _pool(z3, pool_w, pool_scale, *, col_block):
    B, L, _ = z3.shape
    G, cg, _ = pool_w.shape
    P = G * cg
    body = functools.partial(_pool_body, L=L, cg=cg)
    return pl.pallas_call(
        body,
        grid=(B,),
        in_specs=[
            pl.BlockSpec((None, L, P), lambda b: (b, 0, col_block)),
            pl.BlockSpec((G, cg, cg), lambda b: (0, 0, 0)),
            pl.BlockSpec((1, P), lambda b: (0, 0)),
        ],
        out_specs=pl.BlockSpec((None, L, P), lambda b: (b, 0, 0)),
        out_shape=jax.ShapeDtypeStruct((B, L, P), BF16),
        scratch_shapes=[pltpu.VMEM((L + 2 * POOL_PAD, cg), F32)],
        compiler_params=_params("parallel"),
        name="pool_mixer",
    )(z3, pool_w, pool_scale)


def _mix_body(a_ref, p_ref, gt_ref, b_ref, wa_ref, wp_ref, wo_ref, h_ref, o_ref, *, D):
    gates = jax.nn.sigmoid(gt_ref[...].astype(F32) + b_ref[...])
    y = gates[:, :D] * _dot(a_ref[...], wa_ref[...]) + gates[:, D:] * _dot(p_ref[...], wp_ref[...])
    o_ref[...] = h_ref[...] + _dot(y.astype(BF16), wo_ref[...])


def _mix(a, p, z, b_gate, wa, wp, wo, h, *, gate_col_block):
    T, D = h.shape
    A, P = a.shape[1], p.shape[1]
    bm = _largest_divisor(T, (256, 128, 64, 32, 16))
    body = functools.partial(_mix_body, D=D)
    const = lambda shape: pl.BlockSpec(shape, lambda i: (0, 0))
    return pl.pallas_call(
        body,
        grid=(T // bm,),
        in_specs=[
            pl.BlockSpec((bm, A), lambda i: (i, 0)),
            pl.BlockSpec((bm, P), lambda i: (i, 0)),
            pl.BlockSpec((bm, 2 * D), lambda i: (i, gate_col_block)),
            const((1, 2 * D)), const((A, D)), const((P, D)), const((D, D)),
            pl.BlockSpec((bm, D), lambda i: (i, 0)),
        ],
        out_specs=pl.BlockSpec((bm, D), lambda i: (i, 0)),
        out_shape=jax.ShapeDtypeStruct((T, D), F32),
        compiler_params=_params("parallel"),
        name="mix_out",
    )(a, p, z, b_gate, wa, wp, wo, h)


def _ffn_body(x_ref, g_ref, wg_ref, wu_ref, wd_ref, o_ref, xn_ref):
    @pl.when(pl.program_id(1) == 0)
    def _():
        x = x_ref[...]
        xn_ref[...] = _rmsnorm(x, g_ref[...]).astype(BF16)
        o_ref[...] = x

    xn = xn_ref[...]
    hid = jax.nn.silu(_dot(xn, wg_ref[...])) * _dot(xn, wu_ref[...])
    o_ref[...] += _dot(hid.astype(BF16), wd_ref[...])


def _dense_ffn(h, g, wg, wu, wd):
    T, D = h.shape
    F = wg.shape[1]
    bm = _largest_divisor(T, (768, 512, 384, 256, 128, 64, 32, 16))
    fc = _largest_divisor(F, (512, 256, 128))
    return pl.pallas_call(
        _ffn_body,
        grid=(T // bm, F // fc),
        in_specs=[
            pl.BlockSpec((bm, D), lambda i, c: (i, 0)),
            pl.BlockSpec((1, D), lambda i, c: (0, 0)),
            pl.BlockSpec((D, fc), lambda i, c: (0, c)),
            pl.BlockSpec((D, fc), lambda i, c: (0, c)),
            pl.BlockSpec((fc, D), lambda i, c: (c, 0)),
        ],
        out_specs=pl.BlockSpec((bm, D), lambda i, c: (i, 0)),
        out_shape=jax.ShapeDtypeStruct((T, D), F32),
        scratch_shapes=[pltpu.VMEM((bm, D), BF16)],
        compiler_params=_params("parallel", "arbitrary"),
        name="dense_ffn",
    )(h, g, wg, wu, wd)


ROUTE_I1, ROUTE_I2, ROUTE_W1, ROUTE_W2, ROUTE_R1, ROUTE_R2 = range(6)


def _router_body(x_ref, g_ref, rhi_ref, rlo_ref, tri_ref, route_ref, cnt_ref, carry_ref, *, n_experts):
    @pl.when(pl.program_id(0) == 0)
    def _():
        carry_ref[...] = jnp.zeros_like(carry_ref)

    xn = _rmsnorm(x_ref[...], g_ref[...])
    x_hi = xn.astype(BF16)
    x_lo = (xn - x_hi.astype(F32)).astype(BF16)
    logits = _dot(x_hi, rhi_ref[...]) + (_dot(x_lo, rhi_ref[...]) + _dot(x_hi, rlo_ref[...]))

    lane = lax.broadcasted_iota(jnp.int32, (1, LANES), 1)
    lane_f = lane.astype(F32)
    lg = jnp.where(lane < n_experts, logits, NEG_BIG)
    v1 = jnp.max(lg, axis=-1, keepdims=True)
    i1 = jnp.min(jnp.where(lg == v1, lane_f, float(LANES)), axis=-1, keepdims=True)
    lg2 = jnp.where(lane_f == i1, NEG_BIG, lg)
    v2 = jnp.max(lg2, axis=-1, keepdims=True)
    i2 = jnp.min(jnp.where(lg2 == v2, lane_f, float(LANES)), axis=-1, keepdims=True)
    e2 = jnp.exp(v2 - v1)
    w1 = 1.0 / (1.0 + e2)
    w2 = e2 / (1.0 + e2)

    sel1 = lane_f == i1
    sel2 = lane_f == i2
    onehot = jnp.where(sel1 | sel2, 1.0, 0.0)
    rank = _dot(tri_ref[...], onehot.astype(BF16)) + carry_ref[...]
    r1 = jnp.sum(jnp.where(sel1, rank, 0.0), axis=-1, keepdims=True)
    r2 = jnp.sum(jnp.where(sel2, rank, 0.0), axis=-1, keepdims=True)
    carry_ref[...] += jnp.sum(onehot, axis=0, keepdims=True)

    route = jnp.zeros(route_ref.shape, F32)
    for col, val in ((ROUTE_I1, i1), (ROUTE_I2, i2), (ROUTE_W1, w1), (ROUTE_W2, w2), (ROUTE_R1, r1), (ROUTE_R2, r2)):
        route = jnp.where(lane == col, val, route)
    route_ref[...] = route
    cnt_ref[...] = jnp.broadcast_to(carry_ref[...], cnt_ref.shape)


def _router(h, g, r_hi, r_lo, *, n_experts):
    T, D = h.shape
    bm = _largest_divisor(T, (256, 128, 64, 32, 16))
    tri = (lax.broadcasted_iota(jnp.int32, (bm, bm), 0) > lax.broadcasted_iota(jnp.int32, (bm, bm), 1)).astype(BF16)
    body = functools.partial(_router_body, n_experts=n_experts)
    const = lambda shape: pl.BlockSpec(shape, lambda i: (0, 0))
    return pl.pallas_call(
        body,
        grid=(T // bm,),
        in_specs=[pl.BlockSpec((bm, D), lambda i: (i, 0)), const((1, D)), const((D, LANES)), const((D, LANES)),
                  const((bm, bm))],
        out_specs=[pl.BlockSpec((bm, LANES), lambda i: (i, 0)), const((8, LANES))],
        out_shape=[jax.ShapeDtypeStruct((T, LANES), F32), jax.ShapeDtypeStruct((8, LANES), F32)],
        scratch_shapes=[pltpu.VMEM((1, LANES), F32)],
        compiler_params=_params("arbitrary"),
        name="moe_router",
    )(h, g, r_hi, r_lo, tri)


ZERO_ROWS = 64
ROW_DMA_UNROLL = 8


def _dispatch_body(pos_ref, seg_ref, x_ref, g_ref, xs_hbm, xn_ref, zero_ref, sem, *, bm, T, n_experts, be, n_rows):
    i = pl.program_id(0)

    def zero_copy(first_row, c):
        row = pl.multiple_of(first_row + c * ZERO_ROWS, ZERO_ROWS)
        return pltpu.make_async_copy(zero_ref, xs_hbm.at[pl.ds(row, ZERO_ROWS), :], sem)

    def zero_block(first_row):
        for c in range(be // ZERO_ROWS):
            zero_copy(first_row, c).start()
        for c in range(be // ZERO_ROWS):
            zero_copy(first_row, c).wait()

    @pl.when(i == 0)
    def _():
        zero_ref[...] = jnp.zeros_like(zero_ref)
        for e in range(n_experts):
            @pl.when(seg_ref[n_experts + e] > 0)
            def _():
                zero_block(seg_ref[e] - be)

        def clear_unused(j, c):
            zero_block(j * be)
            return c

        lax.fori_loop(seg_ref[n_experts - 1] // be, n_rows // be, clear_unused, 0)

    xn_ref[...] = _rmsnorm(x_ref[...], g_ref[...])

    def row_copy(r, k):
        dst = pos_ref[k * T + i * bm + r]
        return pltpu.make_async_copy(xn_ref.at[pl.ds(r, 1), :], xs_hbm.at[pl.ds(dst, 1), :], sem)

    def issue(grp, c):
        for u in range(ROW_DMA_UNROLL):
            row_copy(grp * ROW_DMA_UNROLL + u, 0).start(priority=u % 2)
            row_copy(grp * ROW_DMA_UNROLL + u, 1).start(priority=(u + 1) % 2)
        return c

    lax.fori_loop(0, bm // ROW_DMA_UNROLL, issue, 0)
    for _ in range(TOP_K):
        pltpu.make_async_copy(xn_ref, xs_hbm.at[pl.ds(0, bm), :], sem).wait()


def _dispatch(pos, seg, h, g, *, n_rows, n_experts, be):
    T, D = h.shape
    bm = _largest_divisor(T, (256, 128, 64, 32, 16))
    body = functools.partial(_dispatch_body, bm=bm, T=T, n_experts=n_experts, be=be, n_rows=n_rows)
    return pl.pallas_call(
        body,
        grid_spec=pltpu.PrefetchScalarGridSpec(
            num_scalar_prefetch=2,
            grid=(T // bm,),
            in_specs=[pl.BlockSpec((bm, D), lambda i, pos, seg: (i, 0)),
                      pl.BlockSpec((1, D), lambda i, pos, seg: (0, 0))],
            out_specs=pl.BlockSpec(memory_space=pl.ANY),
            scratch_shapes=[pltpu.VMEM((bm, D), F32), pltpu.VMEM((ZERO_ROWS, D), F32), pltpu.SemaphoreType.DMA(())],
        ),
        out_shape=jax.ShapeDtypeStruct((n_rows, D), F32),
        compiler_params=_params("arbitrary"),
        name="moe_dispatch",
    )(pos, seg, h, g)


def _expert_body(blk_ref, x_ref, wg_ref, wu_ref, wd_ref, o_ref, xb_ref, *, n_blocks):
    j = pl.program_id(0)

    @pl.when(pl.program_id(1) == 0)
    def _():
        o_ref[...] = jnp.zeros_like(o_ref)

    @pl.when(j < blk_ref[n_blocks])
    def _():
        @pl.when(pl.program_id(1) == 0)
        def _():
            xb_ref[...] = x_ref[...].astype(BF16)

        xb = xb_ref[...]
        hid = jax.nn.silu(_dot(xb, wg_ref[...])) * _dot(xb, wu_ref[...])
        o_ref[...] += _dot(hid.astype(BF16), wd_ref[...])


def _expert_ffn(blk, xs, wg, wu, wd, *, be):
    n_rows, D = xs.shape
    F = wg.shape[2]
    n_blocks = n_rows // be
    fc = _largest_divisor(F, (512, 256, 128))
    nc = F // fc
    body = functools.partial(_expert_body, n_blocks=n_blocks)

    def row_blk(j, blk):
        return jnp.minimum(j, blk[n_blocks] - 1)

    def chunk(j, c, blk):
        return jnp.where(j < blk[n_blocks], c, nc - 1)

    return pl.pallas_call(
        body,
        grid_spec=pltpu.PrefetchScalarGridSpec(
            num_scalar_prefetch=1,
            grid=(n_blocks, nc),
            in_specs=[
                pl.BlockSpec((be, D), lambda j, c, blk: (row_blk(j, blk), 0)),
                pl.BlockSpec((None, D, fc), lambda j, c, blk: (blk[row_blk(j, blk)], 0, chunk(j, c, blk))),
                pl.BlockSpec((None, D, fc), lambda j, c, blk: (blk[row_blk(j, blk)], 0, chunk(j, c, blk))),
                pl.BlockSpec((None, fc, D), lambda j, c, blk: (blk[row_blk(j, blk)], chunk(j, c, blk), 0)),
            ],
            out_specs=pl.BlockSpec((be, D), lambda j, c, blk: (j, 0)),
            scratch_shapes=[pltpu.VMEM((be, D), BF16)],
        ),
        out_shape=jax.ShapeDtypeStruct((n_rows, D), F32),
        compiler_params=_params("arbitrary", "arbitrary"),
        name="moe_experts",
    )(blk, xs, wg, wu, wd)


def _combine_body(pos_ref, h_ref, route_ref, g_ref, y_hbm, o_ref, y1_ref, y2_ref, sem, *, bm, T):
    i = pl.program_id(0)

    def row_copy(r, k):
        src = pos_ref[k * T + i * bm + r]
        dst = y1_ref if k == 0 else y2_ref
        return pltpu.make_async_copy(y_hbm.at[pl.ds(src, 1), :], dst.at[pl.ds(r, 1), :], sem)

    def issue(grp, c):
        for u in range(ROW_DMA_UNROLL):
            row_copy(grp * ROW_DMA_UNROLL + u, 0).start(priority=u % 2)
            row_copy(grp * ROW_DMA_UNROLL + u, 1).start(priority=(u + 1) % 2)
        return c

    lax.fori_loop(0, bm // ROW_DMA_UNROLL, issue, 0)
    pltpu.make_async_copy(y_hbm.at[pl.ds(0, bm), :], y1_ref, sem).wait()
    pltpu.make_async_copy(y_hbm.at[pl.ds(0, bm), :], y2_ref, sem).wait()

    route = route_ref[...]
    w1 = route[:, ROUTE_W1:ROUTE_W1 + 1]
    w2 = route[:, ROUTE_W2:ROUTE_W2 + 1]
    h = h_ref[...] + (w1 * y1_ref[...] + w2 * y2_ref[...])
    o_ref[...] = _rmsnorm(h, g_ref[...])


def _combine(pos, h, route, g, y):
    T, D = h.shape
    bm = _largest_divisor(T, (256, 128, 64, 32, 16))
    body = functools.partial(_combine_body, bm=bm, T=T)
    return pl.pallas_call(
        body,
        grid_spec=pltpu.PrefetchScalarGridSpec(
            num_scalar_prefetch=1,
            grid=(T // bm,),
            in_specs=[pl.BlockSpec((bm, D), lambda i, pos: (i, 0)),
                      pl.BlockSpec((bm, LANES), lambda i, pos: (i, 0)),
                      pl.BlockSpec((1, D), lambda i, pos: (0, 0)),
                      pl.BlockSpec(memory_space=pl.ANY)],
            out_specs=pl.BlockSpec((bm, D), lambda i, pos: (i, 0)),
            scratch_shapes=[pltpu.VMEM((bm, D), F32), pltpu.VMEM((bm, D), F32), pltpu.SemaphoreType.DMA(())],
        ),
        out_shape=jax.ShapeDtypeStruct((T, D), F32),
        compiler_params=_params("arbitrary"),
        name="moe_combine",
    )(pos, h, route, g, y)


def _rope_tables(B, L):
    half = HEAD_DIM // 2
    inv = ROPE_THETA ** (-2.0 * jnp.arange(half, dtype=F32) / HEAD_DIM)
    ang = jnp.arange(L, dtype=F32)[:, None] * inv[None, :]
    cs = jnp.stack([jnp.cos(ang), jnp.sin(ang)])
    cs = jnp.tile(cs, (1, B, LANES // half))
    identity = jnp.stack([jnp.ones_like(cs[0]), jnp.zeros_like(cs[0])])
    return jnp.stack([cs * (HEAD_DIM ** -0.5 * LOG2_E), cs, identity])


def _pair_rotary_halves(w, n_heads):
    D = w.shape[0]
    w = w.reshape(D, n_heads // 2, 2, 2, 2, HEAD_DIM // 2)
    return w.transpose(0, 1, 4, 2, 3, 5).reshape(D, -1)


def _token_mixer(h, layer, B, L, w_in, b_gate, lam, subln_g, pool_w, pool_scale, w_ba, w_bp, w_out, norm_g, rope_cs):
    T, D = h.shape
    A, P = w_ba.shape[0], w_bp.shape[0]
    n_heads = A // (2 * HEAD_DIM)
    lambda_init = 0.8 - 0.6 * math.exp(-0.3 * layer)
    w_in = jnp.concatenate([_pair_rotary_halves(w_in[:, :A], n_heads), _pair_rotary_halves(w_in[:, A:2 * A], n_heads),
                            w_in[:, 2 * A:]], axis=1).astype(BF16)
    z = _inproj(h, norm_g, w_in, rope_cs, attn_width=A)
    z3 = z.reshape(B, L, z.shape[1])
    a = _attention(z3, lam, subln_g, n_heads=n_heads, lambda_init=lambda_init)
    p = _pool(z3, pool_w, pool_scale, col_block=(3 * A) // P)
    return _mix(a.reshape(T, A), p.reshape(T, P), z, b_gate, w_ba, w_bp, w_out, h,
                gate_col_block=(3 * A + P) // (2 * D))


def _moe_and_final_norm(h, norm_g, router, wg, wu, wd, final_g):
    T, D = h.shape
    E = router.shape[1]
    be = 512 if T >= 4096 else 64
    r_pad = jnp.zeros((D, LANES), F32).at[:, :E].set(router)
    r_hi = r_pad.astype(BF16)
    r_lo = (r_pad - r_hi.astype(F32)).astype(BF16)
    route, cnt = _router(h, norm_g, r_hi, r_lo, n_experts=E)

    counts = cnt[0, :E].astype(jnp.int32)
    n_blk = (counts + be - 1) // be
    seg_end = jnp.cumsum(n_blk * be)
    seg_start = seg_end - n_blk * be
    n_blocks = (TOP_K * T + E * (be - 1)) // be
    blk_end = jnp.cumsum(n_blk)
    blk_expert = jnp.minimum(jnp.sum(jnp.arange(n_blocks)[:, None] >= blk_end[None, :], axis=1), E - 1)
    blk = jnp.concatenate([blk_expert.astype(jnp.int32), blk_end[-1:].astype(jnp.int32)])
    seg = jnp.concatenate([seg_end, n_blk]).astype(jnp.int32)

    i1 = route[:, ROUTE_I1].astype(jnp.int32)
    i2 = route[:, ROUTE_I2].astype(jnp.int32)
    pos = jnp.concatenate([seg_start[i1] + route[:, ROUTE_R1].astype(jnp.int32),
                           seg_start[i2] + route[:, ROUTE_R2].astype(jnp.int32)])

    xs = _dispatch(pos, seg, h, norm_g, n_rows=n_blocks * be, n_experts=E, be=be)
    y = _expert_ffn(blk, xs, wg, wu, wd, be=be)
    return _combine(pos, h, route, final_g, y)


def kernel(x, meta_tokens, norm_mix, w_in, b_gate, lambda_q1, lambda_k1, lambda_q2, lambda_k2, subln, pool_w, pool_scale, w_branch_attn, w_branch_pool, w_out, norm_ffn, dense_w_gate, dense_w_up, dense_w_down, router, moe_w_gate, moe_w_up, moe_w_down, norm_final):
    B, S, D = x.shape
    depth = w_in.shape[0]
    assert depth == 2, "the final RMSNorm is fused into the routed layer, which must be the last one"
    L = N_META + S
    T = B * L
    meta = jnp.broadcast_to(meta_tokens[None].astype(x.dtype), (B, N_META, D))
    h = jnp.concatenate([meta, x], axis=1).reshape(T, D)
    rope = _rope_tables(B, L)
    row = lambda v: v.reshape(1, -1)

    for i in range(depth):
        lam = jnp.stack([lambda_q1[i], lambda_k1[i], lambda_q2[i], lambda_k2[i]])
        h = _token_mixer(h, i, B, L, w_in[i], row(b_gate[i]), lam, row(subln[i]),
                         pool_w[i].astype(BF16), row(pool_scale[i]), w_branch_attn[i].astype(BF16),
                         w_branch_pool[i].astype(BF16), w_out[i].astype(BF16), row(norm_mix[i]), rope)
        j = i // 2
        if i % 2 == 0:
            h = _dense_ffn(h, row(norm_ffn[i]), dense_w_gate[j].astype(BF16), dense_w_up[j].astype(BF16),
                           dense_w_down[j].astype(BF16))
        else:
            h = _moe_and_final_norm(h, row(norm_ffn[i]), router[j], moe_w_gate[j].astype(BF16),
                                    moe_w_up[j].astype(BF16), moe_w_down[j].astype(BF16), row(norm_final))
    return h.reshape(B, L, D)[:, N_META:]
```

```python
import functools
import math

import jax
import jax.numpy as jnp
from jax import lax
from jax.experimental import pallas as pl
from jax.experimental.pallas import tpu as pltpu

N_META = 16
HEAD_DIM = 64
POOL_WINDOWS = (2, 4, 8, 16)
TOP_K = 2
ROPE_THETA = 10000.0
EPS = 1e-5

LANES = 128
SUBLANES_BF16 = 16
VMEM_CAP_BYTES = 64 * 1024 * 1024
VMEM_LIMIT_BYTES = VMEM_CAP_BYTES - 6 * 1024 * 1024

NEG_BIG = -1e30
LOG2_E = 1.4426950408889634
POOL_PAD = max(POOL_WINDOWS)

F32 = jnp.float32
BF16 = jnp.bfloat16


def _largest_divisor(n, candidates):
    for c in candidates:
        if c <= n and n % c == 0:
            return c
    raise ValueError(f"no block size in {candidates} divides {n}")


def _params(*semantics):
    return pltpu.CompilerParams(dimension_semantics=semantics, vmem_limit_bytes=VMEM_LIMIT_BYTES)


def _rmsnorm(x, g):
    ms = jnp.mean(x * x, axis=-1, keepdims=True)
    return x * lax.rsqrt(ms + EPS) * g


def _dot(a, b):
    return jnp.dot(a, b, preferred_element_type=F32)


def _inproj_body(x_ref, g_ref, wqk_ref, wr_ref, cs_ref, o_ref, xn_ref, *, bn, n_qk_blocks):
    j = pl.program_id(1)

    @pl.when(j == 0)
    def _():
        xn_ref[...] = _rmsnorm(x_ref[...], g_ref[...]).astype(BF16)

    @pl.when(j < n_qk_blocks)
    def _():
        acc = _dot(xn_ref[...], wqk_ref[...])
        cos, sin = cs_ref[0], cs_ref[1]
        for c in range(0, bn, 2 * LANES):
            t1 = acc[:, c:c + LANES]
            t2 = acc[:, c + LANES:c + 2 * LANES]
            o_ref[:, c:c + LANES] = (t1 * cos - t2 * sin).astype(o_ref.dtype)
            o_ref[:, c + LANES:c + 2 * LANES] = (t2 * cos + t1 * sin).astype(o_ref.dtype)

    @pl.when(j >= n_qk_blocks)
    def _():
        o_ref[...] = _dot(xn_ref[...], wr_ref[...]).astype(o_ref.dtype)


def _inproj(h, g, w_qk, w_rest, rope_cs, *, L):
    T, D = h.shape
    n_qk, n_rest = w_qk.shape[1], w_rest.shape[1]
    bm = _largest_divisor(L, tuple(b for b in range(768, 15, -16)))
    bn = _largest_divisor(math.gcd(n_qk // 2, n_rest), (1024, 512, 256))
    n_qk_blocks = n_qk // bn
    q_blocks = n_qk_blocks // 2
    blocks_per_seq = L // bm
    body = functools.partial(_inproj_body, bn=bn, n_qk_blocks=n_qk_blocks)
    return pl.pallas_call(
        body,
        grid=(T // bm, (n_qk + n_rest) // bn),
        in_specs=[
            pl.BlockSpec((bm, D), lambda i, j: (i, 0)),
            pl.BlockSpec((1, D), lambda i, j: (0, 0)),
            pl.BlockSpec((D, bn), lambda i, j: (0, jnp.minimum(j, n_qk_blocks - 1))),
            pl.BlockSpec((D, bn), lambda i, j: (0, jnp.maximum(j - n_qk_blocks, 0))),
            pl.BlockSpec((None, 2, bm, LANES),
                         lambda i, j: (jnp.minimum(j // q_blocks, 1), 0, i % blocks_per_seq, 0)),
        ],
        out_specs=pl.BlockSpec((bm, bn), lambda i, j: (i, j)),
        out_shape=jax.ShapeDtypeStruct((T, n_qk + n_rest), BF16),
        scratch_shapes=[pltpu.VMEM((bm, D), BF16)],
        compiler_params=_params("parallel", "arbitrary"),
        name="inproj",
    )(h, g, w_qk, w_rest, rope_cs)


PAIR = 2 * LANES


def _attn_body(lam_ref, q_ref, k_ref, v_ref, sg_ref, o_ref, kp_ref, vp_ref, of_ref,
               *, L, l_main, l_keys, bq, lambda_init):
    head_in_pair = pl.program_id(1) % 2
    lane = lax.broadcasted_iota(jnp.int32, (1, PAIR), 1)
    in_head = ((lane % LANES) // HEAD_DIM) == head_in_pair
    map_of_lane = (lane % HEAD_DIM) // (HEAD_DIM // 2)
    map_masks = (in_head & (map_of_lane == 0), in_head & (map_of_lane == 1))

    kp_ref[pl.ds(0, L), :] = k_ref[...]
    vp_ref[pl.ds(0, L), pl.ds(0, LANES)] = v_ref[...]
    vp_ref[:, pl.ds(LANES, LANES)] = jnp.ones((l_keys, LANES), BF16)
    if l_keys > L:
        kp_ref[pl.ds(L, l_keys - L), :] = jnp.zeros((l_keys - L, PAIR), BF16)
        vp_ref[pl.ds(L, l_keys - L), pl.ds(0, LANES)] = jnp.zeros((l_keys - L, LANES), BF16)

    lam = lam_ref[...]
    lam_full = (jnp.exp(jnp.sum(lam[0:1] * lam[1:2], axis=-1, keepdims=True))
                - jnp.exp(jnp.sum(lam[2:3] * lam[3:4], axis=-1, keepdims=True)) + lambda_init)

    tail_valid = (lax.broadcasted_iota(jnp.int32, (1, LANES), 1) + l_main) < L
    contract_last = (((1,), (1,)), ((), ()))
    zero = jnp.zeros((), BF16)
    q_all = q_ref[...]

    for i in range(L // bq):
        qb = q_all[i * bq:(i + 1) * bq]
        q_maps = jnp.concatenate([jnp.where(m, qb, zero) for m in map_masks], axis=0)
        s = lax.dot_general(q_maps, kp_ref[...], contract_last, preferred_element_type=F32)
        s_main = s[:, :l_main]
        m = jnp.max(s_main, axis=-1, keepdims=True)
        if l_keys > l_main:
            s_tail = jnp.where(tail_valid, s[:, l_main:], NEG_BIG)
            m = jnp.maximum(m, jnp.max(s_tail, axis=-1, keepdims=True))
        acc = _dot(jnp.exp2(s_main - m).astype(BF16), vp_ref[pl.ds(0, l_main), :])
        if l_keys > l_main:
            acc = acc + _dot(jnp.exp2(s_tail - m).astype(BF16), vp_ref[pl.ds(l_main, l_keys - l_main), :])
        a0, a1 = acc[:bq], acc[bq:]
        o = a0[:, :LANES] * (1.0 / a0[:, LANES:LANES + 1]) - a1[:, :LANES] * (lam_full / a1[:, LANES:LANES + 1])
        of_ref[pl.ds(i * bq, bq), :] = _rmsnorm(o, sg_ref[...]) * (1.0 - lambda_init)
    o_ref[...] = of_ref[...].astype(o_ref.dtype)


def _attention(z3, lam, subln_g, *, n_heads, lambda_init):
    B, L, _ = z3.shape
    l_main = (L // LANES) * LANES
    l_keys = l_main if l_main == L else l_main + LANES
    bq = _largest_divisor(L, tuple(b for b in range(400, 7, -8)))
    body = functools.partial(_attn_body, L=L, l_main=l_main, l_keys=l_keys, bq=bq, lambda_init=lambda_init)
    pair = lambda off: pl.BlockSpec((None, L, PAIR), lambda b, h: (b, 0, off + h // 2))
    const = lambda shape: pl.BlockSpec(shape, lambda b, h: (0,) * len(shape))
    return pl.pallas_call(
        body,
        grid=(B, n_heads),
        in_specs=[const(lam.shape), pair(0), pair(n_heads // 2),
                  pl.BlockSpec((None, L, LANES), lambda b, h: (b, 0, 2 * n_heads + h)), const(subln_g.shape)],
        out_specs=pl.BlockSpec((None, L, LANES), lambda b, h: (b, 0, h)),
        out_shape=jax.ShapeDtypeStruct((B, L, n_heads * LANES), BF16),
        scratch_shapes=[pltpu.VMEM((l_keys, PAIR), BF16), pltpu.VMEM((l_keys, PAIR), BF16),
                        pltpu.VMEM((L, LANES), F32)],
        compiler_params=_params("parallel", "parallel"),
        name="diff_attention",
    )(lam, z3, z3, z3, subln_g)


def _pool_body(u_ref, pw_ref, ps_ref, o_ref, pad_ref, *, L, cg):
    S = L - N_META
    t = lax.broadcasted_iota(jnp.int32, (L, 1), 0)
    zeros = jnp.zeros((POOL_PAD, cg), F32)
    pad_ref[pl.ds(0, POOL_PAD), :] = zeros
    pad_ref[pl.ds(POOL_PAD + L, POOL_PAD), :] = zeros
    for g, w in enumerate(POOL_WINDOWS):
        cols = slice(g * cg, (g + 1) * cg)
        pad_ref[pl.ds(POOL_PAD, N_META), :] = u_ref[pl.ds(S, N_META), cols].astype(F32)
        pad_ref[pl.ds(POOL_PAD + N_META, S), :] = u_ref[pl.ds(0, S), cols].astype(F32)
        start = POOL_PAD - w // 2
        win = pad_ref[pl.ds(start, L), :]
        for j in range(1, w):
            win = win + pad_ref[pl.ds(start + j, L), :]
        cnt = (jnp.minimum(t + w // 2, L) - jnp.maximum(t - w // 2, 0)).astype(F32)
        m = win / cnt - pad_ref[pl.ds(POOL_PAD, L), :]
        y = (_dot(m.astype(BF16), pw_ref[g]) * ps_ref[:, cols]).astype(o_ref.dtype)
        o_ref[pl.ds(0, S), cols] = y[N_META:]
        o_ref[pl.ds(S, N_META), cols] = y[:N_META]


def _pool(z3, pool_w, pool_scale, *, col_block):
    B, L, _ = z3.shape
    G, cg, _ = pool_w.shape
    P = G * cg
    body = functools.partial(_pool_body, L=L, cg=cg)
    return pl.pallas_call(
        body,
        grid=(B,),
        in_specs=[
            pl.BlockSpec((None, L, P), lambda b: (b, 0, col_block)),
            pl.BlockSpec((G, cg, cg), lambda b: (0, 0, 0)),
            pl.BlockSpec((1, P), lambda b: (0, 0)),
        ],
        out_specs=pl.BlockSpec((None, L, P), lambda b: (b, 0, 0)),
        out_shape=jax.ShapeDtypeStruct((B, L, P), BF16),
        scratch_shapes=[pltpu.VMEM((L + 2 * POOL_PAD, cg), F32)],
        compiler_params=_params("parallel"),
        name="pool_mixer",
    )(z3, pool_w, pool_scale)


def _mix_body(a_ref, p_ref, gt_ref, b_ref, wa_ref, wp_ref, wo_ref, h_ref, o_ref, *, D):
    gates = jax.nn.sigmoid(gt_ref[...].astype(F32) + b_ref[...])
    y = gates[:, :D] * _dot(a_ref[...], wa_ref[...]) + gates[:, D:] * _dot(p_ref[...], wp_ref[...])
    o_ref[...] = h_ref[...] + _dot(y.astype(BF16), wo_ref[...])


def _mix(a, p, z, b_gate, wa, wp, wo, h, *, gate_col_block):
    T, D = h.shape
    A, P = a.shape[1], p.shape[1]
    bm = _largest_divisor(T, (256, 128, 64, 32, 16))
    body = functools.partial(_mix_body, D=D)
    const = lambda shape: pl.BlockSpec(shape, lambda i: (0, 0))
    return pl.pallas_call(
        body,
        grid=(T // bm,),
        in_specs=[
            pl.BlockSpec((bm, A), lambda i: (i, 0)),
            pl.BlockSpec((bm, P), lambda i: (i, 0)),
            pl.BlockSpec((bm, 2 * D), lambda i: (i, gate_col_block)),
            const((1, 2 * D)), const((A, D)), const((P, D)), const((D, D)),
            pl.BlockSpec((bm, D), lambda i: (i, 0)),
        ],
        out_specs=pl.BlockSpec((bm, D), lambda i: (i, 0)),
        out_shape=jax.ShapeDtypeStruct((T, D), F32),
        compiler_params=_params("parallel"),
        name="mix_out",
    )(a, p, z, b_gate, wa, wp, wo, h)


def _ffn_body(x_ref, g_ref, wg_ref, wu_ref, wd_ref, o_ref, xn_ref):
    @pl.when(pl.program_id(1) == 0)
    def _():
        x = x_ref[...]
        xn_ref[...] = _rmsnorm(x, g_ref[...]).astype(BF16)
        o_ref[...] = x

    xn = xn_ref[...]
    hid = jax.nn.silu(_dot(xn, wg_ref[...])) * _dot(xn, wu_ref[...])
    o_ref[...] += _dot(hid.astype(BF16), wd_ref[...])


def _dense_ffn(h, g, wg, wu, wd):
    T, D = h.shape
    F = wg.shape[1]
    bm = _largest_divisor(T, (768, 512, 384, 256, 128, 64, 32, 16))
    fc = _largest_divisor(F, (512, 256, 128))
    return pl.pallas_call(
        _ffn_body,
        grid=(T // bm, F // fc),
        in_specs=[
            pl.BlockSpec((bm, D), lambda i, c: (i, 0)),
            pl.BlockSpec((1, D), lambda i, c: (0, 0)),
            pl.BlockSpec((D, fc), lambda i, c: (0, c)),
            pl.BlockSpec((D, fc), lambda i, c: (0, c)),
            pl.BlockSpec((fc, D), lambda i, c: (c, 0)),
        ],
        out_specs=pl.BlockSpec((bm, D), lambda i, c: (i, 0)),
        out_shape=jax.ShapeDtypeStruct((T, D), F32),
        scratch_shapes=[pltpu.VMEM((bm, D), BF16)],
        compiler_params=_params("parallel", "arbitrary"),
        name="dense_ffn",
    )(h, g, wg, wu, wd)


ROUTE_I1, ROUTE_I2, ROUTE_W1, ROUTE_W2, ROUTE_R1, ROUTE_R2 = range(6)


def _router_body(x_ref, g_ref, r_ref, tri_ref, route_ref, cnt_ref, carry_ref, *, n_experts):
    @pl.when(pl.program_id(0) == 0)
    def _():
        carry_ref[...] = jnp.zeros_like(carry_ref)

    xn = _rmsnorm(x_ref[...], g_ref[...])
    x_hi = xn.astype(BF16)
    x_lo = (xn - x_hi.astype(F32)).astype(BF16)
    hi_both = _dot(x_hi, r_ref[...])
    logits = hi_both[:, :LANES] + (hi_both[:, LANES:] + _dot(x_lo, r_ref[:, pl.ds(0, LANES)]))

    lane = lax.broadcasted_iota(jnp.int32, (1, LANES), 1)
    lane_f = lane.astype(F32)
    lg = jnp.where(lane < n_experts, logits, NEG_BIG)
    v1 = jnp.max(lg, axis=-1, keepdims=True)
    i1 = jnp.min(jnp.where(lg == v1, lane_f, float(LANES)), axis=-1, keepdims=True)
    lg2 = jnp.where(lane_f == i1, NEG_BIG, lg)
    v2 = jnp.max(lg2, axis=-1, keepdims=True)
    i2 = jnp.min(jnp.where(lg2 == v2, lane_f, float(LANES)), axis=-1, keepdims=True)
    e2 = jnp.exp(v2 - v1)
    w1 = 1.0 / (1.0 + e2)
    w2 = e2 / (1.0 + e2)

    sel1 = lane_f == i1
    sel2 = lane_f == i2
    onehot = jnp.where(sel1 | sel2, 1.0, 0.0)
    rank = _dot(tri_ref[...], onehot.astype(BF16)) + carry_ref[...]
    r1 = jnp.sum(jnp.where(sel1, rank, 0.0), axis=-1, keepdims=True)
    r2 = jnp.sum(jnp.where(sel2, rank, 0.0), axis=-1, keepdims=True)
    carry_ref[...] += jnp.sum(onehot, axis=0, keepdims=True)

    route = jnp.zeros(route_ref.shape, F32)
    for col, val in ((ROUTE_I1, i1), (ROUTE_I2, i2), (ROUTE_W1, w1), (ROUTE_W2, w2), (ROUTE_R1, r1), (ROUTE_R2, r2)):
        route = jnp.where(lane == col, val, route)
    route_ref[...] = route
    cnt_ref[...] = jnp.broadcast_to(carry_ref[...], cnt_ref.shape)


def _router(h, g, r_hi_lo, *, n_experts):
    T, D = h.shape
    bm = _largest_divisor(T, (256, 128, 64, 32, 16))
    tri = (lax.broadcasted_iota(jnp.int32, (bm, bm), 0) > lax.broadcasted_iota(jnp.int32, (bm, bm), 1)).astype(BF16)
    body = functools.partial(_router_body, n_experts=n_experts)
    const = lambda shape: pl.BlockSpec(shape, lambda i: (0, 0))
    return pl.pallas_call(
        body,
        grid=(T // bm,),
        in_specs=[pl.BlockSpec((bm, D), lambda i: (i, 0)), const((1, D)), const((D, 2 * LANES)), const((bm, bm))],
        out_specs=[pl.BlockSpec((bm, LANES), lambda i: (i, 0)), const((8, LANES))],
        out_shape=[jax.ShapeDtypeStruct((T, LANES), F32), jax.ShapeDtypeStruct((8, LANES), F32)],
        scratch_shapes=[pltpu.VMEM((1, LANES), F32)],
        compiler_params=_params("arbitrary"),
        name="moe_router",
    )(h, g, r_hi_lo, tri)


ZERO_ROWS = 64
ROW_DMA_UNROLL = 8


def _dispatch_body(pos_ref, seg_ref, x_ref, g_ref, xs_hbm, xn_ref, zero_ref, sems, *, bm, T, n_experts, be, n_rows):
    i = pl.program_id(0)
    n_steps = pl.num_programs(0)
    slot = i % 2

    def zero_copy(first_row, c):
        row = pl.multiple_of(first_row + c * ZERO_ROWS, ZERO_ROWS)
        return pltpu.make_async_copy(zero_ref, xs_hbm.at[pl.ds(row, ZERO_ROWS), :], sems.at[2])

    def zero_block(first_row):
        for c in range(be // ZERO_ROWS):
            zero_copy(first_row, c).start()
        for c in range(be // ZERO_ROWS):
            zero_copy(first_row, c).wait()

    @pl.when(i == 0)
    def _():
        zero_ref[...] = jnp.zeros_like(zero_ref)
        for e in range(n_experts):
            @pl.when(seg_ref[n_experts + e] > 0)
            def _():
                zero_block(seg_ref[e] - be)

        def clear_unused(j, c):
            zero_block(j * be)
            return c

        lax.fori_loop(seg_ref[n_experts - 1] // be, n_rows // be, clear_unused, 0)

    def drain(s):
        for _ in range(TOP_K):
            pltpu.make_async_copy(xn_ref.at[s], xs_hbm.at[pl.ds(0, bm), :], sems.at[s]).wait()

    @pl.when(i >= 2)
    def _():
        drain(slot)

    xn_ref[slot] = _rmsnorm(x_ref[...], g_ref[...])

    def row_copy(r, k):
        dst = pos_ref[k * T + i * bm + r]
        return pltpu.make_async_copy(xn_ref.at[slot, pl.ds(r, 1), :], xs_hbm.at[pl.ds(dst, 1), :], sems.at[slot])

    def issue(grp, c):
        for u in range(ROW_DMA_UNROLL):
            row_copy(grp * ROW_DMA_UNROLL + u, 0).start(priority=u % 2)
            row_copy(grp * ROW_DMA_UNROLL + u, 1).start(priority=(u + 1) % 2)
        return c

    lax.fori_loop(0, bm // ROW_DMA_UNROLL, issue, 0)

    @pl.when(i == n_steps - 1)
    def _():
        drain(slot)

        @pl.when(i >= 1)
        def _():
            drain(1 - slot)


def _dispatch(pos, seg, h, g, *, n_rows, n_experts, be):
    T, D = h.shape
    bm = _largest_divisor(T, (256, 128, 64, 32, 16))
    body = functools.partial(_dispatch_body, bm=bm, T=T, n_experts=n_experts, be=be, n_rows=n_rows)
    return pl.pallas_call(
        body,
        grid_spec=pltpu.PrefetchScalarGridSpec(
            num_scalar_prefetch=2,
            grid=(T // bm,),
            in_specs=[pl.BlockSpec((bm, D), lambda i, pos, seg: (i, 0)),
                      pl.BlockSpec((1, D), lambda i, pos, seg: (0, 0))],
            out_specs=pl.BlockSpec(memory_space=pl.ANY),
            scratch_shapes=[pltpu.VMEM((2, bm, D), F32), pltpu.VMEM((ZERO_ROWS, D), F32),
                            pltpu.SemaphoreType.DMA((3,))],
        ),
        out_shape=jax.ShapeDtypeStruct((n_rows, D), F32),
        compiler_params=_params("arbitrary"),
        name="moe_dispatch",
    )(pos, seg, h, g)


def _expert_body(blk_ref, x_ref, wg_ref, wu_ref, wd_ref, o_ref, xb_ref, *, n_blocks):
    j = pl.program_id(0)

    @pl.when(pl.program_id(1) == 0)
    def _():
        o_ref[...] = jnp.zeros_like(o_ref)

    @pl.when(j < blk_ref[n_blocks])
    def _():
        @pl.when(pl.program_id(1) == 0)
        def _():
            xb_ref[...] = x_ref[...].astype(BF16)

        xb = xb_ref[...]
        hid = jax.nn.silu(_dot(xb, wg_ref[...])) * _dot(xb, wu_ref[...])
        o_ref[...] += _dot(hid.astype(BF16), wd_ref[...])


def _expert_ffn(blk, xs, wg, wu, wd, *, be):
    n_rows, D = xs.shape
    F = wg.shape[2]
    n_blocks = n_rows // be
    fc = _largest_divisor(F, (512, 256, 128))
    nc = F // fc
    body = functools.partial(_expert_body, n_blocks=n_blocks)

    def row_blk(j, blk):
        return jnp.minimum(j, blk[n_blocks] - 1)

    def chunk(j, c, blk):
        return jnp.where(j < blk[n_blocks], c, nc - 1)

    return pl.pallas_call(
        body,
        grid_spec=pltpu.PrefetchScalarGridSpec(
            num_scalar_prefetch=1,
            grid=(n_blocks, nc),
            in_specs=[
                pl.BlockSpec((be, D), lambda j, c, blk: (row_blk(j, blk), 0)),
                pl.BlockSpec((None, D, fc), lambda j, c, blk: (blk[row_blk(j, blk)], 0, chunk(j, c, blk))),
                pl.BlockSpec((None, D, fc), lambda j, c, blk: (blk[row_blk(j, blk)], 0, chunk(j, c, blk))),
                pl.BlockSpec((None, fc, D), lambda j, c, blk: (blk[row_blk(j, blk)], chunk(j, c, blk), 0)),
            ],
            out_specs=pl.BlockSpec((be, D), lambda j, c, blk: (j, 0)),
            scratch_shapes=[pltpu.VMEM((be, D), BF16)],
        ),
        out_shape=jax.ShapeDtypeStruct((n_rows, D), F32),
        compiler_params=_params("arbitrary", "arbitrary"),
        name="moe_experts",
    )(blk, xs, wg, wu, wd)


def _combine_body(pos_ref, h_ref, route_ref, g_ref, y_hbm, o_ref, y_ref, sems, *, bm, T):
    i = pl.program_id(0)
    n_steps = pl.num_programs(0)
    slot = i % 2

    def start_gathers(step, s):
        def row_copy(r, k):
            src = pos_ref[k * T + step * bm + r]
            return pltpu.make_async_copy(y_hbm.at[pl.ds(src, 1), :], y_ref.at[s, k, pl.ds(r, 1), :], sems.at[s])

        def issue(grp, c):
            for u in range(ROW_DMA_UNROLL):
                row_copy(grp * ROW_DMA_UNROLL + u, 0).start(priority=u % 2)
                row_copy(grp * ROW_DMA_UNROLL + u, 1).start(priority=(u + 1) % 2)
            return c

        lax.fori_loop(0, bm // ROW_DMA_UNROLL, issue, 0)

    @pl.when(i == 0)
    def _():
        start_gathers(i, slot)

    @pl.when(i + 1 < n_steps)
    def _():
        start_gathers(i + 1, 1 - slot)

    for k in range(TOP_K):
        pltpu.make_async_copy(y_hbm.at[pl.ds(0, bm), :], y_ref.at[slot, k], sems.at[slot]).wait()

    route = route_ref[...]
    w1 = route[:, ROUTE_W1:ROUTE_W1 + 1]
    w2 = route[:, ROUTE_W2:ROUTE_W2 + 1]
    h = h_ref[...] + (w1 * y_ref[slot, 0] + w2 * y_ref[slot, 1])
    o_ref[...] = _rmsnorm(h, g_ref[...])


def _combine(pos, h, route, g, y, *, B, L):
    T, D = h.shape
    S = L - N_META
    bm = _largest_divisor(L, tuple(b for b in range(512, 7, -8)))
    blocks_per_seq = L // bm
    assert pl.cdiv(S, bm) == blocks_per_seq
    body = functools.partial(_combine_body, bm=bm, T=T)
    return pl.pallas_call(
        body,
        grid_spec=pltpu.PrefetchScalarGridSpec(
            num_scalar_prefetch=1,
            grid=(T // bm,),
            in_specs=[pl.BlockSpec((bm, D), lambda i, pos: (i, 0)),
                      pl.BlockSpec((bm, LANES), lambda i, pos: (i, 0)),
                      pl.BlockSpec((1, D), lambda i, pos: (0, 0)),
                      pl.BlockSpec(memory_space=pl.ANY)],
            out_specs=pl.BlockSpec((None, bm, D), lambda i, pos: (i // blocks_per_seq, i % blocks_per_seq, 0)),
            scratch_shapes=[pltpu.VMEM((2, TOP_K, bm, D), F32), pltpu.SemaphoreType.DMA((2,))],
        ),
        out_shape=jax.ShapeDtypeStruct((B, S, D), F32),
        compiler_params=_params("arbitrary"),
        name="moe_combine",
    )(pos, h, route, g, y)


def _rope_tables(L):
    half = HEAD_DIM // 2
    inv = ROPE_THETA ** (-2.0 * jnp.arange(half, dtype=F32) / HEAD_DIM)
    pos = jnp.concatenate([jnp.arange(N_META, L), jnp.arange(N_META)]).astype(F32)
    ang = pos[:, None] * inv[None, :]
    cs = jnp.tile(jnp.stack([jnp.cos(ang), jnp.sin(ang)]), (1, 1, LANES // half))
    return jnp.stack([cs * (HEAD_DIM ** -0.5 * LOG2_E), cs])


def _pair_rotary_halves(w_qk, n_heads):
    D = w_qk.shape[0]
    w = w_qk.reshape(D, 2, n_heads // 2, 2, 2, 2, HEAD_DIM // 2)
    return w.transpose(0, 1, 2, 5, 3, 4, 6).reshape(D, -1)


def _token_mixer(h, layer, B, L, w_in, b_gate, lam, subln_g, pool_w, pool_scale, w_ba, w_bp, w_out, norm_g, rope_cs):
    T, D = h.shape
    A, P = w_ba.shape[0], w_bp.shape[0]
    n_heads = A // (2 * HEAD_DIM)
    lambda_init = 0.8 - 0.6 * math.exp(-0.3 * layer)
    w_qk = _pair_rotary_halves(w_in[:, :2 * A], n_heads).astype(BF16)
    z = _inproj(h, norm_g, w_qk, w_in[:, 2 * A:].astype(BF16), rope_cs, L=L)
    z3 = z.reshape(B, L, z.shape[1])
    a = _attention(z3, lam, subln_g, n_heads=n_heads, lambda_init=lambda_init)
    p = _pool(z3, pool_w, pool_scale, col_block=(3 * A) // P)
    return _mix(a.reshape(T, A), p.reshape(T, P), z, b_gate, w_ba, w_bp, w_out, h,
                gate_col_block=(3 * A + P) // (2 * D))


def _moe_and_final_norm(h, norm_g, router, wg, wu, wd, final_g, *, B, L):
    T, D = h.shape
    E = router.shape[1]
    be = 512 if T >= 4096 else 64
    r_pad = jnp.zeros((D, LANES), F32).at[:, :E].set(router)
    r_hi = r_pad.astype(BF16)
    r_lo = (r_pad - r_hi.astype(F32)).astype(BF16)
    route, cnt = _router(h, norm_g, jnp.concatenate([r_hi, r_lo], axis=1), n_experts=E)

    counts = cnt[0, :E].astype(jnp.int32)
    n_blk = (counts + be - 1) // be
    seg_end = jnp.cumsum(n_blk * be)
    seg_start = seg_end - n_blk * be
    n_blocks = (TOP_K * T + E * (be - 1)) // be
    blk_end = jnp.cumsum(n_blk)
    blk_expert = jnp.minimum(jnp.sum(jnp.arange(n_blocks)[:, None] >= blk_end[None, :], axis=1), E - 1)
    blk = jnp.concatenate([blk_expert.astype(jnp.int32), blk_end[-1:].astype(jnp.int32)])
    seg = jnp.concatenate([seg_end, n_blk]).astype(jnp.int32)

    i1 = route[:, ROUTE_I1].astype(jnp.int32)
    i2 = route[:, ROUTE_I2].astype(jnp.int32)
    pos = jnp.concatenate([seg_start[i1] + route[:, ROUTE_R1].astype(jnp.int32),
                           seg_start[i2] + route[:, ROUTE_R2].astype(jnp.int32)])

    xs = _dispatch(pos, seg, h, norm_g, n_rows=n_blocks * be, n_experts=E, be=be)
    y = _expert_ffn(blk, xs, wg, wu, wd, be=be)
    return _combine(pos, h, route, final_g, y, B=B, L=L)


def kernel(x, meta_tokens, norm_mix, w_in, b_gate, lambda_q1, lambda_k1, lambda_q2, lambda_k2, subln, pool_w, pool_scale, w_branch_attn, w_branch_pool, w_out, norm_ffn, dense_w_gate, dense_w_up, dense_w_down, router, moe_w_gate, moe_w_up, moe_w_down, norm_final):
    B, S, D = x.shape
    depth = w_in.shape[0]
    assert depth == 2, "the final RMSNorm is fused into the routed layer, which must be the last one"
    L = N_META + S
    T = B * L
    meta = jnp.broadcast_to(meta_tokens[None].astype(x.dtype), (B, N_META, D))
    h = jnp.concatenate([x, meta], axis=1).reshape(T, D)
    rope = _rope_tables(L)
    row = lambda v: v.reshape(1, -1)

    for i in range(depth):
        lam = jnp.stack([lambda_q1[i], lambda_k1[i], lambda_q2[i], lambda_k2[i]])
        h = _token_mixer(h, i, B, L, w_in[i], row(b_gate[i]), lam, row(subln[i]),
                         pool_w[i].astype(BF16), row(pool_scale[i]), w_branch_attn[i].astype(BF16),
                         w_branch_pool[i].astype(BF16), w_out[i].astype(BF16), row(norm_mix[i]), rope)
        j = i // 2
        if i % 2 == 0:
            h = _dense_ffn(h, row(norm_ffn[i]), dense_w_gate[j].astype(BF16), dense_w_up[j].astype(BF16),
                           dense_w_down[j].astype(BF16))
        else:
            h = _moe_and_final_norm(h, row(norm_ffn[i]), router[j], moe_w_gate[j].astype(BF16),
                                    moe_w_up[j].astype(BF16), moe_w_down[j].astype(BF16), row(norm_final),
                                    B=B, L=L)
    return h
```

```python
import functools
import math

import jax
import jax.numpy as jnp
from jax import lax
from jax.experimental import pallas as pl
from jax.experimental.pallas import tpu as pltpu

N_META = 16
HEAD_DIM = 64
POOL_WINDOWS = (2, 4, 8, 16)
TOP_K = 2
ROPE_THETA = 10000.0
EPS = 1e-5

LANES = 128
SUBLANES_BF16 = 16
VMEM_CAP_BYTES = 64 * 1024 * 1024
VMEM_LIMIT_BYTES = VMEM_CAP_BYTES - 6 * 1024 * 1024

NEG_BIG = -1e30
LOG2_E = 1.4426950408889634
POOL_PAD = max(POOL_WINDOWS)
MXU_DEPTH = 256
POOL_BLOCK_ROWS = MXU_DEPTH - 2 * POOL_PAD

F32 = jnp.float32
BF16 = jnp.bfloat16


def _largest_divisor(n, candidates):
    for c in candidates:
        if c <= n and n % c == 0:
            return c
    raise ValueError(f"no block size in {candidates} divides {n}")


def _params(*semantics):
    return pltpu.CompilerParams(dimension_semantics=semantics, vmem_limit_bytes=VMEM_LIMIT_BYTES)


def _rmsnorm(x, g):
    ms = jnp.mean(x * x, axis=-1, keepdims=True)
    return x * lax.rsqrt(ms + EPS) * g


def _dot(a, b):
    return jnp.dot(a, b, preferred_element_type=F32)


def _inproj_body(x_ref, g_ref, wqk_ref, wr_ref, cs_ref, o_ref, xn_ref, *, bn, n_qk_blocks):
    j = pl.program_id(1)

    @pl.when(j == 0)
    def _():
        xn_ref[...] = _rmsnorm(x_ref[...], g_ref[...]).astype(BF16)

    @pl.when(j < n_qk_blocks)
    def _():
        acc = _dot(xn_ref[...], wqk_ref[...])
        cos, sin = cs_ref[0], cs_ref[1]
        for c in range(0, bn, 2 * LANES):
            t1 = acc[:, c:c + LANES]
            t2 = acc[:, c + LANES:c + 2 * LANES]
            o_ref[:, c:c + LANES] = (t1 * cos - t2 * sin).astype(o_ref.dtype)
            o_ref[:, c + LANES:c + 2 * LANES] = (t2 * cos + t1 * sin).astype(o_ref.dtype)

    @pl.when(j >= n_qk_blocks)
    def _():
        o_ref[...] = _dot(xn_ref[...], wr_ref[...]).astype(o_ref.dtype)


def _inproj(h, g, w_qk, w_rest, rope_cs):
    T, D = h.shape
    n_qk, n_rest = w_qk.shape[1], w_rest.shape[1]
    bm = _largest_divisor(T, (768, 512, 384, 256, 128, 64, 32, 16))
    bn = _largest_divisor(math.gcd(n_qk // 2, n_rest), (1024, 512, 256))
    n_qk_blocks = n_qk // bn
    q_blocks = n_qk_blocks // 2
    body = functools.partial(_inproj_body, bn=bn, n_qk_blocks=n_qk_blocks)
    return pl.pallas_call(
        body,
        grid=(T // bm, (n_qk + n_rest) // bn),
        in_specs=[
            pl.BlockSpec((bm, D), lambda i, j: (i, 0)),
            pl.BlockSpec((1, D), lambda i, j: (0, 0)),
            pl.BlockSpec((D, bn), lambda i, j: (0, jnp.minimum(j, n_qk_blocks - 1))),
            pl.BlockSpec((D, bn), lambda i, j: (0, jnp.maximum(j - n_qk_blocks, 0))),
            pl.BlockSpec((None, 2, bm, LANES), lambda i, j: (jnp.minimum(j // q_blocks, 1), 0, i, 0)),
        ],
        out_specs=pl.BlockSpec((bm, bn), lambda i, j: (i, j)),
        out_shape=jax.ShapeDtypeStruct((T, n_qk + n_rest), BF16),
        scratch_shapes=[pltpu.VMEM((bm, D), BF16)],
        compiler_params=_params("parallel", "arbitrary"),
        name="inproj",
    )(h, g, w_qk, w_rest, rope_cs)


PAIR = 2 * LANES


def _attn_body(lam_ref, q_ref, k_ref, v_ref, sg_ref, o_ref, kp_ref, vp_ref, of_ref,
               *, L, l_main, l_keys, bq, lambda_init):
    head_in_pair = pl.program_id(1) % 2
    lane = lax.broadcasted_iota(jnp.int32, (1, PAIR), 1)
    in_head = ((lane % LANES) // HEAD_DIM) == head_in_pair
    map_of_lane = (lane % HEAD_DIM) // (HEAD_DIM // 2)
    map_masks = (in_head & (map_of_lane == 0), in_head & (map_of_lane == 1))

    kp_ref[pl.ds(0, L), :] = k_ref[...]
    vp_ref[pl.ds(0, L), pl.ds(0, LANES)] = v_ref[...]
    vp_ref[:, pl.ds(LANES, LANES)] = jnp.ones((l_keys, LANES), BF16)
    if l_keys > L:
        kp_ref[pl.ds(L, l_keys - L), :] = jnp.zeros((l_keys - L, PAIR), BF16)
        vp_ref[pl.ds(L, l_keys - L), pl.ds(0, LANES)] = jnp.zeros((l_keys - L, LANES), BF16)

    lam = lam_ref[...]
    lam_full = (jnp.exp(jnp.sum(lam[0:1] * lam[1:2], axis=-1, keepdims=True))
                - jnp.exp(jnp.sum(lam[2:3] * lam[3:4], axis=-1, keepdims=True)) + lambda_init)

    tail_valid = (lax.broadcasted_iota(jnp.int32, (1, LANES), 1) + l_main) < L
    contract_last = (((1,), (1,)), ((), ()))
    zero = jnp.zeros((), BF16)
    q_all = q_ref[...]

    for i in range(L // bq):
        qb = q_all[i * bq:(i + 1) * bq]
        q_maps = jnp.concatenate([jnp.where(m, qb, zero) for m in map_masks], axis=0)
        s = lax.dot_general(q_maps, kp_ref[...], contract_last, preferred_element_type=F32)
        s_main = s[:, :l_main]
        m = jnp.max(s_main, axis=-1, keepdims=True)
        if l_keys > l_main:
            s_tail = jnp.where(tail_valid, s[:, l_main:], NEG_BIG)
            m = jnp.maximum(m, jnp.max(s_tail, axis=-1, keepdims=True))
        acc = _dot(jnp.exp2(s_main - m).astype(BF16), vp_ref[pl.ds(0, l_main), :])
        if l_keys > l_main:
            acc = acc + _dot(jnp.exp2(s_tail - m).astype(BF16), vp_ref[pl.ds(l_main, l_keys - l_main), :])
        a0, a1 = acc[:bq], acc[bq:]
        o = a0[:, :LANES] * (1.0 / a0[:, LANES:LANES + 1]) - a1[:, :LANES] * (lam_full / a1[:, LANES:LANES + 1])
        of_ref[pl.ds(i * bq, bq), :] = _rmsnorm(o, sg_ref[...]) * (1.0 - lambda_init)
    o_ref[...] = of_ref[...].astype(o_ref.dtype)


def _attention(z3, lam, subln_g, *, n_heads, lambda_init):
    B, L, _ = z3.shape
    l_main = (L // LANES) * LANES
    l_keys = l_main if l_main == L else l_main + LANES
    bq = _largest_divisor(L, tuple(b for b in range(400, 7, -8)))
    body = functools.partial(_attn_body, L=L, l_main=l_main, l_keys=l_keys, bq=bq, lambda_init=lambda_init)
    pair = lambda off: pl.BlockSpec((None, L, PAIR), lambda b, h: (b, 0, off + h // 2))
    const = lambda shape: pl.BlockSpec(shape, lambda b, h: (0,) * len(shape))
    return pl.pallas_call(
        body,
        grid=(B, n_heads),
        in_specs=[const(lam.shape), pair(0), pair(n_heads // 2),
                  pl.BlockSpec((None, L, LANES), lambda b, h: (b, 0, 2 * n_heads + h)), const(subln_g.shape)],
        out_specs=pl.BlockSpec((None, L, LANES), lambda b, h: (b, 0, h)),
        out_shape=jax.ShapeDtypeStruct((B, L, n_heads * LANES), BF16),
        scratch_shapes=[pltpu.VMEM((l_keys, PAIR), BF16), pltpu.VMEM((l_keys, PAIR), BF16),
                        pltpu.VMEM((L, LANES), F32)],
        compiler_params=_params("parallel", "parallel"),
        name="diff_attention",
    )(lam, z3, z3, z3, subln_g)


def _pool_body(u_ref, band_ref, pw_ref, ps_ref, o_ref, pad_ref, *, L, cg, br):
    S = L - N_META
    P = pad_ref.shape[1]
    pad_ref[pl.ds(0, POOL_PAD), :] = jnp.zeros((POOL_PAD, P), pad_ref.dtype)
    pad_ref[pl.ds(POOL_PAD + L, POOL_PAD), :] = jnp.zeros((POOL_PAD, P), pad_ref.dtype)
    pad_ref[pl.ds(POOL_PAD, N_META), :] = u_ref[pl.ds(S, N_META), :]
    pad_ref[pl.ds(POOL_PAD + N_META, S), :] = u_ref[pl.ds(0, S), :]
    for start in range(0, L, br):
        rows = min(br, L - start)
        t = lax.broadcasted_iota(jnp.int32, (rows, 1), 0) + start
        for g, w in enumerate(POOL_WINDOWS):
            cols = slice(g * cg, (g + 1) * cg)
            win = _dot(band_ref[g, pl.ds(0, rows), pl.ds(0, rows + 2 * POOL_PAD)],
                       pad_ref[pl.ds(start, rows + 2 * POOL_PAD), cols])
            cnt = (jnp.minimum(t + w // 2, L) - jnp.maximum(t - w // 2, 0)).astype(F32)
            m = win * (1.0 / cnt) - pad_ref[pl.ds(start + POOL_PAD, rows), cols].astype(F32)
            y = (_dot(m.astype(BF16), pw_ref[g]) * ps_ref[:, cols]).astype(o_ref.dtype)
            if start == 0:
                o_ref[pl.ds(S, N_META), cols] = y[:N_META]
                o_ref[pl.ds(0, rows - N_META), cols] = y[N_META:]
            else:
                o_ref[pl.ds(start - N_META, rows), cols] = y


def _pool(z3, pool_w, pool_scale, *, col_block):
    B, L, _ = z3.shape
    G, cg, _ = pool_w.shape
    P = G * cg
    br = min(L, POOL_BLOCK_ROWS)
    assert L % SUBLANES_BF16 == 0 and N_META % SUBLANES_BF16 == 0
    r = lax.broadcasted_iota(jnp.int32, (br, br + 2 * POOL_PAD), 0)
    c = lax.broadcasted_iota(jnp.int32, (br, br + 2 * POOL_PAD), 1) - POOL_PAD
    band = jnp.stack([((c >= r - w // 2) & (c < r + w // 2)).astype(BF16) for w in POOL_WINDOWS])
    body = functools.partial(_pool_body, L=L, cg=cg, br=br)
    return pl.pallas_call(
        body,
        grid=(B,),
        in_specs=[
            pl.BlockSpec((None, L, P), lambda b: (b, 0, col_block)),
            pl.BlockSpec(band.shape, lambda b: (0, 0, 0)),
            pl.BlockSpec((G, cg, cg), lambda b: (0, 0, 0)),
            pl.BlockSpec((1, P), lambda b: (0, 0)),
        ],
        out_specs=pl.BlockSpec((None, L, P), lambda b: (b, 0, 0)),
        out_shape=jax.ShapeDtypeStruct((B, L, P), BF16),
        scratch_shapes=[pltpu.VMEM((L + 2 * POOL_PAD, P), BF16)],
        compiler_params=_params("parallel"),
        name="pool_mixer",
    )(z3, band, pool_w, pool_scale)


def _mix_body(a_ref, p_ref, gt_ref, b_ref, wa_ref, wp_ref, wo_ref, h_ref, o_ref, *, D):
    gates = jax.nn.sigmoid(gt_ref[...].astype(F32) + b_ref[...])
    y = gates[:, :D] * _dot(a_ref[...], wa_ref[...]) + gates[:, D:] * _dot(p_ref[...], wp_ref[...])
    o_ref[...] = h_ref[...] + _dot(y.astype(BF16), wo_ref[...])


def _mix(a, p, z, b_gate, wa, wp, wo, h, *, gate_col_block):
    T, D = h.shape
    A, P = a.shape[1], p.shape[1]
    bm = _largest_divisor(T, (256, 128, 64, 32, 16))
    body = functools.partial(_mix_body, D=D)
    const = lambda shape: pl.BlockSpec(shape, lambda i: (0, 0))
    return pl.pallas_call(
        body,
        grid=(T // bm,),
        in_specs=[
            pl.BlockSpec((bm, A), lambda i: (i, 0)),
            pl.BlockSpec((bm, P), lambda i: (i, 0)),
            pl.BlockSpec((bm, 2 * D), lambda i: (i, gate_col_block)),
            const((1, 2 * D)), const((A, D)), const((P, D)), const((D, D)),
            pl.BlockSpec((bm, D), lambda i: (i, 0)),
        ],
        out_specs=pl.BlockSpec((bm, D), lambda i: (i, 0)),
        out_shape=jax.ShapeDtypeStruct((T, D), F32),
        compiler_params=_params("parallel"),
        name="mix_out",
    )(a, p, z, b_gate, wa, wp, wo, h)


def _ffn_body(x_ref, g_ref, wg_ref, wu_ref, wd_ref, o_ref, xn_ref):
    @pl.when(pl.program_id(1) == 0)
    def _():
        x = x_ref[...]
        xn_ref[...] = _rmsnorm(x, g_ref[...]).astype(BF16)
        o_ref[...] = x

    xn = xn_ref[...]
    hid = jax.nn.silu(_dot(xn, wg_ref[...])) * _dot(xn, wu_ref[...])
    o_ref[...] += _dot(hid.astype(BF16), wd_ref[...])


def _dense_ffn(h, g, wg, wu, wd):
    T, D = h.shape
    F = wg.shape[1]
    bm = _largest_divisor(T, (768, 512, 384, 256, 128, 64, 32, 16))
    fc = _largest_divisor(F, (512, 256, 128))
    return pl.pallas_call(
        _ffn_body,
        grid=(T // bm, F // fc),
        in_specs=[
            pl.BlockSpec((bm, D), lambda i, c: (i, 0)),
            pl.BlockSpec((1, D), lambda i, c: (0, 0)),
            pl.BlockSpec((D, fc), lambda i, c: (0, c)),
            pl.BlockSpec((D, fc), lambda i, c: (0, c)),
            pl.BlockSpec((fc, D), lambda i, c: (c, 0)),
        ],
        out_specs=pl.BlockSpec((bm, D), lambda i, c: (i, 0)),
        out_shape=jax.ShapeDtypeStruct((T, D), F32),
        scratch_shapes=[pltpu.VMEM((bm, D), BF16)],
        compiler_params=_params("parallel", "arbitrary"),
        name="dense_ffn",
    )(h, g, wg, wu, wd)


ROUTE_I1, ROUTE_I2, ROUTE_W1, ROUTE_W2, ROUTE_R1, ROUTE_R2 = range(6)


def _router_body(x_ref, g_ref, r_ref, tri_ref, route_ref, cnt_ref, carry_ref, *, n_experts):
    @pl.when(pl.program_id(0) == 0)
    def _():
        carry_ref[...] = jnp.zeros_like(carry_ref)

    xn = _rmsnorm(x_ref[...], g_ref[...])
    x_hi = xn.astype(BF16)
    x_lo = (xn - x_hi.astype(F32)).astype(BF16)
    hi_both = _dot(x_hi, r_ref[...])
    logits = hi_both[:, :LANES] + (hi_both[:, LANES:] + _dot(x_lo, r_ref[:, pl.ds(0, LANES)]))

    lane = lax.broadcasted_iota(jnp.int32, (1, LANES), 1)
    lane_f = lane.astype(F32)
    lg = jnp.where(lane < n_experts, logits, NEG_BIG)
    v1 = jnp.max(lg, axis=-1, keepdims=True)
    i1 = jnp.min(jnp.where(lg == v1, lane_f, float(LANES)), axis=-1, keepdims=True)
    lg2 = jnp.where(lane_f == i1, NEG_BIG, lg)
    v2 = jnp.max(lg2, axis=-1, keepdims=True)
    i2 = jnp.min(jnp.where(lg2 == v2, lane_f, float(LANES)), axis=-1, keepdims=True)
    e2 = jnp.exp(v2 - v1)
    w1 = 1.0 / (1.0 + e2)
    w2 = e2 / (1.0 + e2)

    sel1 = lane_f == i1
    sel2 = lane_f == i2
    onehot = jnp.where(sel1 | sel2, 1.0, 0.0)
    rank = _dot(tri_ref[...], onehot.astype(BF16)) + carry_ref[...]
    r1 = jnp.sum(jnp.where(sel1, rank, 0.0), axis=-1, keepdims=True)
    r2 = jnp.sum(jnp.where(sel2, rank, 0.0), axis=-1, keepdims=True)
    carry_ref[...] += jnp.sum(onehot, axis=0, keepdims=True)

    route = jnp.zeros(route_ref.shape, F32)
    for col, val in ((ROUTE_I1, i1), (ROUTE_I2, i2), (ROUTE_W1, w1), (ROUTE_W2, w2), (ROUTE_R1, r1), (ROUTE_R2, r2)):
        route = jnp.where(lane == col, val, route)
    route_ref[...] = route
    cnt_ref[...] = jnp.broadcast_to(carry_ref[...], cnt_ref.shape)


def _router(h, g, r_hi_lo, *, n_experts):
    T, D = h.shape
    bm = _largest_divisor(T, (256, 128, 64, 32, 16))
    tri = (lax.broadcasted_iota(jnp.int32, (bm, bm), 0) > lax.broadcasted_iota(jnp.int32, (bm, bm), 1)).astype(BF16)
    body = functools.partial(_router_body, n_experts=n_experts)
    const = lambda shape: pl.BlockSpec(shape, lambda i: (0, 0))
    return pl.pallas_call(
        body,
        grid=(T // bm,),
        in_specs=[pl.BlockSpec((bm, D), lambda i: (i, 0)), const((1, D)), const((D, 2 * LANES)), const((bm, bm))],
        out_specs=[pl.BlockSpec((bm, LANES), lambda i: (i, 0)), const((8, LANES))],
        out_shape=[jax.ShapeDtypeStruct((T, LANES), F32), jax.ShapeDtypeStruct((8, LANES), F32)],
        scratch_shapes=[pltpu.VMEM((1, LANES), F32)],
        compiler_params=_params("arbitrary"),
        name="moe_router",
    )(h, g, r_hi_lo, tri)


ZERO_ROWS = 64
ROW_DMA_UNROLL = 8


def _dispatch_body(pos_ref, seg_ref, x_ref, g_ref, xs_hbm, xn_ref, zero_ref, sems, *, bm, T, n_experts, be, n_rows):
    i = pl.program_id(0)
    n_steps = pl.num_programs(0)
    slot = i % 2

    def zero_copy(first_row, c):
        row = pl.multiple_of(first_row + c * ZERO_ROWS, ZERO_ROWS)
        return pltpu.make_async_copy(zero_ref, xs_hbm.at[pl.ds(row, ZERO_ROWS), :], sems.at[2])

    def zero_block(first_row):
        for c in range(be // ZERO_ROWS):
            zero_copy(first_row, c).start()
        for c in range(be // ZERO_ROWS):
            zero_copy(first_row, c).wait()

    @pl.when(i == 0)
    def _():
        zero_ref[...] = jnp.zeros_like(zero_ref)
        for e in range(n_experts):
            @pl.when(seg_ref[n_experts + e] > 0)
            def _():
                zero_block(seg_ref[e] - be)

        def clear_unused(j, c):
            zero_block(j * be)
            return c

        lax.fori_loop(seg_ref[n_experts - 1] // be, n_rows // be, clear_unused, 0)

    def drain(s):
        for _ in range(TOP_K):
            pltpu.make_async_copy(xn_ref.at[s], xs_hbm.at[pl.ds(0, bm), :], sems.at[s]).wait()

    @pl.when(i >= 2)
    def _():
        drain(slot)

    xn_ref[slot] = _rmsnorm(x_ref[...], g_ref[...])

    def row_copy(r, k):
        dst = pos_ref[k * T + i * bm + r]
        return pltpu.make_async_copy(xn_ref.at[slot, pl.ds(r, 1), :], xs_hbm.at[pl.ds(dst, 1), :], sems.at[slot])

    def issue(grp, c):
        for u in range(ROW_DMA_UNROLL):
            row_copy(grp * ROW_DMA_UNROLL + u, 0).start(priority=u % 2)
            row_copy(grp * ROW_DMA_UNROLL + u, 1).start(priority=(u + 1) % 2)
        return c

    lax.fori_loop(0, bm // ROW_DMA_UNROLL, issue, 0)

    @pl.when(i == n_steps - 1)
    def _():
        drain(slot)

        @pl.when(i >= 1)
        def _():
            drain(1 - slot)


def _dispatch(pos, seg, h, g, *, n_rows, n_experts, be):
    T, D = h.shape
    bm = _largest_divisor(T, (256, 128, 64, 32, 16))
    body = functools.partial(_dispatch_body, bm=bm, T=T, n_experts=n_experts, be=be, n_rows=n_rows)
    return pl.pallas_call(
        body,
        grid_spec=pltpu.PrefetchScalarGridSpec(
            num_scalar_prefetch=2,
            grid=(T // bm,),
            in_specs=[pl.BlockSpec((bm, D), lambda i, pos, seg: (i, 0)),
                      pl.BlockSpec((1, D), lambda i, pos, seg: (0, 0))],
            out_specs=pl.BlockSpec(memory_space=pl.ANY),
            scratch_shapes=[pltpu.VMEM((2, bm, D), F32), pltpu.VMEM((ZERO_ROWS, D), F32),
                            pltpu.SemaphoreType.DMA((3,))],
        ),
        out_shape=jax.ShapeDtypeStruct((n_rows, D), F32),
        compiler_params=_params("arbitrary"),
        name="moe_dispatch",
    )(pos, seg, h, g)


def _expert_body(blk_ref, x_ref, wg_ref, wu_ref, wd_ref, o_ref, xb_ref, *, n_blocks):
    j = pl.program_id(0)

    @pl.when(pl.program_id(1) == 0)
    def _():
        o_ref[...] = jnp.zeros_like(o_ref)

    @pl.when(j < blk_ref[n_blocks])
    def _():
        @pl.when(pl.program_id(1) == 0)
        def _():
            xb_ref[...] = x_ref[...].astype(BF16)

        xb = xb_ref[...]
        hid = jax.nn.silu(_dot(xb, wg_ref[...])) * _dot(xb, wu_ref[...])
        o_ref[...] += _dot(hid.astype(BF16), wd_ref[...])


def _expert_ffn(blk, xs, wg, wu, wd, *, be):
    n_rows, D = xs.shape
    F = wg.shape[2]
    n_blocks = n_rows // be
    fc = _largest_divisor(F, (512, 256, 128))
    nc = F // fc
    body = functools.partial(_expert_body, n_blocks=n_blocks)

    def row_blk(j, blk):
        return jnp.minimum(j, blk[n_blocks] - 1)

    def chunk(j, c, blk):
        return jnp.where(j < blk[n_blocks], c, nc - 1)

    return pl.pallas_call(
        body,
        grid_spec=pltpu.PrefetchScalarGridSpec(
            num_scalar_prefetch=1,
            grid=(n_blocks, nc),
            in_specs=[
                pl.BlockSpec((be, D), lambda j, c, blk: (row_blk(j, blk), 0)),
                pl.BlockSpec((None, D, fc), lambda j, c, blk: (blk[row_blk(j, blk)], 0, chunk(j, c, blk))),
                pl.BlockSpec((None, D, fc), lambda j, c, blk: (blk[row_blk(j, blk)], 0, chunk(j, c, blk))),
                pl.BlockSpec((None, fc, D), lambda j, c, blk: (blk[row_blk(j, blk)], chunk(j, c, blk), 0)),
            ],
            out_specs=pl.BlockSpec((be, D), lambda j, c, blk: (j, 0)),
            scratch_shapes=[pltpu.VMEM((be, D), BF16)],
        ),
        out_shape=jax.ShapeDtypeStruct((n_rows, D), F32),
        compiler_params=_params("arbitrary", "arbitrary"),
        name="moe_experts",
    )(blk, xs, wg, wu, wd)


def _combine_body(pos_ref, h_ref, route_ref, g_ref, y_hbm, o_ref, y_ref, sems, *, bm, T):
    i = pl.program_id(0)
    n_steps = pl.num_programs(0)
    slot = i % 2

    def start_gathers(step, s):
        def row_copy(r, k):
            src = pos_ref[k * T + step * bm + r]
            return pltpu.make_async_copy(y_hbm.at[pl.ds(src, 1), :], y_ref.at[s, k, pl.ds(r, 1), :], sems.at[s])

        def issue(grp, c):
            for u in range(ROW_DMA_UNROLL):
                row_copy(grp * ROW_DMA_UNROLL + u, 0).start(priority=u % 2)
                row_copy(grp * ROW_DMA_UNROLL + u, 1).start(priority=(u + 1) % 2)
            return c

        lax.fori_loop(0, bm // ROW_DMA_UNROLL, issue, 0)

    @pl.when(i == 0)
    def _():
        start_gathers(i, slot)

    @pl.when(i + 1 < n_steps)
    def _():
        start_gathers(i + 1, 1 - slot)

    for k in range(TOP_K):
        pltpu.make_async_copy(y_hbm.at[pl.ds(0, bm), :], y_ref.at[slot, k], sems.at[slot]).wait()

    route = route_ref[...]
    w1 = route[:, ROUTE_W1:ROUTE_W1 + 1]
    w2 = route[:, ROUTE_W2:ROUTE_W2 + 1]
    h = h_ref[...] + (w1 * y_ref[slot, 0] + w2 * y_ref[slot, 1])
    o_ref[...] = _rmsnorm(h, g_ref[...])


def _combine(pos, h, route, g, y, *, B, L):
    T, D = h.shape
    S = L - N_META
    bm = _largest_divisor(L, tuple(b for b in range(512, 7, -8)))
    blocks_per_seq = L // bm
    assert pl.cdiv(S, bm) == blocks_per_seq
    body = functools.partial(_combine_body, bm=bm, T=T)
    return pl.pallas_call(
        body,
        grid_spec=pltpu.PrefetchScalarGridSpec(
            num_scalar_prefetch=1,
            grid=(T // bm,),
            in_specs=[pl.BlockSpec((bm, D), lambda i, pos: (i, 0)),
                      pl.BlockSpec((bm, LANES), lambda i, pos: (i, 0)),
                      pl.BlockSpec((1, D), lambda i, pos: (0, 0)),
                      pl.BlockSpec(memory_space=pl.ANY)],
            out_specs=pl.BlockSpec((None, bm, D), lambda i, pos: (i // blocks_per_seq, i % blocks_per_seq, 0)),
            scratch_shapes=[pltpu.VMEM((2, TOP_K, bm, D), F32), pltpu.SemaphoreType.DMA((2,))],
        ),
        out_shape=jax.ShapeDtypeStruct((B, S, D), F32),
        compiler_params=_params("arbitrary"),
        name="moe_combine",
    )(pos, h, route, g, y)


def _rope_tables(B, L):
    half = HEAD_DIM // 2
    inv = ROPE_THETA ** (-2.0 * jnp.arange(half, dtype=F32) / HEAD_DIM)
    pos = jnp.concatenate([jnp.arange(N_META, L), jnp.arange(N_META)]).astype(F32)
    ang = pos[:, None] * inv[None, :]
    cs = jnp.tile(jnp.stack([jnp.cos(ang), jnp.sin(ang)]), (1, 1, LANES // half))
    cs = jnp.stack([cs * (HEAD_DIM ** -0.5 * LOG2_E), cs])
    return jnp.broadcast_to(cs[:, :, None], (2, 2, B, L, LANES)).reshape(2, 2, B * L, LANES)


def _pair_rotary_halves(w_qk, n_heads):
    D = w_qk.shape[0]
    w = w_qk.reshape(D, 2, n_heads // 2, 2, 2, 2, HEAD_DIM // 2)
    return w.transpose(0, 1, 2, 5, 3, 4, 6).reshape(D, -1)


def _token_mixer(h, layer, B, L, w_in, b_gate, lam, subln_g, pool_w, pool_scale, w_ba, w_bp, w_out, norm_g, rope_cs):
    T, D = h.shape
    A, P = w_ba.shape[0], w_bp.shape[0]
    n_heads = A // (2 * HEAD_DIM)
    lambda_init = 0.8 - 0.6 * math.exp(-0.3 * layer)
    w_qk = _pair_rotary_halves(w_in[:, :2 * A], n_heads).astype(BF16)
    z = _inproj(h, norm_g, w_qk, w_in[:, 2 * A:].astype(BF16), rope_cs)
    z3 = z.reshape(B, L, z.shape[1])
    a = _attention(z3, lam, subln_g, n_heads=n_heads, lambda_init=lambda_init)
    p = _pool(z3, pool_w, pool_scale, col_block=(3 * A) // P)
    return _mix(a.reshape(T, A), p.reshape(T, P), z, b_gate, w_ba, w_bp, w_out, h,
                gate_col_block=(3 * A + P) // (2 * D))


def _moe_and_final_norm(h, norm_g, router, wg, wu, wd, final_g, *, B, L):
    T, D = h.shape
    E = router.shape[1]
    be = 512 if T >= 4096 else 64
    r_pad = jnp.zeros((D, LANES), F32).at[:, :E].set(router)
    r_hi = r_pad.astype(BF16)
    r_lo = (r_pad - r_hi.astype(F32)).astype(BF16)
    route, cnt = _router(h, norm_g, jnp.concatenate([r_hi, r_lo], axis=1), n_experts=E)

    counts = cnt[0, :E].astype(jnp.int32)
    n_blk = (counts + be - 1) // be
    seg_end = jnp.cumsum(n_blk * be)
    seg_start = seg_end - n_blk * be
    n_blocks = (TOP_K * T + E * (be - 1)) // be
    blk_end = jnp.cumsum(n_blk)
    blk_expert = jnp.minimum(jnp.sum(jnp.arange(n_blocks)[:, None] >= blk_end[None, :], axis=1), E - 1)
    blk = jnp.concatenate([blk_expert.astype(jnp.int32), blk_end[-1:].astype(jnp.int32)])
    seg = jnp.concatenate([seg_end, n_blk]).astype(jnp.int32)

    i1 = route[:, ROUTE_I1].astype(jnp.int32)
    i2 = route[:, ROUTE_I2].astype(jnp.int32)
    pos = jnp.concatenate([seg_start[i1] + route[:, ROUTE_R1].astype(jnp.int32),
                           seg_start[i2] + route[:, ROUTE_R2].astype(jnp.int32)])

    xs = _dispatch(pos, seg, h, norm_g, n_rows=n_blocks * be, n_experts=E, be=be)
    y = _expert_ffn(blk, xs, wg, wu, wd, be=be)
    return _combine(pos, h, route, final_g, y, B=B, L=L)


def kernel(x, meta_tokens, norm_mix, w_in, b_gate, lambda_q1, lambda_k1, lambda_q2, lambda_k2, subln, pool_w, pool_scale, w_branch_attn, w_branch_pool, w_out, norm_ffn, dense_w_gate, dense_w_up, dense_w_down, router, moe_w_gate, moe_w_up, moe_w_down, norm_final):
    B, S, D = x.shape
    depth = w_in.shape[0]
    assert depth == 2, "the final RMSNorm is fused into the routed layer, which must be the last one"
    L = N_META + S
    T = B * L
    meta = jnp.broadcast_to(meta_tokens[None].astype(x.dtype), (B, N_META, D))
    h = jnp.concatenate([x, meta], axis=1).reshape(T, D)
    rope = _rope_tables(B, L)
    row = lambda v: v.reshape(1, -1)

    for i in range(depth):
        lam = jnp.stack([lambda_q1[i], lambda_k1[i], lambda_q2[i], lambda_k2[i]])
        h = _token_mixer(h, i, B, L, w_in[i], row(b_gate[i]), lam, row(subln[i]),
                         pool_w[i].astype(BF16), row(pool_scale[i]), w_branch_attn[i].astype(BF16),
                         w_branch_pool[i].astype(BF16), w_out[i].astype(BF16), row(norm_mix[i]), rope)
        j = i // 2
        if i % 2 == 0:
            h = _dense_ffn(h, row(norm_ffn[i]), dense_w_gate[j].astype(BF16), dense_w_up[j].astype(BF16),
                           dense_w_down[j].astype(BF16))
        else:
            h = _moe_and_final_norm(h, row(norm_ffn[i]), router[j], moe_w_gate[j].astype(BF16),
                                    moe_w_up[j].astype(BF16), moe_w_down[j].astype(BF16), row(norm_final),
                                    B=B, L=L)
    return h
```

```python
import functools
import math

import jax
import jax.numpy as jnp
from jax import lax
from jax.experimental import pallas as pl
from jax.experimental.pallas import tpu as pltpu

N_META = 16
HEAD_DIM = 64
POOL_WINDOWS = (2, 4, 8, 16)
TOP_K = 2
ROPE_THETA = 10000.0
EPS = 1e-5

LANES = 128
SUBLANES_BF16 = 16
VMEM_CAP_BYTES = 64 * 1024 * 1024
VMEM_LIMIT_BYTES = VMEM_CAP_BYTES - 6 * 1024 * 1024

NEG_BIG = -1e30
LOG2_E = 1.4426950408889634
POOL_PAD = max(POOL_WINDOWS)
MXU_DEPTH = 256
POOL_BLOCK_ROWS = MXU_DEPTH - 2 * POOL_PAD

F32 = jnp.float32
BF16 = jnp.bfloat16


def _largest_divisor(n, candidates):
    for c in candidates:
        if c <= n and n % c == 0:
            return c
    raise ValueError(f"no block size in {candidates} divides {n}")


def _params(*semantics):
    return pltpu.CompilerParams(dimension_semantics=semantics, vmem_limit_bytes=VMEM_LIMIT_BYTES)


def _rmsnorm(x, g):
    ms = jnp.mean(x * x, axis=-1, keepdims=True)
    return x * lax.rsqrt(ms + EPS) * g


def _dot(a, b):
    return jnp.dot(a, b, preferred_element_type=F32)


def _inproj_body(x_ref, g_ref, wqk_ref, wr_ref, cs_ref, o_ref, xn_ref, *, bn, n_qk_blocks):
    j = pl.program_id(1)

    @pl.when(j == 0)
    def _():
        xn_ref[...] = _rmsnorm(x_ref[...], g_ref[...]).astype(BF16)

    @pl.when(j < n_qk_blocks)
    def _():
        acc = _dot(xn_ref[...], wqk_ref[...])
        cos, sin = cs_ref[0], cs_ref[1]
        for c in range(0, bn, 2 * LANES):
            t1 = acc[:, c:c + LANES]
            t2 = acc[:, c + LANES:c + 2 * LANES]
            o_ref[:, c:c + LANES] = (t1 * cos - t2 * sin).astype(o_ref.dtype)
            o_ref[:, c + LANES:c + 2 * LANES] = (t2 * cos + t1 * sin).astype(o_ref.dtype)

    @pl.when(j >= n_qk_blocks)
    def _():
        o_ref[...] = _dot(xn_ref[...], wr_ref[...]).astype(o_ref.dtype)


def _inproj(h, g, w_qk, w_rest, rope_cs):
    T, D = h.shape
    n_qk, n_rest = w_qk.shape[1], w_rest.shape[1]
    bm = _largest_divisor(T, (768, 512, 384, 256, 128, 64, 32, 16))
    bn = _largest_divisor(math.gcd(n_qk // 2, n_rest), (1024, 512, 256))
    n_qk_blocks = n_qk // bn
    q_blocks = n_qk_blocks // 2
    body = functools.partial(_inproj_body, bn=bn, n_qk_blocks=n_qk_blocks)
    return pl.pallas_call(
        body,
        grid=(T // bm, (n_qk + n_rest) // bn),
        in_specs=[
            pl.BlockSpec((bm, D), lambda i, j: (i, 0)),
            pl.BlockSpec((1, D), lambda i, j: (0, 0)),
            pl.BlockSpec((D, bn), lambda i, j: (0, jnp.minimum(j, n_qk_blocks - 1))),
            pl.BlockSpec((D, bn), lambda i, j: (0, jnp.maximum(j - n_qk_blocks, 0))),
            pl.BlockSpec((None, 2, bm, LANES), lambda i, j: (jnp.minimum(j // q_blocks, 1), 0, i, 0)),
        ],
        out_specs=pl.BlockSpec((bm, bn), lambda i, j: (i, j)),
        out_shape=jax.ShapeDtypeStruct((T, n_qk + n_rest), BF16),
        scratch_shapes=[pltpu.VMEM((bm, D), BF16)],
        compiler_params=_params("parallel", "arbitrary"),
        name="inproj",
    )(h, g, w_qk, w_rest, rope_cs)


PAIR = 2 * LANES


def _attn_body(lam_ref, q_ref, k_ref, v_ref, sg_ref, o_ref, kp_ref, vp_ref, of_ref,
               *, L, l_main, l_keys, bq, lambda_init):
    lane = lax.broadcasted_iota(jnp.int32, (1, PAIR), 1)
    map_of_lane = (lane % HEAD_DIM) // (HEAD_DIM // 2)

    kp_ref[pl.ds(0, L), :] = k_ref[...]
    if l_keys > L:
        kp_ref[pl.ds(L, l_keys - L), :] = jnp.zeros((l_keys - L, PAIR), BF16)
    for hh in range(2):
        vp_ref[hh, pl.ds(0, L), pl.ds(0, LANES)] = v_ref[:, hh * LANES:(hh + 1) * LANES]
        vp_ref[hh, :, pl.ds(LANES, LANES)] = jnp.ones((l_keys, LANES), BF16)
        if l_keys > L:
            vp_ref[hh, pl.ds(L, l_keys - L), pl.ds(0, LANES)] = jnp.zeros((l_keys - L, LANES), BF16)

    lam = lam_ref[...]
    lam_full = (jnp.exp(jnp.sum(lam[0:1] * lam[1:2], axis=-1, keepdims=True))
                - jnp.exp(jnp.sum(lam[2:3] * lam[3:4], axis=-1, keepdims=True)) + lambda_init)

    tail_valid = (lax.broadcasted_iota(jnp.int32, (1, LANES), 1) + l_main) < L
    contract_last = (((1,), (1,)), ((), ()))
    zero = jnp.zeros((), BF16)
    q_all = q_ref[...]

    for hh in range(2):
        in_head = ((lane % LANES) // HEAD_DIM) == hh
        map_masks = (in_head & (map_of_lane == 0), in_head & (map_of_lane == 1))
        for i in range(L // bq):
            qb = q_all[i * bq:(i + 1) * bq]
            q_maps = jnp.concatenate([jnp.where(m, qb, zero) for m in map_masks], axis=0)
            s = lax.dot_general(q_maps, kp_ref[...], contract_last, preferred_element_type=F32)
            s_main = s[:, :l_main]
            m = jnp.max(s_main, axis=-1, keepdims=True)
            if l_keys > l_main:
                s_tail = jnp.where(tail_valid, s[:, l_main:], NEG_BIG)
                m = jnp.maximum(m, jnp.max(s_tail, axis=-1, keepdims=True))
            acc = _dot(jnp.exp2(s_main - m).astype(BF16), vp_ref[hh, pl.ds(0, l_main), :])
            if l_keys > l_main:
                acc = acc + _dot(jnp.exp2(s_tail - m).astype(BF16), vp_ref[hh, pl.ds(l_main, l_keys - l_main), :])
            a0, a1 = acc[:bq], acc[bq:]
            o = a0[:, :LANES] * (1.0 / a0[:, LANES:LANES + 1]) - a1[:, :LANES] * (lam_full / a1[:, LANES:LANES + 1])
            of_ref[pl.ds(i * bq, bq), pl.ds(hh * LANES, LANES)] = _rmsnorm(o, sg_ref[...]) * (1.0 - lambda_init)
    o_ref[...] = of_ref[...].astype(o_ref.dtype)


def _attention(z3, lam, subln_g, *, n_heads, lambda_init):
    B, L, _ = z3.shape
    l_main = (L // LANES) * LANES
    l_keys = l_main if l_main == L else l_main + LANES
    bq = _largest_divisor(L, tuple(b for b in range(400, 7, -8)))
    n_pairs = n_heads // 2
    body = functools.partial(_attn_body, L=L, l_main=l_main, l_keys=l_keys, bq=bq, lambda_init=lambda_init)
    pair = lambda off: pl.BlockSpec((None, L, PAIR), lambda b, h: (b, 0, off + h))
    const = lambda shape: pl.BlockSpec(shape, lambda b, h: (0,) * len(shape))
    return pl.pallas_call(
        body,
        grid=(B, n_pairs),
        in_specs=[const(lam.shape), pair(0), pair(n_pairs), pair(2 * n_pairs), const(subln_g.shape)],
        out_specs=pl.BlockSpec((None, L, PAIR), lambda b, h: (b, 0, h)),
        out_shape=jax.ShapeDtypeStruct((B, L, n_heads * LANES), BF16),
        scratch_shapes=[pltpu.VMEM((l_keys, PAIR), BF16), pltpu.VMEM((2, l_keys, PAIR), BF16),
                        pltpu.VMEM((L, PAIR), F32)],
        compiler_params=_params("parallel", "parallel"),
        name="diff_attention",
    )(lam, z3, z3, z3, subln_g)


def _pool_body(u_ref, band_ref, pw_ref, ps_ref, o_ref, pad_ref, *, L, cg, br):
    S = L - N_META
    P = pad_ref.shape[1]
    pad_ref[pl.ds(0, POOL_PAD), :] = jnp.zeros((POOL_PAD, P), pad_ref.dtype)
    pad_ref[pl.ds(POOL_PAD + L, POOL_PAD), :] = jnp.zeros((POOL_PAD, P), pad_ref.dtype)
    pad_ref[pl.ds(POOL_PAD, N_META), :] = u_ref[pl.ds(S, N_META), :]
    pad_ref[pl.ds(POOL_PAD + N_META, S), :] = u_ref[pl.ds(0, S), :]
    for start in range(0, L, br):
        rows = min(br, L - start)
        t = lax.broadcasted_iota(jnp.int32, (rows, 1), 0) + start
        for g, w in enumerate(POOL_WINDOWS):
            cols = slice(g * cg, (g + 1) * cg)
            win = _dot(band_ref[g, pl.ds(0, rows), pl.ds(0, rows + 2 * POOL_PAD)],
                       pad_ref[pl.ds(start, rows + 2 * POOL_PAD), cols])
            cnt = (jnp.minimum(t + w // 2, L) - jnp.maximum(t - w // 2, 0)).astype(F32)
            m = win * (1.0 / cnt) - pad_ref[pl.ds(start + POOL_PAD, rows), cols].astype(F32)
            y = (_dot(m.astype(BF16), pw_ref[g]) * ps_ref[:, cols]).astype(o_ref.dtype)
            if start == 0:
                o_ref[pl.ds(S, N_META), cols] = y[:N_META]
                o_ref[pl.ds(0, rows - N_META), cols] = y[N_META:]
            else:
                o_ref[pl.ds(start - N_META, rows), cols] = y


def _pool(z3, pool_w, pool_scale, *, col_block):
    B, L, _ = z3.shape
    G, cg, _ = pool_w.shape
    P = G * cg
    br = min(L, POOL_BLOCK_ROWS)
    assert L % SUBLANES_BF16 == 0 and N_META % SUBLANES_BF16 == 0
    r = lax.broadcasted_iota(jnp.int32, (br, br + 2 * POOL_PAD), 0)
    c = lax.broadcasted_iota(jnp.int32, (br, br + 2 * POOL_PAD), 1) - POOL_PAD
    band = jnp.stack([((c >= r - w // 2) & (c < r + w // 2)).astype(BF16) for w in POOL_WINDOWS])
    body = functools.partial(_pool_body, L=L, cg=cg, br=br)
    return pl.pallas_call(
        body,
        grid=(B,),
        in_specs=[
            pl.BlockSpec((None, L, P), lambda b: (b, 0, col_block)),
            pl.BlockSpec(band.shape, lambda b: (0, 0, 0)),
            pl.BlockSpec((G, cg, cg), lambda b: (0, 0, 0)),
            pl.BlockSpec((1, P), lambda b: (0, 0)),
        ],
        out_specs=pl.BlockSpec((None, L, P), lambda b: (b, 0, 0)),
        out_shape=jax.ShapeDtypeStruct((B, L, P), BF16),
        scratch_shapes=[pltpu.VMEM((L + 2 * POOL_PAD, P), BF16)],
        compiler_params=_params("parallel"),
        name="pool_mixer",
    )(z3, band, pool_w, pool_scale)


def _mix_body(a_ref, p_ref, gt_ref, b_ref, wa_ref, wp_ref, wo_ref, h_ref, o_ref, *, D):
    gates = jax.nn.sigmoid(gt_ref[...].astype(F32) + b_ref[...])
    y = gates[:, :D] * _dot(a_ref[...], wa_ref[...]) + gates[:, D:] * _dot(p_ref[...], wp_ref[...])
    o_ref[...] = h_ref[...] + _dot(y.astype(BF16), wo_ref[...])


def _mix(a, p, z, b_gate, wa, wp, wo, h, *, gate_col_block):
    T, D = h.shape
    A, P = a.shape[1], p.shape[1]
    bm = _largest_divisor(T, (256, 128, 64, 32, 16))
    body = functools.partial(_mix_body, D=D)
    const = lambda shape: pl.BlockSpec(shape, lambda i: (0, 0))
    return pl.pallas_call(
        body,
        grid=(T // bm,),
        in_specs=[
            pl.BlockSpec((bm, A), lambda i: (i, 0)),
            pl.BlockSpec((bm, P), lambda i: (i, 0)),
            pl.BlockSpec((bm, 2 * D), lambda i: (i, gate_col_block)),
            const((1, 2 * D)), const((A, D)), const((P, D)), const((D, D)),
            pl.BlockSpec((bm, D), lambda i: (i, 0)),
        ],
        out_specs=pl.BlockSpec((bm, D), lambda i: (i, 0)),
        out_shape=jax.ShapeDtypeStruct((T, D), F32),
        compiler_params=_params("parallel"),
        name="mix_out",
    )(a, p, z, b_gate, wa, wp, wo, h)


def _ffn_body(x_ref, g_ref, wg_ref, wu_ref, wd_ref, o_ref, xn_ref):
    @pl.when(pl.program_id(1) == 0)
    def _():
        x = x_ref[...]
        xn_ref[...] = _rmsnorm(x, g_ref[...]).astype(BF16)
        o_ref[...] = x

    xn = xn_ref[...]
    hid = jax.nn.silu(_dot(xn, wg_ref[...])) * _dot(xn, wu_ref[...])
    o_ref[...] += _dot(hid.astype(BF16), wd_ref[...])


def _dense_ffn(h, g, wg, wu, wd):
    T, D = h.shape
    F = wg.shape[1]
    bm = _largest_divisor(T, (768, 512, 384, 256, 128, 64, 32, 16))
    fc = _largest_divisor(F, (512, 256, 128))
    return pl.pallas_call(
        _ffn_body,
        grid=(T // bm, F // fc),
        in_specs=[
            pl.BlockSpec((bm, D), lambda i, c: (i, 0)),
            pl.BlockSpec((1, D), lambda i, c: (0, 0)),
            pl.BlockSpec((D, fc), lambda i, c: (0, c)),
            pl.BlockSpec((D, fc), lambda i, c: (0, c)),
            pl.BlockSpec((fc, D), lambda i, c: (c, 0)),
        ],
        out_specs=pl.BlockSpec((bm, D), lambda i, c: (i, 0)),
        out_shape=jax.ShapeDtypeStruct((T, D), F32),
        scratch_shapes=[pltpu.VMEM((bm, D), BF16)],
        compiler_params=_params("parallel", "arbitrary"),
        name="dense_ffn",
    )(h, g, wg, wu, wd)


ROUTE_I1, ROUTE_I2, ROUTE_W1, ROUTE_W2, ROUTE_R1, ROUTE_R2 = range(6)


def _router_body(x_ref, g_ref, r_ref, tri_ref, route_ref, cnt_ref, carry_ref, *, n_experts):
    @pl.when(pl.program_id(0) == 0)
    def _():
        carry_ref[...] = jnp.zeros_like(carry_ref)

    xn = _rmsnorm(x_ref[...], g_ref[...])
    x_hi = xn.astype(BF16)
    x_lo = (xn - x_hi.astype(F32)).astype(BF16)
    hi_both = _dot(x_hi, r_ref[...])
    logits = hi_both[:, :LANES] + (hi_both[:, LANES:] + _dot(x_lo, r_ref[:, pl.ds(0, LANES)]))

    lane = lax.broadcasted_iota(jnp.int32, (1, LANES), 1)
    lane_f = lane.astype(F32)
    lg = jnp.where(lane < n_experts, logits, NEG_BIG)
    v1 = jnp.max(lg, axis=-1, keepdims=True)
    i1 = jnp.min(jnp.where(lg == v1, lane_f, float(LANES)), axis=-1, keepdims=True)
    lg2 = jnp.where(lane_f == i1, NEG_BIG, lg)
    v2 = jnp.max(lg2, axis=-1, keepdims=True)
    i2 = jnp.min(jnp.where(lg2 == v2, lane_f, float(LANES)), axis=-1, keepdims=True)
    e2 = jnp.exp(v2 - v1)
    w1 = 1.0 / (1.0 + e2)
    w2 = e2 / (1.0 + e2)

    sel1 = lane_f == i1
    sel2 = lane_f == i2
    onehot = jnp.where(sel1 | sel2, 1.0, 0.0)
    rank = _dot(tri_ref[...], onehot.astype(BF16)) + carry_ref[...]
    r1 = jnp.sum(jnp.where(sel1, rank, 0.0), axis=-1, keepdims=True)
    r2 = jnp.sum(jnp.where(sel2, rank, 0.0), axis=-1, keepdims=True)
    carry_ref[...] += jnp.sum(onehot, axis=0, keepdims=True)

    route = jnp.zeros(route_ref.shape, F32)
    for col, val in ((ROUTE_I1, i1), (ROUTE_I2, i2), (ROUTE_W1, w1), (ROUTE_W2, w2), (ROUTE_R1, r1), (ROUTE_R2, r2)):
        route = jnp.where(lane == col, val, route)
    route_ref[...] = route
    cnt_ref[...] = jnp.broadcast_to(carry_ref[...], cnt_ref.shape)


def _router(h, g, r_hi_lo, *, n_experts):
    T, D = h.shape
    bm = _largest_divisor(T, (256, 128, 64, 32, 16))
    tri = (lax.broadcasted_iota(jnp.int32, (bm, bm), 0) > lax.broadcasted_iota(jnp.int32, (bm, bm), 1)).astype(BF16)
    body = functools.partial(_router_body, n_experts=n_experts)
    const = lambda shape: pl.BlockSpec(shape, lambda i: (0, 0))
    return pl.pallas_call(
        body,
        grid=(T // bm,),
        in_specs=[pl.BlockSpec((bm, D), lambda i: (i, 0)), const((1, D)), const((D, 2 * LANES)), const((bm, bm))],
        out_specs=[pl.BlockSpec((bm, LANES), lambda i: (i, 0)), const((8, LANES))],
        out_shape=[jax.ShapeDtypeStruct((T, LANES), F32), jax.ShapeDtypeStruct((8, LANES), F32)],
        scratch_shapes=[pltpu.VMEM((1, LANES), F32)],
        compiler_params=_params("arbitrary"),
        name="moe_router",
    )(h, g, r_hi_lo, tri)


ZERO_ROWS = 64
ROW_DMA_UNROLL = 8


def _dispatch_body(pos_ref, seg_ref, x_ref, g_ref, xs_hbm, xn_ref, zero_ref, sems, *, bm, T, n_experts, be, n_rows):
    i = pl.program_id(0)
    n_steps = pl.num_programs(0)
    slot = i % 2

    def zero_copy(first_row, c):
        row = pl.multiple_of(first_row + c * ZERO_ROWS, ZERO_ROWS)
        return pltpu.make_async_copy(zero_ref, xs_hbm.at[pl.ds(row, ZERO_ROWS), :], sems.at[2])

    def zero_block(first_row):
        for c in range(be // ZERO_ROWS):
            zero_copy(first_row, c).start()
        for c in range(be // ZERO_ROWS):
            zero_copy(first_row, c).wait()

    @pl.when(i == 0)
    def _():
        zero_ref[...] = jnp.zeros_like(zero_ref)
        for e in range(n_experts):
            @pl.when(seg_ref[n_experts + e] > 0)
            def _():
                zero_block(seg_ref[e] - be)

        def clear_unused(j, c):
            zero_block(j * be)
            return c

        lax.fori_loop(seg_ref[n_experts - 1] // be, n_rows // be, clear_unused, 0)

    def drain(s):
        for _ in range(TOP_K):
            pltpu.make_async_copy(xn_ref.at[s], xs_hbm.at[pl.ds(0, bm), :], sems.at[s]).wait()

    @pl.when(i >= 2)
    def _():
        drain(slot)

    xn_ref[slot] = _rmsnorm(x_ref[...], g_ref[...])

    def row_copy(r, k):
        dst = pos_ref[k * T + i * bm + r]
        return pltpu.make_async_copy(xn_ref.at[slot, pl.ds(r, 1), :], xs_hbm.at[pl.ds(dst, 1), :], sems.at[slot])

    def issue(grp, c):
        for u in range(ROW_DMA_UNROLL):
            row_copy(grp * ROW_DMA_UNROLL + u, 0).start(priority=u % 2)
            row_copy(grp * ROW_DMA_UNROLL + u, 1).start(priority=(u + 1) % 2)
        return c

    lax.fori_loop(0, bm // ROW_DMA_UNROLL, issue, 0)

    @pl.when(i == n_steps - 1)
    def _():
        drain(slot)

        @pl.when(i >= 1)
        def _():
            drain(1 - slot)


def _dispatch(pos, seg, h, g, *, n_rows, n_experts, be):
    T, D = h.shape
    bm = _largest_divisor(T, (256, 128, 64, 32, 16))
    body = functools.partial(_dispatch_body, bm=bm, T=T, n_experts=n_experts, be=be, n_rows=n_rows)
    return pl.pallas_call(
        body,
        grid_spec=pltpu.PrefetchScalarGridSpec(
            num_scalar_prefetch=2,
            grid=(T // bm,),
            in_specs=[pl.BlockSpec((bm, D), lambda i, pos, seg: (i, 0)),
                      pl.BlockSpec((1, D), lambda i, pos, seg: (0, 0))],
            out_specs=pl.BlockSpec(memory_space=pl.ANY),
            scratch_shapes=[pltpu.VMEM((2, bm, D), F32), pltpu.VMEM((ZERO_ROWS, D), F32),
                            pltpu.SemaphoreType.DMA((3,))],
        ),
        out_shape=jax.ShapeDtypeStruct((n_rows, D), F32),
        compiler_params=_params("arbitrary"),
        name="moe_dispatch",
    )(pos, seg, h, g)


def _expert_body(blk_ref, x_ref, wg_ref, wu_ref, wd_ref, o_ref, xb_ref, *, n_blocks):
    j = pl.program_id(0)

    @pl.when(pl.program_id(1) == 0)
    def _():
        o_ref[...] = jnp.zeros_like(o_ref)

    @pl.when(j < blk_ref[n_blocks])
    def _():
        @pl.when(pl.program_id(1) == 0)
        def _():
            xb_ref[...] = x_ref[...].astype(BF16)

        xb = xb_ref[...]
        hid = jax.nn.silu(_dot(xb, wg_ref[...])) * _dot(xb, wu_ref[...])
        o_ref[...] += _dot(hid.astype(BF16), wd_ref[...])


def _expert_ffn(blk, xs, wg, wu, wd, *, be):
    n_rows, D = xs.shape
    F = wg.shape[2]
    n_blocks = n_rows // be
    fc = _largest_divisor(F, (512, 256, 128))
    nc = F // fc
    body = functools.partial(_expert_body, n_blocks=n_blocks)

    def row_blk(j, blk):
        return jnp.minimum(j, blk[n_blocks] - 1)

    def chunk(j, c, blk):
        return jnp.where(j < blk[n_blocks], c, nc - 1)

    return pl.pallas_call(
        body,
        grid_spec=pltpu.PrefetchScalarGridSpec(
            num_scalar_prefetch=1,
            grid=(n_blocks, nc),
            in_specs=[
                pl.BlockSpec((be, D), lambda j, c, blk: (row_blk(j, blk), 0)),
                pl.BlockSpec((None, D, fc), lambda j, c, blk: (blk[row_blk(j, blk)], 0, chunk(j, c, blk))),
                pl.BlockSpec((None, D, fc), lambda j, c, blk: (blk[row_blk(j, blk)], 0, chunk(j, c, blk))),
                pl.BlockSpec((None, fc, D), lambda j, c, blk: (blk[row_blk(j, blk)], chunk(j, c, blk), 0)),
            ],
            out_specs=pl.BlockSpec((be, D), lambda j, c, blk: (j, 0)),
            scratch_shapes=[pltpu.VMEM((be, D), BF16)],
        ),
        out_shape=jax.ShapeDtypeStruct((n_rows, D), F32),
        compiler_params=_params("arbitrary", "arbitrary"),
        name="moe_experts",
    )(blk, xs, wg, wu, wd)


def _combine_body(pos_ref, h_ref, route_ref, g_ref, y_hbm, o_ref, y_ref, sems, *, bm, T):
    i = pl.program_id(0)
    n_steps = pl.num_programs(0)
    slot = i % 2

    def start_gathers(step, s):
        def row_copy(r, k):
            src = pos_ref[k * T + step * bm + r]
            return pltpu.make_async_copy(y_hbm.at[pl.ds(src, 1), :], y_ref.at[s, k, pl.ds(r, 1), :], sems.at[s])

        def issue(grp, c):
            for u in range(ROW_DMA_UNROLL):
                row_copy(grp * ROW_DMA_UNROLL + u, 0).start(priority=u % 2)
                row_copy(grp * ROW_DMA_UNROLL + u, 1).start(priority=(u + 1) % 2)
            return c

        lax.fori_loop(0, bm // ROW_DMA_UNROLL, issue, 0)

    @pl.when(i == 0)
    def _():
        start_gathers(i, slot)

    @pl.when(i + 1 < n_steps)
    def _():
        start_gathers(i + 1, 1 - slot)

    for k in range(TOP_K):
        pltpu.make_async_copy(y_hbm.at[pl.ds(0, bm), :], y_ref.at[slot, k], sems.at[slot]).wait()

    route = route_ref[...]
    w1 = route[:, ROUTE_W1:ROUTE_W1 + 1]
    w2 = route[:, ROUTE_W2:ROUTE_W2 + 1]
    h = h_ref[...] + (w1 * y_ref[slot, 0] + w2 * y_ref[slot, 1])
    o_ref[...] = _rmsnorm(h, g_ref[...])


def _combine(pos, h, route, g, y, *, B, L):
    T, D = h.shape
    S = L - N_META
    bm = _largest_divisor(L, tuple(b for b in range(512, 7, -8)))
    blocks_per_seq = L // bm
    assert pl.cdiv(S, bm) == blocks_per_seq
    body = functools.partial(_combine_body, bm=bm, T=T)
    return pl.pallas_call(
        body,
        grid_spec=pltpu.PrefetchScalarGridSpec(
            num_scalar_prefetch=1,
            grid=(T // bm,),
            in_specs=[pl.BlockSpec((bm, D), lambda i, pos: (i, 0)),
                      pl.BlockSpec((bm, LANES), lambda i, pos: (i, 0)),
                      pl.BlockSpec((1, D), lambda i, pos: (0, 0)),
                      pl.BlockSpec(memory_space=pl.ANY)],
            out_specs=pl.BlockSpec((None, bm, D), lambda i, pos: (i // blocks_per_seq, i % blocks_per_seq, 0)),
            scratch_shapes=[pltpu.VMEM((2, TOP_K, bm, D), F32), pltpu.SemaphoreType.DMA((2,))],
        ),
        out_shape=jax.ShapeDtypeStruct((B, S, D), F32),
        compiler_params=_params("arbitrary"),
        name="moe_combine",
    )(pos, h, route, g, y)


def _rope_tables(B, L):
    half = HEAD_DIM // 2
    inv = ROPE_THETA ** (-2.0 * jnp.arange(half, dtype=F32) / HEAD_DIM)
    pos = jnp.concatenate([jnp.arange(N_META, L), jnp.arange(N_META)]).astype(F32)
    ang = pos[:, None] * inv[None, :]
    cs = jnp.tile(jnp.stack([jnp.cos(ang), jnp.sin(ang)]), (1, 1, LANES // half))
    cs = jnp.stack([cs * (HEAD_DIM ** -0.5 * LOG2_E), cs])
    return jnp.broadcast_to(cs[:, :, None], (2, 2, B, L, LANES)).reshape(2, 2, B * L, LANES)


def _pair_rotary_halves(w_qk, n_heads):
    D = w_qk.shape[0]
    w = w_qk.reshape(D, 2, n_heads // 2, 2, 2, 2, HEAD_DIM // 2)
    return w.transpose(0, 1, 2, 5, 3, 4, 6).reshape(D, -1)


def _token_mixer(h, layer, B, L, w_in, b_gate, lam, subln_g, pool_w, pool_scale, w_ba, w_bp, w_out, norm_g, rope_cs):
    T, D = h.shape
    A, P = w_ba.shape[0], w_bp.shape[0]
    n_heads = A // (2 * HEAD_DIM)
    lambda_init = 0.8 - 0.6 * math.exp(-0.3 * layer)
    w_qk = _pair_rotary_halves(w_in[:, :2 * A], n_heads).astype(BF16)
    z = _inproj(h, norm_g, w_qk, w_in[:, 2 * A:].astype(BF16), rope_cs)
    z3 = z.reshape(B, L, z.shape[1])
    a = _attention(z3, lam, subln_g, n_heads=n_heads, lambda_init=lambda_init)
    p = _pool(z3, pool_w, pool_scale, col_block=(3 * A) // P)
    return _mix(a.reshape(T, A), p.reshape(T, P), z, b_gate, w_ba, w_bp, w_out, h,
                gate_col_block=(3 * A + P) // (2 * D))


def _moe_and_final_norm(h, norm_g, router, wg, wu, wd, final_g, *, B, L):
    T, D = h.shape
    E = router.shape[1]
    be = 512 if T >= 4096 else 64
    r_pad = jnp.zeros((D, LANES), F32).at[:, :E].set(router)
    r_hi = r_pad.astype(BF16)
    r_lo = (r_pad - r_hi.astype(F32)).astype(BF16)
    route, cnt = _router(h, norm_g, jnp.concatenate([r_hi, r_lo], axis=1), n_experts=E)

    counts = cnt[0, :E].astype(jnp.int32)
    n_blk = (counts + be - 1) // be
    seg_end = jnp.cumsum(n_blk * be)
    seg_start = seg_end - n_blk * be
    n_blocks = (TOP_K * T + E * (be - 1)) // be
    blk_end = jnp.cumsum(n_blk)
    blk_expert = jnp.minimum(jnp.sum(jnp.arange(n_blocks)[:, None] >= blk_end[None, :], axis=1), E - 1)
    blk = jnp.concatenate([blk_expert.astype(jnp.int32), blk_end[-1:].astype(jnp.int32)])
    seg = jnp.concatenate([seg_end, n_blk]).astype(jnp.int32)

    i1 = route[:, ROUTE_I1].astype(jnp.int32)
    i2 = route[:, ROUTE_I2].astype(jnp.int32)
    pos = jnp.concatenate([seg_start[i1] + route[:, ROUTE_R1].astype(jnp.int32),
                           seg_start[i2] + route[:, ROUTE_R2].astype(jnp.int32)])

    xs = _dispatch(pos, seg, h, norm_g, n_rows=n_blocks * be, n_experts=E, be=be)
    y = _expert_ffn(blk, xs, wg, wu, wd, be=be)
    return _combine(pos, h, route, final_g, y, B=B, L=L)


def kernel(x, meta_tokens, norm_mix, w_in, b_gate, lambda_q1, lambda_k1, lambda_q2, lambda_k2, subln, pool_w, pool_scale, w_branch_attn, w_branch_pool, w_out, norm_ffn, dense_w_gate, dense_w_up, dense_w_down, router, moe_w_gate, moe_w_up, moe_w_down, norm_final):
    B, S, D = x.shape
    depth = w_in.shape[0]
    assert depth == 2, "the final RMSNorm is fused into the routed layer, which must be the last one"
    L = N_META + S
    T = B * L
    meta = jnp.broadcast_to(meta_tokens[None].astype(x.dtype), (B, N_META, D))
    h = jnp.concatenate([x, meta], axis=1).reshape(T, D)
    rope = _rope_tables(B, L)
    row = lambda v: v.reshape(1, -1)

    for i in range(depth):
        lam = jnp.stack([lambda_q1[i], lambda_k1[i], lambda_q2[i], lambda_k2[i]])
        h = _token_mixer(h, i, B, L, w_in[i], row(b_gate[i]), lam, row(subln[i]),
                         pool_w[i].astype(BF16), row(pool_scale[i]), w_branch_attn[i].astype(BF16),
                         w_branch_pool[i].astype(BF16), w_out[i].astype(BF16), row(norm_mix[i]), rope)
        j = i // 2
        if i % 2 == 0:
            h = _dense_ffn(h, row(norm_ffn[i]), dense_w_gate[j].astype(BF16), dense_w_up[j].astype(BF16),
                           dense_w_down[j].astype(BF16))
        else:
            h = _moe_and_final_norm(h, row(norm_ffn[i]), router[j], moe_w_gate[j].astype(BF16),
                                    moe_w_up[j].astype(BF16), moe_w_down[j].astype(BF16), row(norm_final),
                                    B=B, L=L)
    return h
```

```python
import functools
import math

import jax
import jax.numpy as jnp
from jax import lax
from jax.experimental import pallas as pl
from jax.experimental.pallas import tpu as pltpu

N_META = 16
HEAD_DIM = 64
POOL_WINDOWS = (2, 4, 8, 16)
TOP_K = 2
ROPE_THETA = 10000.0
EPS = 1e-5

LANES = 128
SUBLANES_BF16 = 16
VMEM_CAP_BYTES = 64 * 1024 * 1024
VMEM_LIMIT_BYTES = VMEM_CAP_BYTES - 6 * 1024 * 1024

NEG_BIG = -1e30
LOG2_E = 1.4426950408889634
POOL_PAD = max(POOL_WINDOWS)
MXU_DEPTH = 256
POOL_BLOCK_ROWS = MXU_DEPTH - 2 * POOL_PAD

F32 = jnp.float32
BF16 = jnp.bfloat16


def _largest_divisor(n, candidates):
    for c in candidates:
        if c <= n and n % c == 0:
            return c
    raise ValueError(f"no block size in {candidates} divides {n}")


def _params(*semantics):
    return pltpu.CompilerParams(dimension_semantics=semantics, vmem_limit_bytes=VMEM_LIMIT_BYTES)


def _rmsnorm(x, g):
    ms = jnp.mean(x * x, axis=-1, keepdims=True)
    return x * lax.rsqrt(ms + EPS) * g


def _dot(a, b):
    return jnp.dot(a, b, preferred_element_type=F32)


def _inproj_body(x_ref, g_ref, wqk_ref, wr_ref, cs_ref, o_ref, xn_ref, *, n_qk):
    j = pl.program_id(1)

    @pl.when(j == 0)
    def _():
        xn_ref[...] = _rmsnorm(x_ref[...], g_ref[...]).astype(BF16)
        acc = _dot(xn_ref[...], wqk_ref[...])
        for c in range(0, n_qk, 2 * LANES):
            table = 0 if c < n_qk // 2 else 1
            cos, sin = cs_ref[table, 0], cs_ref[table, 1]
            t1 = acc[:, c:c + LANES]
            t2 = acc[:, c + LANES:c + 2 * LANES]
            o_ref[:, c:c + LANES] = (t1 * cos - t2 * sin).astype(o_ref.dtype)
            o_ref[:, c + LANES:c + 2 * LANES] = (t2 * cos + t1 * sin).astype(o_ref.dtype)

    @pl.when(j > 0)
    def _():
        o_ref[...] = _dot(xn_ref[...], wr_ref[...]).astype(o_ref.dtype)


def _inproj(h, g, w_qk, w_rest, rope_cs):
    T, D = h.shape
    n_qk, n_rest = w_qk.shape[1], w_rest.shape[1]
    assert n_rest % n_qk == 0
    bm = _largest_divisor(T, (768, 512, 384, 256, 128, 64, 32, 16))
    body = functools.partial(_inproj_body, n_qk=n_qk)
    return pl.pallas_call(
        body,
        grid=(T // bm, 1 + n_rest // n_qk),
        in_specs=[
            pl.BlockSpec((bm, D), lambda i, j: (i, 0)),
            pl.BlockSpec((1, D), lambda i, j: (0, 0)),
            pl.BlockSpec((D, n_qk), lambda i, j: (0, 0), pipeline_mode=pl.Buffered(1)),
            pl.BlockSpec((D, n_qk), lambda i, j: (0, jnp.maximum(j - 1, 0))),
            pl.BlockSpec((2, 2, bm, LANES), lambda i, j: (0, 0, i, 0)),
        ],
        out_specs=pl.BlockSpec((bm, n_qk), lambda i, j: (i, j)),
        out_shape=jax.ShapeDtypeStruct((T, n_qk + n_rest), BF16),
        scratch_shapes=[pltpu.VMEM((bm, D), BF16)],
        compiler_params=_params("parallel", "arbitrary"),
        name="inproj",
    )(h, g, w_qk, w_rest, rope_cs)


PAIR = 2 * LANES


def _attn_body(lam_ref, q_ref, k_ref, v_ref, sg_ref, o_ref, kp_ref, vp_ref, of_ref,
               *, L, l_main, l_keys, bq, lambda_init):
    lane = lax.broadcasted_iota(jnp.int32, (1, PAIR), 1)
    map_of_lane = (lane % HEAD_DIM) // (HEAD_DIM // 2)

    kp_ref[pl.ds(0, L), :] = k_ref[...]
    if l_keys > L:
        kp_ref[pl.ds(L, l_keys - L), :] = jnp.zeros((l_keys - L, PAIR), BF16)
    for hh in range(2):
        vp_ref[hh, pl.ds(0, L), pl.ds(0, LANES)] = v_ref[:, hh * LANES:(hh + 1) * LANES]
        vp_ref[hh, :, pl.ds(LANES, LANES)] = jnp.ones((l_keys, LANES), BF16)
        if l_keys > L:
            vp_ref[hh, pl.ds(L, l_keys - L), pl.ds(0, LANES)] = jnp.zeros((l_keys - L, LANES), BF16)

    lam = lam_ref[...]
    lam_full = (jnp.exp(jnp.sum(lam[0:1] * lam[1:2], axis=-1, keepdims=True))
                - jnp.exp(jnp.sum(lam[2:3] * lam[3:4], axis=-1, keepdims=True)) + lambda_init)

    tail_valid = (lax.broadcasted_iota(jnp.int32, (1, LANES), 1) + l_main) < L
    contract_last = (((1,), (1,)), ((), ()))
    zero = jnp.zeros((), BF16)
    q_all = q_ref[...]

    for hh in range(2):
        in_head = ((lane % LANES) // HEAD_DIM) == hh
        map_masks = (in_head & (map_of_lane == 0), in_head & (map_of_lane == 1))
        for i in range(L // bq):
            qb = q_all[i * bq:(i + 1) * bq]
            q_maps = jnp.concatenate([jnp.where(m, qb, zero) for m in map_masks], axis=0)
            s = lax.dot_general(q_maps, kp_ref[...], contract_last, preferred_element_type=F32)
            s_main = s[:, :l_main]
            m = jnp.max(s_main, axis=-1, keepdims=True)
            if l_keys > l_main:
                s_tail = jnp.where(tail_valid, s[:, l_main:], NEG_BIG)
                m = jnp.maximum(m, jnp.max(s_tail, axis=-1, keepdims=True))
            acc = _dot(jnp.exp2(s_main - m).astype(BF16), vp_ref[hh, pl.ds(0, l_main), :])
            if l_keys > l_main:
                acc = acc + _dot(jnp.exp2(s_tail - m).astype(BF16), vp_ref[hh, pl.ds(l_main, l_keys - l_main), :])
            a0, a1 = acc[:bq], acc[bq:]
            o = a0[:, :LANES] * (1.0 / a0[:, LANES:LANES + 1]) - a1[:, :LANES] * (lam_full / a1[:, LANES:LANES + 1])
            of_ref[pl.ds(i * bq, bq), pl.ds(hh * LANES, LANES)] = _rmsnorm(o, sg_ref[...]) * (1.0 - lambda_init)
    o_ref[...] = of_ref[...].astype(o_ref.dtype)


def _attention(z3, lam, subln_g, *, n_heads, lambda_init):
    B, L, _ = z3.shape
    l_main = (L // LANES) * LANES
    l_keys = l_main if l_main == L else l_main + LANES
    bq = _largest_divisor(L, tuple(b for b in range(400, 7, -8)))
    n_pairs = n_heads // 2
    body = functools.partial(_attn_body, L=L, l_main=l_main, l_keys=l_keys, bq=bq, lambda_init=lambda_init)
    pair = lambda off: pl.BlockSpec((None, L, PAIR), lambda b, h: (b, 0, off + h))
    const = lambda shape: pl.BlockSpec(shape, lambda b, h: (0,) * len(shape))
    return pl.pallas_call(
        body,
        grid=(B, n_pairs),
        in_specs=[const(lam.shape), pair(0), pair(n_pairs), pair(2 * n_pairs), const(subln_g.shape)],
        out_specs=pl.BlockSpec((None, L, PAIR), lambda b, h: (b, 0, h)),
        out_shape=jax.ShapeDtypeStruct((B, L, n_heads * LANES), BF16),
        scratch_shapes=[pltpu.VMEM((l_keys, PAIR), BF16), pltpu.VMEM((2, l_keys, PAIR), BF16),
                        pltpu.VMEM((L, PAIR), F32)],
        compiler_params=_params("parallel", "parallel"),
        name="diff_attention",
    )(lam, z3, z3, z3, subln_g)


def _pool_body(u_ref, band_ref, pw_ref, ps_ref, o_ref, pad_ref, *, L, cg, br):
    S = L - N_META
    P = pad_ref.shape[1]
    pad_ref[pl.ds(0, POOL_PAD), :] = jnp.zeros((POOL_PAD, P), pad_ref.dtype)
    pad_ref[pl.ds(POOL_PAD + L, POOL_PAD), :] = jnp.zeros((POOL_PAD, P), pad_ref.dtype)
    pad_ref[pl.ds(POOL_PAD, N_META), :] = u_ref[pl.ds(S, N_META), :]
    pad_ref[pl.ds(POOL_PAD + N_META, S), :] = u_ref[pl.ds(0, S), :]
    for start in range(0, L, br):
        rows = min(br, L - start)
        t = lax.broadcasted_iota(jnp.int32, (rows, 1), 0) + start
        for g, w in enumerate(POOL_WINDOWS):
            cols = slice(g * cg, (g + 1) * cg)
            win = _dot(band_ref[g, pl.ds(0, rows), pl.ds(0, rows + 2 * POOL_PAD)],
                       pad_ref[pl.ds(start, rows + 2 * POOL_PAD), cols])
            cnt = (jnp.minimum(t + w // 2, L) - jnp.maximum(t - w // 2, 0)).astype(F32)
            m = win * (1.0 / cnt) - pad_ref[pl.ds(start + POOL_PAD, rows), cols].astype(F32)
            y = (_dot(m.astype(BF16), pw_ref[g]) * ps_ref[:, cols]).astype(o_ref.dtype)
            if start == 0:
                o_ref[pl.ds(S, N_META), cols] = y[:N_META]
                o_ref[pl.ds(0, rows - N_META), cols] = y[N_META:]
            else:
                o_ref[pl.ds(start - N_META, rows), cols] = y


def _pool(z3, pool_w, pool_scale, *, col_block):
    B, L, _ = z3.shape
    G, cg, _ = pool_w.shape
    P = G * cg
    br = min(L, POOL_BLOCK_ROWS)
    assert L % SUBLANES_BF16 == 0 and N_META % SUBLANES_BF16 == 0
    r = lax.broadcasted_iota(jnp.int32, (br, br + 2 * POOL_PAD), 0)
    c = lax.broadcasted_iota(jnp.int32, (br, br + 2 * POOL_PAD), 1) - POOL_PAD
    band = jnp.stack([((c >= r - w // 2) & (c < r + w // 2)).astype(BF16) for w in POOL_WINDOWS])
    body = functools.partial(_pool_body, L=L, cg=cg, br=br)
    return pl.pallas_call(
        body,
        grid=(B,),
        in_specs=[
            pl.BlockSpec((None, L, P), lambda b: (b, 0, col_block)),
            pl.BlockSpec(band.shape, lambda b: (0, 0, 0)),
            pl.BlockSpec((G, cg, cg), lambda b: (0, 0, 0)),
            pl.BlockSpec((1, P), lambda b: (0, 0)),
        ],
        out_specs=pl.BlockSpec((None, L, P), lambda b: (b, 0, 0)),
        out_shape=jax.ShapeDtypeStruct((B, L, P), BF16),
        scratch_shapes=[pltpu.VMEM((L + 2 * POOL_PAD, P), BF16)],
        compiler_params=_params("parallel"),
        name="pool_mixer",
    )(z3, band, pool_w, pool_scale)


def _mix_body(a_ref, p_ref, gt_ref, b_ref, wa_ref, wp_ref, wo_ref, h_ref, o_ref, *, D):
    gates = jax.nn.sigmoid(gt_ref[...].astype(F32) + b_ref[...])
    y = gates[:, :D] * _dot(a_ref[...], wa_ref[...]) + gates[:, D:] * _dot(p_ref[...], wp_ref[...])
    o_ref[...] = h_ref[...] + _dot(y.astype(BF16), wo_ref[...])


def _mix(a, p, z, b_gate, wa, wp, wo, h, *, gate_col_block):
    T, D = h.shape
    A, P = a.shape[1], p.shape[1]
    bm = _largest_divisor(T, (384, 256, 128, 64, 32, 16))
    body = functools.partial(_mix_body, D=D)
    const = lambda shape: pl.BlockSpec(shape, lambda i: (0, 0), pipeline_mode=pl.Buffered(1))
    return pl.pallas_call(
        body,
        grid=(T // bm,),
        in_specs=[
            pl.BlockSpec((bm, A), lambda i: (i, 0)),
            pl.BlockSpec((bm, P), lambda i: (i, 0)),
            pl.BlockSpec((bm, 2 * D), lambda i: (i, gate_col_block)),
            const((1, 2 * D)), const((A, D)), const((P, D)), const((D, D)),
            pl.BlockSpec((bm, D), lambda i: (i, 0)),
        ],
        out_specs=pl.BlockSpec((bm, D), lambda i: (i, 0)),
        out_shape=jax.ShapeDtypeStruct((T, D), F32),
        compiler_params=_params("parallel"),
        name="mix_out",
    )(a, p, z, b_gate, wa, wp, wo, h)


def _ffn_body(x_ref, g_ref, wg_ref, wu_ref, wd_ref, o_ref, xn_ref):
    @pl.when(pl.program_id(1) == 0)
    def _():
        x = x_ref[...]
        xn_ref[...] = _rmsnorm(x, g_ref[...]).astype(BF16)
        o_ref[...] = x

    xn = xn_ref[...]
    hid = jax.nn.silu(_dot(xn, wg_ref[...])) * _dot(xn, wu_ref[...])
    o_ref[...] += _dot(hid.astype(BF16), wd_ref[...])


def _dense_ffn(h, g, wg, wu, wd):
    T, D = h.shape
    F = wg.shape[1]
    bm = _largest_divisor(T, (768, 512, 384, 256, 128, 64, 32, 16))
    fc = _largest_divisor(F, (512, 256, 128))
    return pl.pallas_call(
        _ffn_body,
        grid=(T // bm, F // fc),
        in_specs=[
            pl.BlockSpec((bm, D), lambda i, c: (i, 0)),
            pl.BlockSpec((1, D), lambda i, c: (0, 0)),
            pl.BlockSpec((D, fc), lambda i, c: (0, c)),
            pl.BlockSpec((D, fc), lambda i, c: (0, c)),
            pl.BlockSpec((fc, D), lambda i, c: (c, 0)),
        ],
        out_specs=pl.BlockSpec((bm, D), lambda i, c: (i, 0)),
        out_shape=jax.ShapeDtypeStruct((T, D), F32),
        scratch_shapes=[pltpu.VMEM((bm, D), BF16)],
        compiler_params=_params("parallel", "arbitrary"),
        name="dense_ffn",
    )(h, g, wg, wu, wd)


ROUTE_I1, ROUTE_I2, ROUTE_W1, ROUTE_W2, ROUTE_R1, ROUTE_R2 = range(6)


def _router_body(x_ref, g_ref, r_ref, tri_ref, route_ref, cnt_ref, carry_ref, *, n_experts):
    @pl.when(pl.program_id(0) == 0)
    def _():
        carry_ref[...] = jnp.zeros_like(carry_ref)

    xn = _rmsnorm(x_ref[...], g_ref[...])
    x_hi = xn.astype(BF16)
    x_lo = (xn - x_hi.astype(F32)).astype(BF16)
    hi_both = _dot(x_hi, r_ref[...])
    logits = hi_both[:, :LANES] + (hi_both[:, LANES:] + _dot(x_lo, r_ref[:, pl.ds(0, LANES)]))

    lane = lax.broadcasted_iota(jnp.int32, (1, LANES), 1)
    lane_f = lane.astype(F32)
    lg = jnp.where(lane < n_experts, logits, NEG_BIG)
    v1 = jnp.max(lg, axis=-1, keepdims=True)
    i1 = jnp.min(jnp.where(lg == v1, lane_f, float(LANES)), axis=-1, keepdims=True)
    lg2 = jnp.where(lane_f == i1, NEG_BIG, lg)
    v2 = jnp.max(lg2, axis=-1, keepdims=True)
    i2 = jnp.min(jnp.where(lg2 == v2, lane_f, float(LANES)), axis=-1, keepdims=True)
    e2 = jnp.exp(v2 - v1)
    w1 = 1.0 / (1.0 + e2)
    w2 = e2 / (1.0 + e2)

    sel1 = lane_f == i1
    sel2 = lane_f == i2
    onehot = jnp.where(sel1 | sel2, 1.0, 0.0)
    rank = _dot(tri_ref[...], onehot.astype(BF16)) + carry_ref[...]
    r1 = jnp.sum(jnp.where(sel1, rank, 0.0), axis=-1, keepdims=True)
    r2 = jnp.sum(jnp.where(sel2, rank, 0.0), axis=-1, keepdims=True)
    carry_ref[...] += jnp.sum(onehot, axis=0, keepdims=True)

    route = jnp.zeros(route_ref.shape, F32)
    for col, val in ((ROUTE_I1, i1), (ROUTE_I2, i2), (ROUTE_W1, w1), (ROUTE_W2, w2), (ROUTE_R1, r1), (ROUTE_R2, r2)):
        route = jnp.where(lane == col, val, route)
    route_ref[...] = route
    cnt_ref[...] = jnp.broadcast_to(carry_ref[...], cnt_ref.shape)


def _router(h, g, r_hi_lo, *, n_experts):
    T, D = h.shape
    bm = _largest_divisor(T, (256, 128, 64, 32, 16))
    tri = (lax.broadcasted_iota(jnp.int32, (bm, bm), 0) > lax.broadcasted_iota(jnp.int32, (bm, bm), 1)).astype(BF16)
    body = functools.partial(_router_body, n_experts=n_experts)
    const = lambda shape: pl.BlockSpec(shape, lambda i: (0, 0))
    return pl.pallas_call(
        body,
        grid=(T // bm,),
        in_specs=[pl.BlockSpec((bm, D), lambda i: (i, 0)), const((1, D)), const((D, 2 * LANES)), const((bm, bm))],
        out_specs=[pl.BlockSpec((bm, LANES), lambda i: (i, 0)), const((8, LANES))],
        out_shape=[jax.ShapeDtypeStruct((T, LANES), F32), jax.ShapeDtypeStruct((8, LANES), F32)],
        scratch_shapes=[pltpu.VMEM((1, LANES), F32)],
        compiler_params=_params("arbitrary"),
        name="moe_router",
    )(h, g, r_hi_lo, tri)


ZERO_ROWS = 64
ROW_DMA_UNROLL = 8


def _dispatch_body(pos_ref, seg_ref, x_ref, g_ref, xs_hbm, xn_ref, zero_ref, sems, *, bm, T, n_experts, be, n_rows):
    i = pl.program_id(0)
    n_steps = pl.num_programs(0)
    slot = i % 2

    def zero_copy(first_row, c):
        row = pl.multiple_of(first_row + c * ZERO_ROWS, ZERO_ROWS)
        return pltpu.make_async_copy(zero_ref, xs_hbm.at[pl.ds(row, ZERO_ROWS), :], sems.at[2])

    def zero_block(first_row):
        for c in range(be // ZERO_ROWS):
            zero_copy(first_row, c).start()
        for c in range(be // ZERO_ROWS):
            zero_copy(first_row, c).wait()

    @pl.when(i == 0)
    def _():
        zero_ref[...] = jnp.zeros_like(zero_ref)
        for e in range(n_experts):
            @pl.when(seg_ref[n_experts + e] > 0)
            def _():
                zero_block(seg_ref[e] - be)

        def clear_unused(j, c):
            zero_block(j * be)
            return c

        lax.fori_loop(seg_ref[n_experts - 1] // be, n_rows // be, clear_unused, 0)

    def drain(s):
        for _ in range(TOP_K):
            pltpu.make_async_copy(xn_ref.at[s], xs_hbm.at[pl.ds(0, bm), :], sems.at[s]).wait()

    @pl.when(i >= 2)
    def _():
        drain(slot)

    xn_ref[slot] = _rmsnorm(x_ref[...], g_ref[...])

    def row_copy(r, k):
        dst = pos_ref[k * T + i * bm + r]
        return pltpu.make_async_copy(xn_ref.at[slot, pl.ds(r, 1), :], xs_hbm.at[pl.ds(dst, 1), :], sems.at[slot])

    def issue(grp, c):
        for u in range(ROW_DMA_UNROLL):
            row_copy(grp * ROW_DMA_UNROLL + u, 0).start(priority=u % 2)
            row_copy(grp * ROW_DMA_UNROLL + u, 1).start(priority=(u + 1) % 2)
        return c

    lax.fori_loop(0, bm // ROW_DMA_UNROLL, issue, 0)

    @pl.when(i == n_steps - 1)
    def _():
        drain(slot)

        @pl.when(i >= 1)
        def _():
            drain(1 - slot)


def _dispatch(pos, seg, h, g, *, n_rows, n_experts, be):
    T, D = h.shape
    bm = _largest_divisor(T, (256, 128, 64, 32, 16))
    body = functools.partial(_dispatch_body, bm=bm, T=T, n_experts=n_experts, be=be, n_rows=n_rows)
    return pl.pallas_call(
        body,
        grid_spec=pltpu.PrefetchScalarGridSpec(
            num_scalar_prefetch=2,
            grid=(T // bm,),
            in_specs=[pl.BlockSpec((bm, D), lambda i, pos, seg: (i, 0)),
                      pl.BlockSpec((1, D), lambda i, pos, seg: (0, 0))],
            out_specs=pl.BlockSpec(memory_space=pl.ANY),
            scratch_shapes=[pltpu.VMEM((2, bm, D), F32), pltpu.VMEM((ZERO_ROWS, D), F32),
                            pltpu.SemaphoreType.DMA((3,))],
        ),
        out_shape=jax.ShapeDtypeStruct((n_rows, D), F32),
        compiler_params=_params("arbitrary"),
        name="moe_dispatch",
    )(pos, seg, h, g)


def _expert_body(blk_ref, x_ref, wg_ref, wu_ref, wd_ref, o_ref, xb_ref, *, n_blocks):
    j = pl.program_id(0)

    @pl.when(pl.program_id(1) == 0)
    def _():
        o_ref[...] = jnp.zeros_like(o_ref)

    @pl.when(j < blk_ref[n_blocks])
    def _():
        @pl.when(pl.program_id(1) == 0)
        def _():
            xb_ref[...] = x_ref[...].astype(BF16)

        xb = xb_ref[...]
        hid = jax.nn.silu(_dot(xb, wg_ref[...])) * _dot(xb, wu_ref[...])
        o_ref[...] += _dot(hid.astype(BF16), wd_ref[...])


def _expert_ffn(blk, xs, wg, wu, wd, *, be):
    n_rows, D = xs.shape
    F = wg.shape[2]
    n_blocks = n_rows // be
    fc = _largest_divisor(F, (512, 256, 128))
    nc = F // fc
    body = functools.partial(_expert_body, n_blocks=n_blocks)

    def row_blk(j, blk):
        return jnp.minimum(j, blk[n_blocks] - 1)

    def chunk(j, c, blk):
        return jnp.where(j < blk[n_blocks], c, nc - 1)

    return pl.pallas_call(
        body,
        grid_spec=pltpu.PrefetchScalarGridSpec(
            num_scalar_prefetch=1,
            grid=(n_blocks, nc),
            in_specs=[
                pl.BlockSpec((be, D), lambda j, c, blk: (row_blk(j, blk), 0)),
                pl.BlockSpec((None, D, fc), lambda j, c, blk: (blk[row_blk(j, blk)], 0, chunk(j, c, blk))),
                pl.BlockSpec((None, D, fc), lambda j, c, blk: (blk[row_blk(j, blk)], 0, chunk(j, c, blk))),
                pl.BlockSpec((None, fc, D), lambda j, c, blk: (blk[row_blk(j, blk)], chunk(j, c, blk), 0)),
            ],
            out_specs=pl.BlockSpec((be, D), lambda j, c, blk: (j, 0)),
            scratch_shapes=[pltpu.VMEM((be, D), BF16)],
        ),
        out_shape=jax.ShapeDtypeStruct((n_rows, D), F32),
        compiler_params=_params("arbitrary", "arbitrary"),
        name="moe_experts",
    )(blk, xs, wg, wu, wd)


def _combine_body(pos_ref, h_ref, route_ref, g_ref, y_hbm, o_ref, y_ref, sems, *, bm, T):
    i = pl.program_id(0)
    n_steps = pl.num_programs(0)
    slot = i % 2

    def start_gathers(step, s):
        def row_copy(r, k):
            src = pos_ref[k * T + step * bm + r]
            return pltpu.make_async_copy(y_hbm.at[pl.ds(src, 1), :], y_ref.at[s, k, pl.ds(r, 1), :], sems.at[s])

        def issue(grp, c):
            for u in range(ROW_DMA_UNROLL):
                row_copy(grp * ROW_DMA_UNROLL + u, 0).start(priority=u % 2)
                row_copy(grp * ROW_DMA_UNROLL + u, 1).start(priority=(u + 1) % 2)
            return c

        lax.fori_loop(0, bm // ROW_DMA_UNROLL, issue, 0)

    @pl.when(i == 0)
    def _():
        start_gathers(i, slot)

    @pl.when(i + 1 < n_steps)
    def _():
        start_gathers(i + 1, 1 - slot)

    for k in range(TOP_K):
        pltpu.make_async_copy(y_hbm.at[pl.ds(0, bm), :], y_ref.at[slot, k], sems.at[slot]).wait()

    route = route_ref[...]
    w1 = route[:, ROUTE_W1:ROUTE_W1 + 1]
    w2 = route[:, ROUTE_W2:ROUTE_W2 + 1]
    h = h_ref[...] + (w1 * y_ref[slot, 0] + w2 * y_ref[slot, 1])
    o_ref[...] = _rmsnorm(h, g_ref[...])


def _combine(pos, h, route, g, y, *, B, L):
    T, D = h.shape
    S = L - N_META
    bm = _largest_divisor(L, tuple(b for b in range(512, 7, -8)))
    blocks_per_seq = L // bm
    assert pl.cdiv(S, bm) == blocks_per_seq
    body = functools.partial(_combine_body, bm=bm, T=T)
    return pl.pallas_call(
        body,
        grid_spec=pltpu.PrefetchScalarGridSpec(
            num_scalar_prefetch=1,
            grid=(T // bm,),
            in_specs=[pl.BlockSpec((bm, D), lambda i, pos: (i, 0)),
                      pl.BlockSpec((bm, LANES), lambda i, pos: (i, 0)),
                      pl.BlockSpec((1, D), lambda i, pos: (0, 0)),
                      pl.BlockSpec(memory_space=pl.ANY)],
            out_specs=pl.BlockSpec((None, bm, D), lambda i, pos: (i // blocks_per_seq, i % blocks_per_seq, 0)),
            scratch_shapes=[pltpu.VMEM((2, TOP_K, bm, D), F32), pltpu.SemaphoreType.DMA((2,))],
        ),
        out_shape=jax.ShapeDtypeStruct((B, S, D), F32),
        compiler_params=_params("arbitrary"),
        name="moe_combine",
    )(pos, h, route, g, y)


def _rope_tables(B, L):
    half = HEAD_DIM // 2
    inv = ROPE_THETA ** (-2.0 * jnp.arange(half, dtype=F32) / HEAD_DIM)
    pos = jnp.concatenate([jnp.arange(N_META, L), jnp.arange(N_META)]).astype(F32)
    ang = pos[:, None] * inv[None, :]
    cs = jnp.tile(jnp.stack([jnp.cos(ang), jnp.sin(ang)]), (1, 1, LANES // half))
    cs = jnp.stack([cs * (HEAD_DIM ** -0.5 * LOG2_E), cs])
    return jnp.broadcast_to(cs[:, :, None], (2, 2, B, L, LANES)).reshape(2, 2, B * L, LANES)


def _pair_rotary_halves(w_qk, n_heads):
    D = w_qk.shape[0]
    w = w_qk.reshape(D, 2, n_heads // 2, 2, 2, 2, HEAD_DIM // 2)
    return w.transpose(0, 1, 2, 5, 3, 4, 6).reshape(D, -1)


def _token_mixer(h, layer, B, L, w_in, b_gate, lam, subln_g, pool_w, pool_scale, w_ba, w_bp, w_out, norm_g, rope_cs):
    T, D = h.shape
    A, P = w_ba.shape[0], w_bp.shape[0]
    n_heads = A // (2 * HEAD_DIM)
    lambda_init = 0.8 - 0.6 * math.exp(-0.3 * layer)
    w_qk = _pair_rotary_halves(w_in[:, :2 * A], n_heads).astype(BF16)
    z = _inproj(h, norm_g, w_qk, w_in[:, 2 * A:].astype(BF16), rope_cs)
    z3 = z.reshape(B, L, z.shape[1])
    a = _attention(z3, lam, subln_g, n_heads=n_heads, lambda_init=lambda_init)
    p = _pool(z3, pool_w, pool_scale, col_block=(3 * A) // P)
    return _mix(a.reshape(T, A), p.reshape(T, P), z, b_gate, w_ba, w_bp, w_out, h,
                gate_col_block=(3 * A + P) // (2 * D))


def _moe_and_final_norm(h, norm_g, router, wg, wu, wd, final_g, *, B, L):
    T, D = h.shape
    E = router.shape[1]
    be = 512 if T >= 4096 else 64
    r_pad = jnp.zeros((D, LANES), F32).at[:, :E].set(router)
    r_hi = r_pad.astype(BF16)
    r_lo = (r_pad - r_hi.astype(F32)).astype(BF16)
    route, cnt = _router(h, norm_g, jnp.concatenate([r_hi, r_lo], axis=1), n_experts=E)

    counts = cnt[0, :E].astype(jnp.int32)
    n_blk = (counts + be - 1) // be
    seg_end = jnp.cumsum(n_blk * be)
    seg_start = seg_end - n_blk * be
    n_blocks = (TOP_K * T + E * (be - 1)) // be
    blk_end = jnp.cumsum(n_blk)
    blk_expert = jnp.minimum(jnp.sum(jnp.arange(n_blocks)[:, None] >= blk_end[None, :], axis=1), E - 1)
    blk = jnp.concatenate([blk_expert.astype(jnp.int32), blk_end[-1:].astype(jnp.int32)])
    seg = jnp.concatenate([seg_end, n_blk]).astype(jnp.int32)

    i1 = route[:, ROUTE_I1].astype(jnp.int32)
    i2 = route[:, ROUTE_I2].astype(jnp.int32)
    pos = jnp.concatenate([seg_start[i1] + route[:, ROUTE_R1].astype(jnp.int32),
                           seg_start[i2] + route[:, ROUTE_R2].astype(jnp.int32)])

    xs = _dispatch(pos, seg, h, norm_g, n_rows=n_blocks * be, n_experts=E, be=be)
    y = _expert_ffn(blk, xs, wg, wu, wd, be=be)
    return _combine(pos, h, route, final_g, y, B=B, L=L)


def kernel(x, meta_tokens, norm_mix, w_in, b_gate, lambda_q1, lambda_k1, lambda_q2, lambda_k2, subln, pool_w, pool_scale, w_branch_attn, w_branch_pool, w_out, norm_ffn, dense_w_gate, dense_w_up, dense_w_down, router, moe_w_gate, moe_w_up, moe_w_down, norm_final):
    B, S, D = x.shape
    depth = w_in.shape[0]
    assert depth == 2, "the final RMSNorm is fused into the routed layer, which must be the last one"
    L = N_META + S
    T = B * L
    meta = jnp.broadcast_to(meta_tokens[None].astype(x.dtype), (B, N_META, D))
    h = jnp.concatenate([x, meta], axis=1).reshape(T, D)
    rope = _rope_tables(B, L)
    row = lambda v: v.reshape(1, -1)

    for i in range(depth):
        lam = jnp.stack([lambda_q1[i], lambda_k1[i], lambda_q2[i], lambda_k2[i]])
        h = _token_mixer(h, i, B, L, w_in[i], row(b_gate[i]), lam, row(subln[i]),
                         pool_w[i].astype(BF16), row(pool_scale[i]), w_branch_attn[i].astype(BF16),
                         w_branch_pool[i].astype(BF16), w_out[i].astype(BF16), row(norm_mix[i]), rope)
        j = i // 2
        if i % 2 == 0:
            h = _dense_ffn(h, row(norm_ffn[i]), dense_w_gate[j].astype(BF16), dense_w_up[j].astype(BF16),
                           dense_w_down[j].astype(BF16))
        else:
            h = _moe_and_final_norm(h, row(norm_ffn[i]), router[j], moe_w_gate[j].astype(BF16),
                                    moe_w_up[j].astype(BF16), moe_w_down[j].astype(BF16), row(norm_final),
                                    B=B, L=L)
    return h
```

```python
import functools
import math

import jax
import jax.numpy as jnp
from jax import lax
from jax.experimental import pallas as pl
from jax.experimental.pallas import tpu as pltpu

N_META = 16
HEAD_DIM = 64
POOL_WINDOWS = (2, 4, 8, 16)
TOP_K = 2
ROPE_THETA = 10000.0
EPS = 1e-5

LANES = 128
SUBLANES_BF16 = 16
VMEM_CAP_BYTES = 64 * 1024 * 1024
VMEM_LIMIT_BYTES = VMEM_CAP_BYTES - 6 * 1024 * 1024

NEG_BIG = -1e30
LOG2_E = 1.4426950408889634
POOL_PAD = max(POOL_WINDOWS)
MXU_DEPTH = 256
POOL_BLOCK_ROWS = MXU_DEPTH - 2 * POOL_PAD

F32 = jnp.float32
BF16 = jnp.bfloat16


def _largest_divisor(n, candidates):
    for c in candidates:
        if c <= n and n % c == 0:
            return c
    raise ValueError(f"no block size in {candidates} divides {n}")


def _params(*semantics):
    return pltpu.CompilerParams(dimension_semantics=semantics, vmem_limit_bytes=VMEM_LIMIT_BYTES)


def _rmsnorm(x, g):
    ms = jnp.mean(x * x, axis=-1, keepdims=True)
    return x * lax.rsqrt(ms + EPS) * g


def _dot(a, b):
    return jnp.dot(a, b, preferred_element_type=F32)


def _inproj_body(x_ref, g_ref, wqk_ref, wr_ref, cs_ref, o_ref, xn_ref, *, n_qk):
    j = pl.program_id(1)

    @pl.when(j == 0)
    def _():
        xn_ref[...] = _rmsnorm(x_ref[...], g_ref[...]).astype(BF16)
        acc = _dot(xn_ref[...], wqk_ref[...])
        for c in range(0, n_qk, 2 * LANES):
            table = 0 if c < n_qk // 2 else 1
            cos, sin = cs_ref[table, 0], cs_ref[table, 1]
            t1 = acc[:, c:c + LANES]
            t2 = acc[:, c + LANES:c + 2 * LANES]
            o_ref[:, c:c + LANES] = (t1 * cos - t2 * sin).astype(o_ref.dtype)
            o_ref[:, c + LANES:c + 2 * LANES] = (t2 * cos + t1 * sin).astype(o_ref.dtype)

    @pl.when(j > 0)
    def _():
        o_ref[...] = _dot(xn_ref[...], wr_ref[...]).astype(o_ref.dtype)


def _inproj(h, g, w_qk, w_rest, rope_cs):
    T, D = h.shape
    n_qk, n_rest = w_qk.shape[1], w_rest.shape[1]
    assert n_rest % n_qk == 0
    bm = _largest_divisor(T, (768, 512, 384, 256, 128, 64, 32, 16))
    body = functools.partial(_inproj_body, n_qk=n_qk)
    return pl.pallas_call(
        body,
        grid=(T // bm, 1 + n_rest // n_qk),
        in_specs=[
            pl.BlockSpec((bm, D), lambda i, j: (i, 0)),
            pl.BlockSpec((1, D), lambda i, j: (0, 0)),
            pl.BlockSpec((D, n_qk), lambda i, j: (0, 0), pipeline_mode=pl.Buffered(1)),
            pl.BlockSpec((D, n_qk), lambda i, j: (0, jnp.maximum(j - 1, 0))),
            pl.BlockSpec((2, 2, bm, LANES), lambda i, j: (0, 0, i, 0)),
        ],
        out_specs=pl.BlockSpec((bm, n_qk), lambda i, j: (i, j)),
        out_shape=jax.ShapeDtypeStruct((T, n_qk + n_rest), BF16),
        scratch_shapes=[pltpu.VMEM((bm, D), BF16)],
        compiler_params=_params("parallel", "arbitrary"),
        name="inproj",
    )(h, g, w_qk, w_rest, rope_cs)


PAIR = 2 * LANES


def _attn_body(lam_ref, q_ref, k_ref, v_ref, sg_ref, o_ref, kp_ref, vp_ref, of_ref,
               *, L, l_main, l_keys, bq, lambda_init):
    lane = lax.broadcasted_iota(jnp.int32, (1, PAIR), 1)
    map_of_lane = (lane % HEAD_DIM) // (HEAD_DIM // 2)

    kp_ref[pl.ds(0, L), :] = k_ref[...]
    if l_keys > L:
        kp_ref[pl.ds(L, l_keys - L), :] = jnp.zeros((l_keys - L, PAIR), BF16)
    for hh in range(2):
        vp_ref[hh, pl.ds(0, L), pl.ds(0, LANES)] = v_ref[:, hh * LANES:(hh + 1) * LANES]
        vp_ref[hh, :, pl.ds(LANES, LANES)] = jnp.ones((l_keys, LANES), BF16)
        if l_keys > L:
            vp_ref[hh, pl.ds(L, l_keys - L), pl.ds(0, LANES)] = jnp.zeros((l_keys - L, LANES), BF16)

    lam = lam_ref[...]
    lam_full = (jnp.exp(jnp.sum(lam[0:1] * lam[1:2], axis=-1, keepdims=True))
                - jnp.exp(jnp.sum(lam[2:3] * lam[3:4], axis=-1, keepdims=True)) + lambda_init)

    tail_valid = (lax.broadcasted_iota(jnp.int32, (1, LANES), 1) + l_main) < L
    contract_last = (((1,), (1,)), ((), ()))
    zero = jnp.zeros((), BF16)
    q_all = q_ref[...]

    for hh in range(2):
        in_head = ((lane % LANES) // HEAD_DIM) == hh
        map_masks = (in_head & (map_of_lane == 0), in_head & (map_of_lane == 1))
        for i in range(L // bq):
            qb = q_all[i * bq:(i + 1) * bq]
            q_maps = jnp.concatenate([jnp.where(m, qb, zero) for m in map_masks], axis=0)
            s = lax.dot_general(q_maps, kp_ref[...], contract_last, preferred_element_type=F32)
            s_main = s[:, :l_main]
            m = jnp.max(s_main, axis=-1, keepdims=True)
            if l_keys > l_main:
                s_tail = jnp.where(tail_valid, s[:, l_main:], NEG_BIG)
                m = jnp.maximum(m, jnp.max(s_tail, axis=-1, keepdims=True))
            acc = _dot(jnp.exp2(s_main - m).astype(BF16), vp_ref[hh, pl.ds(0, l_main), :])
            if l_keys > l_main:
                acc = acc + _dot(jnp.exp2(s_tail - m).astype(BF16), vp_ref[hh, pl.ds(l_main, l_keys - l_main), :])
            a0, a1 = acc[:bq], acc[bq:]
            o = a0[:, :LANES] * (1.0 / a0[:, LANES:LANES + 1]) - a1[:, :LANES] * (lam_full / a1[:, LANES:LANES + 1])
            of_ref[pl.ds(i * bq, bq), pl.ds(hh * LANES, LANES)] = _rmsnorm(o, sg_ref[...]) * (1.0 - lambda_init)
    o_ref[...] = of_ref[...].astype(o_ref.dtype)


def _attention(z3, lam, subln_g, *, n_heads, lambda_init):
    B, L, _ = z3.shape
    l_main = (L // LANES) * LANES
    l_keys = l_main if l_main == L else l_main + LANES
    bq = _largest_divisor(L, tuple(b for b in range(400, 7, -8)))
    n_pairs = n_heads // 2
    body = functools.partial(_attn_body, L=L, l_main=l_main, l_keys=l_keys, bq=bq, lambda_init=lambda_init)
    pair = lambda off: pl.BlockSpec((None, L, PAIR), lambda b, h: (b, 0, off + h))
    const = lambda shape: pl.BlockSpec(shape, lambda b, h: (0,) * len(shape))
    return pl.pallas_call(
        body,
        grid=(B, n_pairs),
        in_specs=[const(lam.shape), pair(0), pair(n_pairs), pair(2 * n_pairs), const(subln_g.shape)],
        out_specs=pl.BlockSpec((None, L, PAIR), lambda b, h: (b, 0, h)),
        out_shape=jax.ShapeDtypeStruct((B, L, n_heads * LANES), BF16),
        scratch_shapes=[pltpu.VMEM((l_keys, PAIR), BF16), pltpu.VMEM((2, l_keys, PAIR), BF16),
                        pltpu.VMEM((L, PAIR), F32)],
        compiler_params=_params("parallel", "parallel"),
        name="diff_attention",
    )(lam, z3, z3, z3, subln_g)


def _pool_body(u_ref, band_ref, pw_ref, ps_ref, o_ref, pad_ref, *, L, cg, br):
    S = L - N_META
    P = pad_ref.shape[1]
    pad_ref[pl.ds(0, POOL_PAD), :] = jnp.zeros((POOL_PAD, P), pad_ref.dtype)
    pad_ref[pl.ds(POOL_PAD + L, POOL_PAD), :] = jnp.zeros((POOL_PAD, P), pad_ref.dtype)
    pad_ref[pl.ds(POOL_PAD, N_META), :] = u_ref[pl.ds(S, N_META), :]
    pad_ref[pl.ds(POOL_PAD + N_META, S), :] = u_ref[pl.ds(0, S), :]
    for start in range(0, L, br):
        rows = min(br, L - start)
        t = lax.broadcasted_iota(jnp.int32, (rows, 1), 0) + start
        for g, w in enumerate(POOL_WINDOWS):
            cols = slice(g * cg, (g + 1) * cg)
            win = _dot(band_ref[g, pl.ds(0, rows), pl.ds(0, rows + 2 * POOL_PAD)],
                       pad_ref[pl.ds(start, rows + 2 * POOL_PAD), cols])
            cnt = (jnp.minimum(t + w // 2, L) - jnp.maximum(t - w // 2, 0)).astype(F32)
            m = win * (1.0 / cnt) - pad_ref[pl.ds(start + POOL_PAD, rows), cols].astype(F32)
            y = (_dot(m.astype(BF16), pw_ref[g]) * ps_ref[:, cols]).astype(o_ref.dtype)
            if start == 0:
                o_ref[pl.ds(S, N_META), cols] = y[:N_META]
                o_ref[pl.ds(0, rows - N_META), cols] = y[N_META:]
            else:
                o_ref[pl.ds(start - N_META, rows), cols] = y


def _pool(z3, pool_w, pool_scale, *, col_block):
    B, L, _ = z3.shape
    G, cg, _ = pool_w.shape
    P = G * cg
    br = min(L, POOL_BLOCK_ROWS)
    assert L % SUBLANES_BF16 == 0 and N_META % SUBLANES_BF16 == 0
    r = lax.broadcasted_iota(jnp.int32, (br, br + 2 * POOL_PAD), 0)
    c = lax.broadcasted_iota(jnp.int32, (br, br + 2 * POOL_PAD), 1) - POOL_PAD
    band = jnp.stack([((c >= r - w // 2) & (c < r + w // 2)).astype(BF16) for w in POOL_WINDOWS])
    body = functools.partial(_pool_body, L=L, cg=cg, br=br)
    return pl.pallas_call(
        body,
        grid=(B,),
        in_specs=[
            pl.BlockSpec((None, L, P), lambda b: (b, 0, col_block)),
            pl.BlockSpec(band.shape, lambda b: (0, 0, 0)),
            pl.BlockSpec((G, cg, cg), lambda b: (0, 0, 0)),
            pl.BlockSpec((1, P), lambda b: (0, 0)),
        ],
        out_specs=pl.BlockSpec((None, L, P), lambda b: (b, 0, 0)),
        out_shape=jax.ShapeDtypeStruct((B, L, P), BF16),
        scratch_shapes=[pltpu.VMEM((L + 2 * POOL_PAD, P), BF16)],
        compiler_params=_params("parallel"),
        name="pool_mixer",
    )(z3, band, pool_w, pool_scale)


def _mix_body(a_ref, p_ref, gt_ref, b_ref, wa_ref, wp_ref, wo_ref, h_ref, o_ref, *, D):
    gates = jax.nn.sigmoid(gt_ref[...].astype(F32) + b_ref[...])
    y = gates[:, :D] * _dot(a_ref[...], wa_ref[...]) + gates[:, D:] * _dot(p_ref[...], wp_ref[...])
    o_ref[...] = h_ref[...] + _dot(y.astype(BF16), wo_ref[...])


def _mix(a, p, z, b_gate, wa, wp, wo, h, *, gate_col_block):
    T, D = h.shape
    A, P = a.shape[1], p.shape[1]
    bm = _largest_divisor(T, (384, 256, 128, 64, 32, 16))
    body = functools.partial(_mix_body, D=D)
    const = lambda shape: pl.BlockSpec(shape, lambda i: (0, 0), pipeline_mode=pl.Buffered(1))
    return pl.pallas_call(
        body,
        grid=(T // bm,),
        in_specs=[
            pl.BlockSpec((bm, A), lambda i: (i, 0)),
            pl.BlockSpec((bm, P), lambda i: (i, 0)),
            pl.BlockSpec((bm, 2 * D), lambda i: (i, gate_col_block)),
            const((1, 2 * D)), const((A, D)), const((P, D)), const((D, D)),
            pl.BlockSpec((bm, D), lambda i: (i, 0)),
        ],
        out_specs=pl.BlockSpec((bm, D), lambda i: (i, 0)),
        out_shape=jax.ShapeDtypeStruct((T, D), F32),
        compiler_params=_params("parallel"),
        name="mix_out",
    )(a, p, z, b_gate, wa, wp, wo, h)


def _ffn_body(x_ref, g_ref, wg_ref, wu_ref, wd_ref, o_ref, xn_ref):
    @pl.when(pl.program_id(1) == 0)
    def _():
        x = x_ref[...]
        xn_ref[...] = _rmsnorm(x, g_ref[...]).astype(BF16)
        o_ref[...] = x

    xn = xn_ref[...]
    hid = jax.nn.silu(_dot(xn, wg_ref[...])) * _dot(xn, wu_ref[...])
    o_ref[...] += _dot(hid.astype(BF16), wd_ref[...])


def _dense_ffn(h, g, wg, wu, wd):
    T, D = h.shape
    F = wg.shape[1]
    bm = _largest_divisor(T, (768, 512, 384, 256, 128, 64, 32, 16))
    fc = _largest_divisor(F, (512, 256, 128))
    return pl.pallas_call(
        _ffn_body,
        grid=(T // bm, F // fc),
        in_specs=[
            pl.BlockSpec((bm, D), lambda i, c: (i, 0)),
            pl.BlockSpec((1, D), lambda i, c: (0, 0)),
            pl.BlockSpec((D, fc), lambda i, c: (0, c)),
            pl.BlockSpec((D, fc), lambda i, c: (0, c)),
            pl.BlockSpec((fc, D), lambda i, c: (c, 0)),
        ],
        out_specs=pl.BlockSpec((bm, D), lambda i, c: (i, 0)),
        out_shape=jax.ShapeDtypeStruct((T, D), F32),
        scratch_shapes=[pltpu.VMEM((bm, D), BF16)],
        compiler_params=_params("parallel", "arbitrary"),
        name="dense_ffn",
    )(h, g, wg, wu, wd)


ROUTE_I1, ROUTE_I2, ROUTE_W1, ROUTE_W2, ROUTE_R1, ROUTE_R2 = range(6)


def _router_body(x_ref, g_ref, r_ref, tri_ref, route_ref, route_t_ref, cnt_ref, carry_ref, *, n_experts):
    @pl.when(pl.program_id(0) == 0)
    def _():
        carry_ref[...] = jnp.zeros_like(carry_ref)

    xn = _rmsnorm(x_ref[...], g_ref[...])
    x_hi = xn.astype(BF16)
    x_lo = (xn - x_hi.astype(F32)).astype(BF16)
    hi_both = _dot(x_hi, r_ref[...])
    logits = hi_both[:, :LANES] + (hi_both[:, LANES:] + _dot(x_lo, r_ref[:, pl.ds(0, LANES)]))

    lane = lax.broadcasted_iota(jnp.int32, (1, LANES), 1)
    lane_f = lane.astype(F32)
    lg = jnp.where(lane < n_experts, logits, NEG_BIG)
    v1 = jnp.max(lg, axis=-1, keepdims=True)
    i1 = jnp.min(jnp.where(lg == v1, lane_f, float(LANES)), axis=-1, keepdims=True)
    lg2 = jnp.where(lane_f == i1, NEG_BIG, lg)
    v2 = jnp.max(lg2, axis=-1, keepdims=True)
    i2 = jnp.min(jnp.where(lg2 == v2, lane_f, float(LANES)), axis=-1, keepdims=True)
    e2 = jnp.exp(v2 - v1)
    w1 = 1.0 / (1.0 + e2)
    w2 = e2 / (1.0 + e2)

    sel1 = lane_f == i1
    sel2 = lane_f == i2
    onehot = jnp.where(sel1 | sel2, 1.0, 0.0)
    rank = _dot(tri_ref[...], onehot.astype(BF16)) + carry_ref[...]
    r1 = jnp.sum(jnp.where(sel1, rank, 0.0), axis=-1, keepdims=True)
    r2 = jnp.sum(jnp.where(sel2, rank, 0.0), axis=-1, keepdims=True)
    carry_ref[...] += jnp.sum(onehot, axis=0, keepdims=True)

    route = jnp.zeros(route_ref.shape, F32)
    for col, val in ((ROUTE_I1, i1), (ROUTE_I2, i2), (ROUTE_W1, w1), (ROUTE_W2, w2), (ROUTE_R1, r1), (ROUTE_R2, r2)):
        route = jnp.where(lane == col, val, route)
    route_ref[...] = route
    route_t_ref[...] = jnp.transpose(route)[:route_t_ref.shape[0]]
    cnt_ref[...] = jnp.broadcast_to(carry_ref[...], cnt_ref.shape)


def _router(h, g, r_hi_lo, *, n_experts):
    T, D = h.shape
    bm = _largest_divisor(T, (256, 128, 64, 32, 16))
    tri = (lax.broadcasted_iota(jnp.int32, (bm, bm), 0) > lax.broadcasted_iota(jnp.int32, (bm, bm), 1)).astype(BF16)
    body = functools.partial(_router_body, n_experts=n_experts)
    const = lambda shape: pl.BlockSpec(shape, lambda i: (0, 0))
    return pl.pallas_call(
        body,
        grid=(T // bm,),
        in_specs=[pl.BlockSpec((bm, D), lambda i: (i, 0)), const((1, D)), const((D, 2 * LANES)), const((bm, bm))],
        out_specs=[pl.BlockSpec((bm, LANES), lambda i: (i, 0)), pl.BlockSpec((8, bm), lambda i: (0, i)),
                   const((8, LANES))],
        out_shape=[jax.ShapeDtypeStruct((T, LANES), F32), jax.ShapeDtypeStruct((8, T), F32),
                   jax.ShapeDtypeStruct((8, LANES), F32)],
        scratch_shapes=[pltpu.VMEM((1, LANES), F32)],
        compiler_params=_params("arbitrary"),
        name="moe_router",
    )(h, g, r_hi_lo, tri)


ZERO_ROWS = 64
ROW_DMA_UNROLL = 8


def _dispatch_body(pos_ref, seg_ref, x_ref, g_ref, xs_hbm, xn_ref, zero_ref, sems, *, bm, T, n_experts, be, n_rows):
    i = pl.program_id(0)
    n_steps = pl.num_programs(0)
    slot = i % 2

    def zero_copy(first_row, c):
        row = pl.multiple_of(first_row + c * ZERO_ROWS, ZERO_ROWS)
        return pltpu.make_async_copy(zero_ref, xs_hbm.at[pl.ds(row, ZERO_ROWS), :], sems.at[2])

    def zero_block(first_row):
        for c in range(be // ZERO_ROWS):
            zero_copy(first_row, c).start()
        for c in range(be // ZERO_ROWS):
            zero_copy(first_row, c).wait()

    @pl.when(i == 0)
    def _():
        zero_ref[...] = jnp.zeros_like(zero_ref)
        for e in range(n_experts):
            @pl.when(seg_ref[n_experts + e] > 0)
            def _():
                zero_block(seg_ref[e] - be)

        def clear_unused(j, c):
            zero_block(j * be)
            return c

        lax.fori_loop(seg_ref[n_experts - 1] // be, n_rows // be, clear_unused, 0)

    def drain(s):
        for _ in range(TOP_K):
            pltpu.make_async_copy(xn_ref.at[s], xs_hbm.at[pl.ds(0, bm), :], sems.at[s]).wait()

    @pl.when(i >= 2)
    def _():
        drain(slot)

    xn_ref[slot] = _rmsnorm(x_ref[...], g_ref[...])

    def row_copy(r, k):
        dst = pos_ref[k * T + i * bm + r]
        return pltpu.make_async_copy(xn_ref.at[slot, pl.ds(r, 1), :], xs_hbm.at[pl.ds(dst, 1), :], sems.at[slot])

    def issue(grp, c):
        for u in range(ROW_DMA_UNROLL):
            row_copy(grp * ROW_DMA_UNROLL + u, 0).start(priority=u % 2)
            row_copy(grp * ROW_DMA_UNROLL + u, 1).start(priority=(u + 1) % 2)
        return c

    lax.fori_loop(0, bm // ROW_DMA_UNROLL, issue, 0)

    @pl.when(i == n_steps - 1)
    def _():
        drain(slot)

        @pl.when(i >= 1)
        def _():
            drain(1 - slot)


def _dispatch(pos, seg, h, g, *, n_rows, n_experts, be):
    T, D = h.shape
    bm = _largest_divisor(T, (256, 128, 64, 32, 16))
    body = functools.partial(_dispatch_body, bm=bm, T=T, n_experts=n_experts, be=be, n_rows=n_rows)
    return pl.pallas_call(
        body,
        grid_spec=pltpu.PrefetchScalarGridSpec(
            num_scalar_prefetch=2,
            grid=(T // bm,),
            in_specs=[pl.BlockSpec((bm, D), lambda i, pos, seg: (i, 0)),
                      pl.BlockSpec((1, D), lambda i, pos, seg: (0, 0))],
            out_specs=pl.BlockSpec(memory_space=pl.ANY),
            scratch_shapes=[pltpu.VMEM((2, bm, D), F32), pltpu.VMEM((ZERO_ROWS, D), F32),
                            pltpu.SemaphoreType.DMA((3,))],
        ),
        out_shape=jax.ShapeDtypeStruct((n_rows, D), F32),
        compiler_params=_params("arbitrary"),
        name="moe_dispatch",
    )(pos, seg, h, g)


def _expert_body(blk_ref, x_ref, wg_ref, wu_ref, wd_ref, o_ref, xb_ref, *, n_blocks):
    j = pl.program_id(0)

    @pl.when(pl.program_id(1) == 0)
    def _():
        o_ref[...] = jnp.zeros_like(o_ref)

    @pl.when(j < blk_ref[n_blocks])
    def _():
        @pl.when(pl.program_id(1) == 0)
        def _():
            xb_ref[...] = x_ref[...].astype(BF16)

        xb = xb_ref[...]
        hid = jax.nn.silu(_dot(xb, wg_ref[...])) * _dot(xb, wu_ref[...])
        o_ref[...] += _dot(hid.astype(BF16), wd_ref[...])


def _expert_ffn(blk, xs, wg, wu, wd, *, be):
    n_rows, D = xs.shape
    F = wg.shape[2]
    n_blocks = n_rows // be
    fc = _largest_divisor(F, (512, 256, 128))
    nc = F // fc
    body = functools.partial(_expert_body, n_blocks=n_blocks)

    def row_blk(j, blk):
        return jnp.minimum(j, blk[n_blocks] - 1)

    def chunk(j, c, blk):
        return jnp.where(j < blk[n_blocks], c, nc - 1)

    return pl.pallas_call(
        body,
        grid_spec=pltpu.PrefetchScalarGridSpec(
            num_scalar_prefetch=1,
            grid=(n_blocks, nc),
            in_specs=[
                pl.BlockSpec((be, D), lambda j, c, blk: (row_blk(j, blk), 0)),
                pl.BlockSpec((None, D, fc), lambda j, c, blk: (blk[row_blk(j, blk)], 0, chunk(j, c, blk))),
                pl.BlockSpec((None, D, fc), lambda j, c, blk: (blk[row_blk(j, blk)], 0, chunk(j, c, blk))),
                pl.BlockSpec((None, fc, D), lambda j, c, blk: (blk[row_blk(j, blk)], chunk(j, c, blk), 0)),
            ],
            out_specs=pl.BlockSpec((be, D), lambda j, c, blk: (j, 0)),
            scratch_shapes=[pltpu.VMEM((be, D), BF16)],
        ),
        out_shape=jax.ShapeDtypeStruct((n_rows, D), F32),
        compiler_params=_params("arbitrary", "arbitrary"),
        name="moe_experts",
    )(blk, xs, wg, wu, wd)


def _combine_body(pos_ref, h_ref, route_ref, g_ref, y_hbm, o_ref, y_ref, sems, *, bm, T):
    i = pl.program_id(0)
    n_steps = pl.num_programs(0)
    slot = i % 2

    def start_gathers(step, s):
        def row_copy(r, k):
            src = pos_ref[k * T + step * bm + r]
            return pltpu.make_async_copy(y_hbm.at[pl.ds(src, 1), :], y_ref.at[s, k, pl.ds(r, 1), :], sems.at[s])

        def issue(grp, c):
            for u in range(ROW_DMA_UNROLL):
                row_copy(grp * ROW_DMA_UNROLL + u, 0).start(priority=u % 2)
                row_copy(grp * ROW_DMA_UNROLL + u, 1).start(priority=(u + 1) % 2)
            return c

        lax.fori_loop(0, bm // ROW_DMA_UNROLL, issue, 0)

    @pl.when(i == 0)
    def _():
        start_gathers(i, slot)

    @pl.when(i + 1 < n_steps)
    def _():
        start_gathers(i + 1, 1 - slot)

    for k in range(TOP_K):
        pltpu.make_async_copy(y_hbm.at[pl.ds(0, bm), :], y_ref.at[slot, k], sems.at[slot]).wait()

    route = route_ref[...]
    w1 = route[:, ROUTE_W1:ROUTE_W1 + 1]
    w2 = route[:, ROUTE_W2:ROUTE_W2 + 1]
    h = h_ref[...] + (w1 * y_ref[slot, 0] + w2 * y_ref[slot, 1])
    o_ref[...] = _rmsnorm(h, g_ref[...])


def _combine(pos, h, route, g, y, *, B, L):
    T, D = h.shape
    S = L - N_META
    bm = _largest_divisor(L, tuple(b for b in range(512, 7, -8)))
    blocks_per_seq = L // bm
    assert pl.cdiv(S, bm) == blocks_per_seq
    body = functools.partial(_combine_body, bm=bm, T=T)
    return pl.pallas_call(
        body,
        grid_spec=pltpu.PrefetchScalarGridSpec(
            num_scalar_prefetch=1,
            grid=(T // bm,),
            in_specs=[pl.BlockSpec((bm, D), lambda i, pos: (i, 0)),
                      pl.BlockSpec((bm, LANES), lambda i, pos: (i, 0)),
                      pl.BlockSpec((1, D), lambda i, pos: (0, 0)),
                      pl.BlockSpec(memory_space=pl.ANY)],
            out_specs=pl.BlockSpec((None, bm, D), lambda i, pos: (i // blocks_per_seq, i % blocks_per_seq, 0)),
            scratch_shapes=[pltpu.VMEM((2, TOP_K, bm, D), F32), pltpu.SemaphoreType.DMA((2,))],
        ),
        out_shape=jax.ShapeDtypeStruct((B, S, D), F32),
        compiler_params=_params("arbitrary"),
        name="moe_combine",
    )(pos, h, route, g, y)


def _rope_tables(B, L):
    half = HEAD_DIM // 2
    inv = ROPE_THETA ** (-2.0 * jnp.arange(half, dtype=F32) / HEAD_DIM)
    pos = jnp.concatenate([jnp.arange(N_META, L), jnp.arange(N_META)]).astype(F32)
    ang = pos[:, None] * inv[None, :]
    cs = jnp.tile(jnp.stack([jnp.cos(ang), jnp.sin(ang)]), (1, 1, LANES // half))
    cs = jnp.stack([cs * (HEAD_DIM ** -0.5 * LOG2_E), cs])
    return jnp.broadcast_to(cs[:, :, None], (2, 2, B, L, LANES)).reshape(2, 2, B * L, LANES)


def _pair_rotary_halves(w_qk, n_heads):
    D = w_qk.shape[0]
    w = w_qk.reshape(D, 2, n_heads // 2, 2, 2, 2, HEAD_DIM // 2)
    return w.transpose(0, 1, 2, 5, 3, 4, 6).reshape(D, -1)


def _token_mixer(h, layer, B, L, w_in, b_gate, lam, subln_g, pool_w, pool_scale, w_ba, w_bp, w_out, norm_g, rope_cs):
    T, D = h.shape
    A, P = w_ba.shape[0], w_bp.shape[0]
    n_heads = A // (2 * HEAD_DIM)
    lambda_init = 0.8 - 0.6 * math.exp(-0.3 * layer)
    w_qk = _pair_rotary_halves(w_in[:, :2 * A], n_heads).astype(BF16)
    z = _inproj(h, norm_g, w_qk, w_in[:, 2 * A:].astype(BF16), rope_cs)
    z3 = z.reshape(B, L, z.shape[1])
    a = _attention(z3, lam, subln_g, n_heads=n_heads, lambda_init=lambda_init)
    p = _pool(z3, pool_w, pool_scale, col_block=(3 * A) // P)
    return _mix(a.reshape(T, A), p.reshape(T, P), z, b_gate, w_ba, w_bp, w_out, h,
                gate_col_block=(3 * A + P) // (2 * D))


def _moe_and_final_norm(h, norm_g, router, wg, wu, wd, final_g, *, B, L):
    T, D = h.shape
    E = router.shape[1]
    be = 1024 if T >= 8192 else 64
    r_pad = jnp.zeros((D, LANES), F32).at[:, :E].set(router)
    r_hi = r_pad.astype(BF16)
    r_lo = (r_pad - r_hi.astype(F32)).astype(BF16)
    route, route_t, cnt = _router(h, norm_g, jnp.concatenate([r_hi, r_lo], axis=1), n_experts=E)

    counts = cnt[0, :E].astype(jnp.int32)
    n_blk = (counts + be - 1) // be
    seg_end = jnp.cumsum(n_blk * be)
    seg_start = seg_end - n_blk * be
    n_blocks = (TOP_K * T + E * (be - 1)) // be
    blk_end = jnp.cumsum(n_blk)
    blk_expert = jnp.minimum(jnp.sum(jnp.arange(n_blocks)[:, None] >= blk_end[None, :], axis=1), E - 1)
    blk = jnp.concatenate([blk_expert.astype(jnp.int32), blk_end[-1:].astype(jnp.int32)])
    seg = jnp.concatenate([seg_end, n_blk]).astype(jnp.int32)

    def slot_of(expert_f, rank_f):
        start = jnp.sum(jnp.where(expert_f[:, None] == jnp.arange(E, dtype=F32)[None, :], seg_start[None, :], 0), axis=1)
        return start + rank_f.astype(jnp.int32)

    pos = jnp.concatenate([slot_of(route_t[ROUTE_I1], route_t[ROUTE_R1]), slot_of(route_t[ROUTE_I2], route_t[ROUTE_R2])])

    xs = _dispatch(pos, seg, h, norm_g, n_rows=n_blocks * be, n_experts=E, be=be)
    y = _expert_ffn(blk, xs, wg, wu, wd, be=be)
    return _combine(pos, h, route, final_g, y, B=B, L=L)


def kernel(x, meta_tokens, norm_mix, w_in, b_gate, lambda_q1, lambda_k1, lambda_q2, lambda_k2, subln, pool_w, pool_scale, w_branch_attn, w_branch_pool, w_out, norm_ffn, dense_w_gate, dense_w_up, dense_w_down, router, moe_w_gate, moe_w_up, moe_w_down, norm_final):
    B, S, D = x.shape
    depth = w_in.shape[0]
    assert depth == 2, "the final RMSNorm is fused into the routed layer, which must be the last one"
    L = N_META + S
    T = B * L
    meta = jnp.broadcast_to(meta_tokens[None].astype(x.dtype), (B, N_META, D))
    h = jnp.concatenate([x, meta], axis=1).reshape(T, D)
    rope = _rope_tables(B, L)
    row = lambda v: v.reshape(1, -1)

    for i in range(depth):
        lam = jnp.stack([lambda_q1[i], lambda_k1[i], lambda_q2[i], lambda_k2[i]])
        h = _token_mixer(h, i, B, L, w_in[i], row(b_gate[i]), lam, row(subln[i]),
                         pool_w[i].astype(BF16), row(pool_scale[i]), w_branch_attn[i].astype(BF16),
                         w_branch_pool[i].astype(BF16), w_out[i].astype(BF16), row(norm_mix[i]), rope)
        j = i // 2
        if i % 2 == 0:
            h = _dense_ffn(h, row(norm_ffn[i]), dense_w_gate[j].astype(BF16), dense_w_up[j].astype(BF16),
                           dense_w_down[j].astype(BF16))
        else:
            h = _moe_and_final_norm(h, row(norm_ffn[i]), router[j], moe_w_gate[j].astype(BF16),
                                    moe_w_up[j].astype(BF16), moe_w_down[j].astype(BF16), row(norm_final),
                                    B=B, L=L)
    return h
```

```python
import functools
import math

import jax
import jax.numpy as jnp
from jax import lax
from jax.experimental import pallas as pl
from jax.experimental.pallas import tpu as pltpu

N_META = 16
HEAD_DIM = 64
POOL_WINDOWS = (2, 4, 8, 16)
TOP_K = 2
ROPE_THETA = 10000.0
EPS = 1e-5

LANES = 128
SUBLANES_BF16 = 16
VMEM_CAP_BYTES = 64 * 1024 * 1024
VMEM_LIMIT_BYTES = VMEM_CAP_BYTES - 6 * 1024 * 1024

NEG_BIG = -1e30
LOG2_E = 1.4426950408889634
POOL_PAD = max(POOL_WINDOWS)
MXU_DEPTH = 256
POOL_BLOCK_ROWS = MXU_DEPTH - 2 * POOL_PAD

F32 = jnp.float32
BF16 = jnp.bfloat16


def _largest_divisor(n, candidates):
    for c in candidates:
        if c <= n and n % c == 0:
            return c
    raise ValueError(f"no block size in {candidates} divides {n}")


def _params(*semantics):
    return pltpu.CompilerParams(dimension_semantics=semantics, vmem_limit_bytes=VMEM_LIMIT_BYTES)


def _rmsnorm(x, g):
    ms = jnp.mean(x * x, axis=-1, keepdims=True)
    return x * lax.rsqrt(ms + EPS) * g


def _dot(a, b):
    return jnp.dot(a, b, preferred_element_type=F32)


def _inproj_body(x_ref, g_ref, wqk_ref, wr_ref, cs_ref, o_ref, xn_ref, *, n_qk):
    j = pl.program_id(1)

    @pl.when(j == 0)
    def _():
        xn_ref[...] = _rmsnorm(x_ref[...], g_ref[...]).astype(BF16)
        acc = _dot(xn_ref[...], wqk_ref[...])
        for c in range(0, n_qk, 2 * LANES):
            table = 0 if c < n_qk // 2 else 1
            cos, sin = cs_ref[table, 0], cs_ref[table, 1]
            t1 = acc[:, c:c + LANES]
            t2 = acc[:, c + LANES:c + 2 * LANES]
            o_ref[:, c:c + LANES] = (t1 * cos - t2 * sin).astype(o_ref.dtype)
            o_ref[:, c + LANES:c + 2 * LANES] = (t2 * cos + t1 * sin).astype(o_ref.dtype)

    @pl.when(j > 0)
    def _():
        o_ref[...] = _dot(xn_ref[...], wr_ref[...]).astype(o_ref.dtype)


def _inproj(h, g, w_qk, w_all, rope_cs):
    T, D = h.shape
    n_qk = w_qk.shape[1]
    n_rest = w_all.shape[1] - n_qk
    assert n_rest % n_qk == 0
    bm = _largest_divisor(T, (768, 512, 384, 256, 128, 64, 32, 16))
    body = functools.partial(_inproj_body, n_qk=n_qk)
    return pl.pallas_call(
        body,
        grid=(T // bm, 1 + n_rest // n_qk),
        in_specs=[
            pl.BlockSpec((bm, D), lambda i, j: (i, 0)),
            pl.BlockSpec((1, D), lambda i, j: (0, 0)),
            pl.BlockSpec((D, n_qk), lambda i, j: (0, 0), pipeline_mode=pl.Buffered(1)),
            pl.BlockSpec((D, n_qk), lambda i, j: (0, jnp.maximum(j, 1))),
            pl.BlockSpec((2, 2, bm, LANES), lambda i, j: (0, 0, i, 0)),
        ],
        out_specs=pl.BlockSpec((bm, n_qk), lambda i, j: (i, j)),
        out_shape=jax.ShapeDtypeStruct((T, n_qk + n_rest), BF16),
        scratch_shapes=[pltpu.VMEM((bm, D), BF16)],
        compiler_params=_params("parallel", "arbitrary"),
        name="inproj",
    )(h, g, w_qk, w_all, rope_cs)


PAIR = 2 * LANES


def _attn_body(lam_ref, q_ref, k_ref, v_ref, sg_ref, o_ref, kp_ref, vp_ref, of_ref,
               *, L, l_main, l_keys, bq, lambda_init):
    lane = lax.broadcasted_iota(jnp.int32, (1, PAIR), 1)
    map_of_lane = (lane % HEAD_DIM) // (HEAD_DIM // 2)

    kp_ref[pl.ds(0, L), :] = k_ref[...]
    if l_keys > L:
        kp_ref[pl.ds(L, l_keys - L), :] = jnp.zeros((l_keys - L, PAIR), BF16)
    for hh in range(2):
        vp_ref[hh, pl.ds(0, L), pl.ds(0, LANES)] = v_ref[:, hh * LANES:(hh + 1) * LANES]
        vp_ref[hh, :, pl.ds(LANES, LANES)] = jnp.ones((l_keys, LANES), BF16)
        if l_keys > L:
            vp_ref[hh, pl.ds(L, l_keys - L), pl.ds(0, LANES)] = jnp.zeros((l_keys - L, LANES), BF16)

    lam = lam_ref[...]
    lam_full = (jnp.exp(jnp.sum(lam[0:1] * lam[1:2], axis=-1, keepdims=True))
                - jnp.exp(jnp.sum(lam[2:3] * lam[3:4], axis=-1, keepdims=True)) + lambda_init)

    tail_valid = (lax.broadcasted_iota(jnp.int32, (1, LANES), 1) + l_main) < L
    contract_last = (((1,), (1,)), ((), ()))
    zero = jnp.zeros((), BF16)
    q_all = q_ref[...]

    for hh in range(2):
        in_head = ((lane % LANES) // HEAD_DIM) == hh
        map_masks = (in_head & (map_of_lane == 0), in_head & (map_of_lane == 1))
        for i in range(L // bq):
            qb = q_all[i * bq:(i + 1) * bq]
            q_maps = jnp.concatenate([jnp.where(m, qb, zero) for m in map_masks], axis=0)
            s = lax.dot_general(q_maps, kp_ref[...], contract_last, preferred_element_type=F32)
            s_main = s[:, :l_main]
            m = jnp.max(s_main, axis=-1, keepdims=True)
            if l_keys > l_main:
                s_tail = jnp.where(tail_valid, s[:, l_main:], NEG_BIG)
                m = jnp.maximum(m, jnp.max(s_tail, axis=-1, keepdims=True))
            acc = _dot(jnp.exp2(s_main - m).astype(BF16), vp_ref[hh, pl.ds(0, l_main), :])
            if l_keys > l_main:
                acc = acc + _dot(jnp.exp2(s_tail - m).astype(BF16), vp_ref[hh, pl.ds(l_main, l_keys - l_main), :])
            a0, a1 = acc[:bq], acc[bq:]
            o = a0[:, :LANES] * (1.0 / a0[:, LANES:LANES + 1]) - a1[:, :LANES] * (lam_full / a1[:, LANES:LANES + 1])
            of_ref[pl.ds(i * bq, bq), pl.ds(hh * LANES, LANES)] = _rmsnorm(o, sg_ref[...]) * (1.0 - lambda_init)
    o_ref[...] = of_ref[...].astype(o_ref.dtype)


def _attention(z3, lam, subln_g, *, n_heads, lambda_init):
    B, L, _ = z3.shape
    l_main = (L // LANES) * LANES
    l_keys = l_main if l_main == L else l_main + LANES
    bq = _largest_divisor(L, tuple(b for b in range(400, 7, -8)))
    n_pairs = n_heads // 2
    body = functools.partial(_attn_body, L=L, l_main=l_main, l_keys=l_keys, bq=bq, lambda_init=lambda_init)
    pair = lambda off: pl.BlockSpec((None, L, PAIR), lambda b, h: (b, 0, off + h))
    const = lambda shape: pl.BlockSpec(shape, lambda b, h: (0,) * len(shape))
    return pl.pallas_call(
        body,
        grid=(B, n_pairs),
        in_specs=[const(lam.shape), pair(0), pair(n_pairs), pair(2 * n_pairs), const(subln_g.shape)],
        out_specs=pl.BlockSpec((None, L, PAIR), lambda b, h: (b, 0, h)),
        out_shape=jax.ShapeDtypeStruct((B, L, n_heads * LANES), BF16),
        scratch_shapes=[pltpu.VMEM((l_keys, PAIR), BF16), pltpu.VMEM((2, l_keys, PAIR), BF16),
                        pltpu.VMEM((L, PAIR), F32)],
        compiler_params=_params("parallel", "parallel"),
        name="diff_attention",
    )(lam, z3, z3, z3, subln_g)


def _pool_body(u_ref, band_ref, pw_ref, ps_ref, o_ref, pad_ref, *, L, cg, br):
    S = L - N_META
    P = pad_ref.shape[1]
    pad_ref[pl.ds(0, POOL_PAD), :] = jnp.zeros((POOL_PAD, P), pad_ref.dtype)
    pad_ref[pl.ds(POOL_PAD + L, POOL_PAD), :] = jnp.zeros((POOL_PAD, P), pad_ref.dtype)
    pad_ref[pl.ds(POOL_PAD, N_META), :] = u_ref[pl.ds(S, N_META), :]
    pad_ref[pl.ds(POOL_PAD + N_META, S), :] = u_ref[pl.ds(0, S), :]
    for start in range(0, L, br):
        rows = min(br, L - start)
        t = lax.broadcasted_iota(jnp.int32, (rows, 1), 0) + start
        for g, w in enumerate(POOL_WINDOWS):
            cols = slice(g * cg, (g + 1) * cg)
            win = _dot(band_ref[g, pl.ds(0, rows), pl.ds(0, rows + 2 * POOL_PAD)],
                       pad_ref[pl.ds(start, rows + 2 * POOL_PAD), cols])
            cnt = (jnp.minimum(t + w // 2, L) - jnp.maximum(t - w // 2, 0)).astype(F32)
            m = win * (1.0 / cnt) - pad_ref[pl.ds(start + POOL_PAD, rows), cols].astype(F32)
            y = (_dot(m.astype(BF16), pw_ref[g]) * ps_ref[:, cols]).astype(o_ref.dtype)
            if start == 0:
                o_ref[pl.ds(S, N_META), cols] = y[:N_META]
                o_ref[pl.ds(0, rows - N_META), cols] = y[N_META:]
            else:
                o_ref[pl.ds(start - N_META, rows), cols] = y


def _pool(z3, pool_w, pool_scale, *, col_block):
    B, L, _ = z3.shape
    G, cg, _ = pool_w.shape
    P = G * cg
    br = min(L, POOL_BLOCK_ROWS)
    assert L % SUBLANES_BF16 == 0 and N_META % SUBLANES_BF16 == 0
    r = lax.broadcasted_iota(jnp.int32, (br, br + 2 * POOL_PAD), 0)
    c = lax.broadcasted_iota(jnp.int32, (br, br + 2 * POOL_PAD), 1) - POOL_PAD
    band = jnp.stack([((c >= r - w // 2) & (c < r + w // 2)).astype(BF16) for w in POOL_WINDOWS])
    body = functools.partial(_pool_body, L=L, cg=cg, br=br)
    return pl.pallas_call(
        body,
        grid=(B,),
        in_specs=[
            pl.BlockSpec((None, L, P), lambda b: (b, 0, col_block)),
            pl.BlockSpec(band.shape, lambda b: (0, 0, 0)),
            pl.BlockSpec((G, cg, cg), lambda b: (0, 0, 0)),
            pl.BlockSpec((1, P), lambda b: (0, 0)),
        ],
        out_specs=pl.BlockSpec((None, L, P), lambda b: (b, 0, 0)),
        out_shape=jax.ShapeDtypeStruct((B, L, P), BF16),
        scratch_shapes=[pltpu.VMEM((L + 2 * POOL_PAD, P), BF16)],
        compiler_params=_params("parallel"),
        name="pool_mixer",
    )(z3, band, pool_w, pool_scale)


def _mix_rows(a_ref, p_ref, gt_ref, b_ref, wa_ref, wp_ref, wo_ref, h_ref, D):
    gates = jax.nn.sigmoid(gt_ref[...].astype(F32) + b_ref[...])
    y = gates[:, :D] * _dot(a_ref[...], wa_ref[...]) + gates[:, D:] * _dot(p_ref[...], wp_ref[...])
    return h_ref[...] + _dot(y.astype(BF16), wo_ref[...])


def _mix_body(a_ref, p_ref, gt_ref, b_ref, wa_ref, wp_ref, wo_ref, h_ref, o_ref, *, D):
    o_ref[...] = _mix_rows(a_ref, p_ref, gt_ref, b_ref, wa_ref, wp_ref, wo_ref, h_ref, D)


def _mix_route_body(a_ref, p_ref, gt_ref, b_ref, wa_ref, wp_ref, wo_ref, h_ref, g_ref, r_ref, tri_ref,
                    o_ref, route_ref, route_t_ref, cnt_ref, carry_ref, *, D, n_experts):
    h = _mix_rows(a_ref, p_ref, gt_ref, b_ref, wa_ref, wp_ref, wo_ref, h_ref, D)
    o_ref[...] = h
    _route_rows(h, g_ref, r_ref, tri_ref, route_ref, route_t_ref, cnt_ref, carry_ref, n_experts)


def _mix(a, p, z, b_gate, wa, wp, wo, h, *, gate_col_block, routing=None):
    T, D = h.shape
    A, P = a.shape[1], p.shape[1]
    bm = _largest_divisor(T, (384, 256, 128, 64, 32, 16))
    const = lambda shape: pl.BlockSpec(shape, lambda i: (0, 0), pipeline_mode=pl.Buffered(1))
    rows = lambda width: pl.BlockSpec((bm, width), lambda i: (i, 0))
    in_specs = [rows(A), rows(P), pl.BlockSpec((bm, 2 * D), lambda i: (i, gate_col_block)),
                const((1, 2 * D)), const((A, D)), const((P, D)), const((D, D)), rows(D)]
    if routing is None:
        return pl.pallas_call(
            functools.partial(_mix_body, D=D),
            grid=(T // bm,),
            in_specs=in_specs,
            out_specs=rows(D),
            out_shape=jax.ShapeDtypeStruct((T, D), F32),
            compiler_params=_params("parallel"),
            name="mix_out",
        )(a, p, z, b_gate, wa, wp, wo, h)
    g_ffn, r_hi_lo, n_experts = routing
    tri = (lax.broadcasted_iota(jnp.int32, (bm, bm), 0) > lax.broadcasted_iota(jnp.int32, (bm, bm), 1)).astype(BF16)
    return pl.pallas_call(
        functools.partial(_mix_route_body, D=D, n_experts=n_experts),
        grid=(T // bm,),
        in_specs=in_specs + [const((1, D)), const((D, 2 * LANES)), const((bm, bm))],
        out_specs=[rows(D), rows(LANES), pl.BlockSpec((8, bm), lambda i: (0, i)),
                   pl.BlockSpec((8, LANES), lambda i: (0, 0))],
        out_shape=[jax.ShapeDtypeStruct((T, D), F32), jax.ShapeDtypeStruct((T, LANES), F32),
                   jax.ShapeDtypeStruct((8, T), F32), jax.ShapeDtypeStruct((8, LANES), F32)],
        scratch_shapes=[pltpu.VMEM((1, LANES), F32)],
        compiler_params=_params("arbitrary"),
        name="mix_out_route",
    )(a, p, z, b_gate, wa, wp, wo, h, g_ffn, r_hi_lo, tri)


def _ffn_body(x_ref, g_ref, wg_ref, wu_ref, wd_ref, o_ref, xn_ref):
    @pl.when(pl.program_id(1) == 0)
    def _():
        x = x_ref[...]
        xn_ref[...] = _rmsnorm(x, g_ref[...]).astype(BF16)
        o_ref[...] = x

    xn = xn_ref[...]
    hid = jax.nn.silu(_dot(xn, wg_ref[...])) * _dot(xn, wu_ref[...])
    o_ref[...] += _dot(hid.astype(BF16), wd_ref[...])


def _dense_ffn(h, g, wg, wu, wd):
    T, D = h.shape
    F = wg.shape[1]
    bm = _largest_divisor(T, (768, 512, 384, 256, 128, 64, 32, 16))
    fc = _largest_divisor(F, (512, 256, 128))
    return pl.pallas_call(
        _ffn_body,
        grid=(T // bm, F // fc),
        in_specs=[
            pl.BlockSpec((bm, D), lambda i, c: (i, 0)),
            pl.BlockSpec((1, D), lambda i, c: (0, 0)),
            pl.BlockSpec((D, fc), lambda i, c: (0, c)),
            pl.BlockSpec((D, fc), lambda i, c: (0, c)),
            pl.BlockSpec((fc, D), lambda i, c: (c, 0)),
        ],
        out_specs=pl.BlockSpec((bm, D), lambda i, c: (i, 0)),
        out_shape=jax.ShapeDtypeStruct((T, D), F32),
        scratch_shapes=[pltpu.VMEM((bm, D), BF16)],
        compiler_params=_params("parallel", "arbitrary"),
        name="dense_ffn",
    )(h, g, wg, wu, wd)


ROUTE_I1, ROUTE_I2, ROUTE_W1, ROUTE_W2, ROUTE_R1, ROUTE_R2 = range(6)


def _route_rows(x, g_ref, r_ref, tri_ref, route_ref, route_t_ref, cnt_ref, carry_ref, n_experts):
    @pl.when(pl.program_id(0) == 0)
    def _():
        carry_ref[...] = jnp.zeros_like(carry_ref)

    xn = _rmsnorm(x, g_ref[...])
    x_hi = xn.astype(BF16)
    x_lo = (xn - x_hi.astype(F32)).astype(BF16)
    hi_both = _dot(x_hi, r_ref[...])
    logits = hi_both[:, :LANES] + (hi_both[:, LANES:] + _dot(x_lo, r_ref[:, pl.ds(0, LANES)]))

    lane = lax.broadcasted_iota(jnp.int32, (1, LANES), 1)
    lane_f = lane.astype(F32)
    lg = jnp.where(lane < n_experts, logits, NEG_BIG)
    v1 = jnp.max(lg, axis=-1, keepdims=True)
    i1 = jnp.min(jnp.where(lg == v1, lane_f, float(LANES)), axis=-1, keepdims=True)
    lg2 = jnp.where(lane_f == i1, NEG_BIG, lg)
    v2 = jnp.max(lg2, axis=-1, keepdims=True)
    i2 = jnp.min(jnp.where(lg2 == v2, lane_f, float(LANES)), axis=-1, keepdims=True)
    e2 = jnp.exp(v2 - v1)
    w1 = 1.0 / (1.0 + e2)
    w2 = e2 / (1.0 + e2)

    sel1 = lane_f == i1
    sel2 = lane_f == i2
    onehot = jnp.where(sel1 | sel2, 1.0, 0.0)
    rank = _dot(tri_ref[...], onehot.astype(BF16)) + carry_ref[...]
    r1 = jnp.sum(jnp.where(sel1, rank, 0.0), axis=-1, keepdims=True)
    r2 = jnp.sum(jnp.where(sel2, rank, 0.0), axis=-1, keepdims=True)
    carry_ref[...] += jnp.sum(onehot, axis=0, keepdims=True)

    route = jnp.zeros(route_ref.shape, F32)
    for col, val in ((ROUTE_I1, i1), (ROUTE_I2, i2), (ROUTE_W1, w1), (ROUTE_W2, w2), (ROUTE_R1, r1), (ROUTE_R2, r2)):
        route = jnp.where(lane == col, val, route)
    route_ref[...] = route
    route_t_ref[...] = jnp.transpose(route)[:route_t_ref.shape[0]]
    cnt_ref[...] = jnp.broadcast_to(carry_ref[...], cnt_ref.shape)


ZERO_ROWS = 64
ROW_DMA_UNROLL = 8


def _dispatch_body(pos_ref, seg_ref, x_ref, g_ref, xs_hbm, xn_ref, zero_ref, sems, *, bm, T, n_experts, be, n_rows):
    i = pl.program_id(0)
    n_steps = pl.num_programs(0)
    slot = i % 2

    def zero_copy(first_row, c):
        row = pl.multiple_of(first_row + c * ZERO_ROWS, ZERO_ROWS)
        return pltpu.make_async_copy(zero_ref, xs_hbm.at[pl.ds(row, ZERO_ROWS), :], sems.at[2])

    def zero_block(first_row):
        for c in range(be // ZERO_ROWS):
            zero_copy(first_row, c).start()
        for c in range(be // ZERO_ROWS):
            zero_copy(first_row, c).wait()

    @pl.when(i == 0)
    def _():
        zero_ref[...] = jnp.zeros_like(zero_ref)
        for e in range(n_experts):
            @pl.when(seg_ref[n_experts + e] > 0)
            def _():
                zero_block(seg_ref[e] - be)

        def clear_unused(j, c):
            zero_block(j * be)
            return c

        lax.fori_loop(seg_ref[n_experts - 1] // be, n_rows // be, clear_unused, 0)

    def drain(s):
        for _ in range(TOP_K):
            pltpu.make_async_copy(xn_ref.at[s], xs_hbm.at[pl.ds(0, bm), :], sems.at[s]).wait()

    @pl.when(i >= 2)
    def _():
        drain(slot)

    xn_ref[slot] = _rmsnorm(x_ref[...], g_ref[...])

    def row_copy(r, k):
        dst = pos_ref[k * T + i * bm + r]
        return pltpu.make_async_copy(xn_ref.at[slot, pl.ds(r, 1), :], xs_hbm.at[pl.ds(dst, 1), :], sems.at[slot])

    def issue(grp, c):
        for u in range(ROW_DMA_UNROLL):
            row_copy(grp * ROW_DMA_UNROLL + u, 0).start(priority=u % 2)
            row_copy(grp * ROW_DMA_UNROLL + u, 1).start(priority=(u + 1) % 2)
        return c

    lax.fori_loop(0, bm // ROW_DMA_UNROLL, issue, 0)

    @pl.when(i == n_steps - 1)
    def _():
        drain(slot)

        @pl.when(i >= 1)
        def _():
            drain(1 - slot)


def _dispatch(pos, seg, h, g, *, n_rows, n_experts, be):
    T, D = h.shape
    bm = _largest_divisor(T, (256, 128, 64, 32, 16))
    body = functools.partial(_dispatch_body, bm=bm, T=T, n_experts=n_experts, be=be, n_rows=n_rows)
    return pl.pallas_call(
        body,
        grid_spec=pltpu.PrefetchScalarGridSpec(
            num_scalar_prefetch=2,
            grid=(T // bm,),
            in_specs=[pl.BlockSpec((bm, D), lambda i, pos, seg: (i, 0)),
                      pl.BlockSpec((1, D), lambda i, pos, seg: (0, 0))],
            out_specs=pl.BlockSpec(memory_space=pl.ANY),
            scratch_shapes=[pltpu.VMEM((2, bm, D), F32), pltpu.VMEM((ZERO_ROWS, D), F32),
                            pltpu.SemaphoreType.DMA((3,))],
        ),
        out_shape=jax.ShapeDtypeStruct((n_rows, D), F32),
        compiler_params=_params("arbitrary"),
        name="moe_dispatch",
    )(pos, seg, h, g)


def _expert_body(blk_ref, x_ref, wg_ref, wu_ref, wd_ref, o_ref, xb_ref, *, n_blocks):
    j = pl.program_id(0)

    @pl.when(pl.program_id(1) == 0)
    def _():
        o_ref[...] = jnp.zeros_like(o_ref)

    @pl.when(j < blk_ref[n_blocks])
    def _():
        @pl.when(pl.program_id(1) == 0)
        def _():
            xb_ref[...] = x_ref[...].astype(BF16)

        xb = xb_ref[...]
        hid = jax.nn.silu(_dot(xb, wg_ref[...])) * _dot(xb, wu_ref[...])
        o_ref[...] += _dot(hid.astype(BF16), wd_ref[...])


def _expert_ffn(blk, xs, wg, wu, wd, *, be):
    n_rows, D = xs.shape
    F = wg.shape[2]
    n_blocks = n_rows // be
    fc = _largest_divisor(F, (512, 256, 128))
    nc = F // fc
    body = functools.partial(_expert_body, n_blocks=n_blocks)

    def row_blk(j, blk):
        return jnp.minimum(j, blk[n_blocks] - 1)

    def chunk(j, c, blk):
        return jnp.where(j < blk[n_blocks], c, nc - 1)

    return pl.pallas_call(
        body,
        grid_spec=pltpu.PrefetchScalarGridSpec(
            num_scalar_prefetch=1,
            grid=(n_blocks, nc),
            in_specs=[
                pl.BlockSpec((be, D), lambda j, c, blk: (row_blk(j, blk), 0)),
                pl.BlockSpec((None, D, fc), lambda j, c, blk: (blk[row_blk(j, blk)], 0, chunk(j, c, blk))),
                pl.BlockSpec((None, D, fc), lambda j, c, blk: (blk[row_blk(j, blk)], 0, chunk(j, c, blk))),
                pl.BlockSpec((None, fc, D), lambda j, c, blk: (blk[row_blk(j, blk)], chunk(j, c, blk), 0)),
            ],
            out_specs=pl.BlockSpec((be, D), lambda j, c, blk: (j, 0)),
            scratch_shapes=[pltpu.VMEM((be, D), BF16)],
        ),
        out_shape=jax.ShapeDtypeStruct((n_rows, D), F32),
        compiler_params=_params("arbitrary", "arbitrary"),
        name="moe_experts",
    )(blk, xs, wg, wu, wd)


def _combine_body(pos_ref, h_ref, route_ref, g_ref, y_hbm, o_ref, y_ref, sems, *, bm, T):
    i = pl.program_id(0)
    n_steps = pl.num_programs(0)
    slot = i % 2

    def start_gathers(step, s):
        def row_copy(r, k):
            src = pos_ref[k * T + step * bm + r]
            return pltpu.make_async_copy(y_hbm.at[pl.ds(src, 1), :], y_ref.at[s, k, pl.ds(r, 1), :], sems.at[s])

        def issue(grp, c):
            for u in range(ROW_DMA_UNROLL):
                row_copy(grp * ROW_DMA_UNROLL + u, 0).start(priority=u % 2)
                row_copy(grp * ROW_DMA_UNROLL + u, 1).start(priority=(u + 1) % 2)
            return c

        lax.fori_loop(0, bm // ROW_DMA_UNROLL, issue, 0)

    @pl.when(i == 0)
    def _():
        start_gathers(i, slot)

    @pl.when(i + 1 < n_steps)
    def _():
        start_gathers(i + 1, 1 - slot)

    for k in range(TOP_K):
        pltpu.make_async_copy(y_hbm.at[pl.ds(0, bm), :], y_ref.at[slot, k], sems.at[slot]).wait()

    route = route_ref[...]
    w1 = route[:, ROUTE_W1:ROUTE_W1 + 1]
    w2 = route[:, ROUTE_W2:ROUTE_W2 + 1]
    h = h_ref[...] + (w1 * y_ref[slot, 0] + w2 * y_ref[slot, 1])
    o_ref[...] = _rmsnorm(h, g_ref[...])


def _combine(pos, h, route, g, y, *, B, L):
    T, D = h.shape
    S = L - N_META
    bm = _largest_divisor(L, tuple(b for b in range(512, 7, -8)))
    blocks_per_seq = L // bm
    assert pl.cdiv(S, bm) == blocks_per_seq
    body = functools.partial(_combine_body, bm=bm, T=T)
    return pl.pallas_call(
        body,
        grid_spec=pltpu.PrefetchScalarGridSpec(
            num_scalar_prefetch=1,
            grid=(T // bm,),
            in_specs=[pl.BlockSpec((bm, D), lambda i, pos: (i, 0)),
                      pl.BlockSpec((bm, LANES), lambda i, pos: (i, 0)),
                      pl.BlockSpec((1, D), lambda i, pos: (0, 0)),
                      pl.BlockSpec(memory_space=pl.ANY)],
            out_specs=pl.BlockSpec((None, bm, D), lambda i, pos: (i // blocks_per_seq, i % blocks_per_seq, 0)),
            scratch_shapes=[pltpu.VMEM((2, TOP_K, bm, D), F32), pltpu.SemaphoreType.DMA((2,))],
        ),
        out_shape=jax.ShapeDtypeStruct((B, S, D), F32),
        compiler_params=_params("arbitrary"),
        name="moe_combine",
    )(pos, h, route, g, y)


def _rope_tables(B, L):
    half = HEAD_DIM // 2
    inv = ROPE_THETA ** (-2.0 * jnp.arange(half, dtype=F32) / HEAD_DIM)
    pos = jnp.concatenate([jnp.arange(N_META, L), jnp.arange(N_META)]).astype(F32)
    ang = pos[:, None] * inv[None, :]
    cs = jnp.tile(jnp.stack([jnp.cos(ang), jnp.sin(ang)]), (1, 1, LANES // half))
    cs = jnp.stack([cs * (HEAD_DIM ** -0.5 * LOG2_E), cs])
    return jnp.broadcast_to(cs[:, :, None], (2, 2, B, L, LANES)).reshape(2, 2, B * L, LANES)


def _pair_rotary_halves(w_qk, n_heads):
    D = w_qk.shape[0]
    w = w_qk.reshape(D, 2, n_heads // 2, 2, 2, 2, HEAD_DIM // 2)
    return w.transpose(0, 1, 2, 5, 3, 4, 6).reshape(D, -1)


def _token_mixer(h, layer, B, L, w_in, b_gate, lam, subln_g, pool_w, pool_scale, w_ba, w_bp, w_out, norm_g, rope_cs,
                 routing=None):
    T, D = h.shape
    A, P = w_ba.shape[0], w_bp.shape[0]
    n_heads = A // (2 * HEAD_DIM)
    lambda_init = 0.8 - 0.6 * math.exp(-0.3 * layer)
    w_qk = _pair_rotary_halves(w_in[:, :2 * A], n_heads).astype(BF16)
    z = _inproj(h, norm_g, w_qk, w_in.astype(BF16), rope_cs)
    z3 = z.reshape(B, L, z.shape[1])
    a = _attention(z3, lam, subln_g, n_heads=n_heads, lambda_init=lambda_init)
    p = _pool(z3, pool_w, pool_scale, col_block=(3 * A) // P)
    return _mix(a.reshape(T, A), p.reshape(T, P), z, b_gate, w_ba, w_bp, w_out, h,
                gate_col_block=(3 * A + P) // (2 * D), routing=routing)


def _router_hi_lo(router):
    D, E = router.shape
    r_pad = jnp.zeros((D, LANES), F32).at[:, :E].set(router)
    r_hi = r_pad.astype(BF16)
    r_lo = (r_pad - r_hi.astype(F32)).astype(BF16)
    return jnp.concatenate([r_hi, r_lo], axis=1)


def _moe_and_final_norm(h, route, route_t, cnt, norm_g, E, wg, wu, wd, final_g, *, B, L):
    T, D = h.shape
    be = 1024 if T >= 8192 else 64

    counts = cnt[0, :E].astype(jnp.int32)
    n_blk = (counts + be - 1) // be
    seg_end = jnp.cumsum(n_blk * be)
    seg_start = seg_end - n_blk * be
    n_blocks = (TOP_K * T + E * (be - 1)) // be
    blk_end = jnp.cumsum(n_blk)
    blk_expert = jnp.minimum(jnp.sum(jnp.arange(n_blocks)[:, None] >= blk_end[None, :], axis=1), E - 1)
    blk = jnp.concatenate([blk_expert.astype(jnp.int32), blk_end[-1:].astype(jnp.int32)])
    seg = jnp.concatenate([seg_end, n_blk]).astype(jnp.int32)

    def slot_of(expert_f, rank_f):
        start = jnp.sum(jnp.where(expert_f[:, None] == jnp.arange(E, dtype=F32)[None, :], seg_start[None, :], 0), axis=1)
        return start + rank_f.astype(jnp.int32)

    pos = jnp.concatenate([slot_of(route_t[ROUTE_I1], route_t[ROUTE_R1]), slot_of(route_t[ROUTE_I2], route_t[ROUTE_R2])])

    xs = _dispatch(pos, seg, h, norm_g, n_rows=n_blocks * be, n_experts=E, be=be)
    y = _expert_ffn(blk, xs, wg, wu, wd, be=be)
    return _combine(pos, h, route, final_g, y, B=B, L=L)


def kernel(x, meta_tokens, norm_mix, w_in, b_gate, lambda_q1, lambda_k1, lambda_q2, lambda_k2, subln, pool_w, pool_scale, w_branch_attn, w_branch_pool, w_out, norm_ffn, dense_w_gate, dense_w_up, dense_w_down, router, moe_w_gate, moe_w_up, moe_w_down, norm_final):
    B, S, D = x.shape
    depth = w_in.shape[0]
    assert depth == 2, "the final RMSNorm is fused into the routed layer, which must be the last one"
    L = N_META + S
    T = B * L
    meta = jnp.broadcast_to(meta_tokens[None].astype(x.dtype), (B, N_META, D))
    h = jnp.concatenate([x, meta], axis=1).reshape(T, D)
    rope = _rope_tables(B, L)
    row = lambda v: v.reshape(1, -1)

    for i in range(depth):
        lam = jnp.stack([lambda_q1[i], lambda_k1[i], lambda_q2[i], lambda_k2[i]])
        j = i // 2
        routed = i % 2 == 1
        n_experts = router.shape[2]
        mixed = _token_mixer(h, i, B, L, w_in[i], row(b_gate[i]), lam, row(subln[i]),
                             pool_w[i].astype(BF16), row(pool_scale[i]), w_branch_attn[i].astype(BF16),
                             w_branch_pool[i].astype(BF16), w_out[i].astype(BF16), row(norm_mix[i]), rope,
                             routing=(row(norm_ffn[i]), _router_hi_lo(router[j]), n_experts) if routed else None)
        if not routed:
            h = _dense_ffn(mixed, row(norm_ffn[i]), dense_w_gate[j].astype(BF16), dense_w_up[j].astype(BF16),
                           dense_w_down[j].astype(BF16))
        else:
            h, route, route_t, cnt = mixed
            h = _moe_and_final_norm(h, route, route_t, cnt, row(norm_ffn[i]), n_experts, moe_w_gate[j].astype(BF16),
                                    moe_w_up[j].astype(BF16), moe_w_down[j].astype(BF16), row(norm_final),
                                    B=B, L=L)
    return h
```

```python
import functools
import math

import jax
import jax.numpy as jnp
from jax import lax
from jax.experimental import pallas as pl
from jax.experimental.pallas import tpu as pltpu

N_META = 16
HEAD_DIM = 64
POOL_WINDOWS = (2, 4, 8, 16)
TOP_K = 2
ROPE_THETA = 10000.0
EPS = 1e-5

LANES = 128
SUBLANES_BF16 = 16
VMEM_CAP_BYTES = 64 * 1024 * 1024
VMEM_LIMIT_BYTES = VMEM_CAP_BYTES - 6 * 1024 * 1024

NEG_BIG = -1e30
LOG2_E = 1.4426950408889634
POOL_PAD = max(POOL_WINDOWS)
MXU_DEPTH = 256
POOL_BLOCK_ROWS = MXU_DEPTH - 2 * POOL_PAD

F32 = jnp.float32
BF16 = jnp.bfloat16


def _largest_divisor(n, candidates):
    for c in candidates:
        if c <= n and n % c == 0:
            return c
    raise ValueError(f"no block size in {candidates} divides {n}")


def _params(*semantics, vmem_limit_bytes=VMEM_LIMIT_BYTES):
    return pltpu.CompilerParams(dimension_semantics=semantics, vmem_limit_bytes=vmem_limit_bytes)


def _rmsnorm(x, g):
    ms = jnp.mean(x * x, axis=-1, keepdims=True)
    return x * lax.rsqrt(ms + EPS) * g


def _dot(a, b):
    return jnp.dot(a, b, preferred_element_type=F32)


def _inproj_body(x_ref, g_ref, wqk_ref, wr_ref, cs_ref, o_ref, xn_ref, *, n_qk):
    _inproj_step(lambda: x_ref[...], g_ref, wqk_ref, wr_ref, cs_ref, o_ref, xn_ref, n_qk)


def _inproj_first_body(x_ref, meta_ref, g_ref, wqk_ref, wr_ref, cs_ref, o_ref, h_ref, xn_ref, *, n_qk, blocks_per_seq):
    def assemble():
        h_ref[...] = x_ref[...]

        @pl.when(pl.program_id(0) % blocks_per_seq == blocks_per_seq - 1)
        def _():
            h_ref[pl.ds(h_ref.shape[0] - N_META, N_META), :] = meta_ref[...]

        return h_ref[...]

    _inproj_step(assemble, g_ref, wqk_ref, wr_ref, cs_ref, o_ref, xn_ref, n_qk)


def _inproj_step(load_rows, g_ref, wqk_ref, wr_ref, cs_ref, o_ref, xn_ref, n_qk):
    j = pl.program_id(1)

    @pl.when(j == 0)
    def _():
        xn_ref[...] = _rmsnorm(load_rows(), g_ref[...]).astype(BF16)
        acc = _dot(xn_ref[...], wqk_ref[...])
        for c in range(0, n_qk, 2 * LANES):
            table = 0 if c < n_qk // 2 else 1
            cos, sin = cs_ref[table, 0], cs_ref[table, 1]
            t1 = acc[:, c:c + LANES]
            t2 = acc[:, c + LANES:c + 2 * LANES]
            o_ref[:, c:c + LANES] = (t1 * cos - t2 * sin).astype(o_ref.dtype)
            o_ref[:, c + LANES:c + 2 * LANES] = (t2 * cos + t1 * sin).astype(o_ref.dtype)

    @pl.when(j > 0)
    def _():
        o_ref[...] = _dot(xn_ref[...], wr_ref[...]).astype(o_ref.dtype)


def _inproj(h, g, w_qk, w_all, rope_cs):
    T, D = h.shape
    n_qk = w_qk.shape[1]
    n_rest = w_all.shape[1] - n_qk
    assert n_rest % n_qk == 0
    bm = _largest_divisor(T, (768, 512, 384, 256, 128, 64, 32, 16))
    body = functools.partial(_inproj_body, n_qk=n_qk)
    return pl.pallas_call(
        body,
        grid=(T // bm, 1 + n_rest // n_qk),
        in_specs=[
            pl.BlockSpec((bm, D), lambda i, j: (i, 0)),
            pl.BlockSpec((1, D), lambda i, j: (0, 0)),
            pl.BlockSpec((D, n_qk), lambda i, j: (0, 0), pipeline_mode=pl.Buffered(1)),
            pl.BlockSpec((D, n_qk), lambda i, j: (0, jnp.maximum(j, 1))),
            pl.BlockSpec((2, 2, bm, LANES), lambda i, j: (0, 0, i, 0)),
        ],
        out_specs=pl.BlockSpec((bm, n_qk), lambda i, j: (i, j)),
        out_shape=jax.ShapeDtypeStruct((T, n_qk + n_rest), BF16),
        scratch_shapes=[pltpu.VMEM((bm, D), BF16)],
        compiler_params=_params("parallel", "arbitrary"),
        name="inproj",
    )(h, g, w_qk, w_all, rope_cs)


def _inproj_first(x, meta, g, w_qk, w_all, rope_cs):
    B, S, D = x.shape
    L = S + N_META
    T = B * L
    n_qk = w_qk.shape[1]
    n_rest = w_all.shape[1] - n_qk
    assert n_rest % n_qk == 0
    bm = _largest_divisor(L, tuple(b for b in range(768, 15, -16)))
    blocks_per_seq = L // bm
    assert pl.cdiv(S, bm) == blocks_per_seq and blocks_per_seq * bm - S == N_META
    body = functools.partial(_inproj_first_body, n_qk=n_qk, blocks_per_seq=blocks_per_seq)
    return pl.pallas_call(
        body,
        grid=(T // bm, 1 + n_rest // n_qk),
        in_specs=[
            pl.BlockSpec((None, bm, D), lambda i, j: (i // blocks_per_seq, i % blocks_per_seq, 0)),
            pl.BlockSpec((N_META, D), lambda i, j: (0, 0)),
            pl.BlockSpec((1, D), lambda i, j: (0, 0)),
            pl.BlockSpec((D, n_qk), lambda i, j: (0, 0), pipeline_mode=pl.Buffered(1)),
            pl.BlockSpec((D, n_qk), lambda i, j: (0, jnp.maximum(j, 1))),
            pl.BlockSpec((2, 2, bm, LANES), lambda i, j: (0, 0, i, 0)),
        ],
        out_specs=[pl.BlockSpec((bm, n_qk), lambda i, j: (i, j)), pl.BlockSpec((bm, D), lambda i, j: (i, 0))],
        out_shape=[jax.ShapeDtypeStruct((T, n_qk + n_rest), BF16), jax.ShapeDtypeStruct((T, D), F32)],
        scratch_shapes=[pltpu.VMEM((bm, D), BF16)],
        compiler_params=_params("parallel", "arbitrary", vmem_limit_bytes=VMEM_CAP_BYTES - 2 * 1024 * 1024),
        name="inproj_first",
    )(x, meta, g, w_qk, w_all, rope_cs)


PAIR = 2 * LANES


def _attn_body(lam_ref, q_ref, k_ref, v_ref, sg_ref, o_ref, kp_ref, vp_ref, of_ref,
               *, L, l_main, l_keys, bq, lambda_init):
    lane = lax.broadcasted_iota(jnp.int32, (1, PAIR), 1)
    map_of_lane = (lane % HEAD_DIM) // (HEAD_DIM // 2)

    kp_ref[pl.ds(0, L), :] = k_ref[...]
    if l_keys > L:
        kp_ref[pl.ds(L, l_keys - L), :] = jnp.zeros((l_keys - L, PAIR), BF16)
    for hh in range(2):
        vp_ref[hh, pl.ds(0, L), pl.ds(0, LANES)] = v_ref[:, hh * LANES:(hh + 1) * LANES]
        vp_ref[hh, :, pl.ds(LANES, LANES)] = jnp.ones((l_keys, LANES), BF16)
        if l_keys > L:
            vp_ref[hh, pl.ds(L, l_keys - L), pl.ds(0, LANES)] = jnp.zeros((l_keys - L, LANES), BF16)

    lam = lam_ref[...]
    lam_full = (jnp.exp(jnp.sum(lam[0:1] * lam[1:2], axis=-1, keepdims=True))
                - jnp.exp(jnp.sum(lam[2:3] * lam[3:4], axis=-1, keepdims=True)) + lambda_init)

    tail_valid = (lax.broadcasted_iota(jnp.int32, (1, LANES), 1) + l_main) < L
    contract_last = (((1,), (1,)), ((), ()))
    zero = jnp.zeros((), BF16)
    q_all = q_ref[...]

    for hh in range(2):
        in_head = ((lane % LANES) // HEAD_DIM) == hh
        map_masks = (in_head & (map_of_lane == 0), in_head & (map_of_lane == 1))
        for i in range(L // bq):
            qb = q_all[i * bq:(i + 1) * bq]
            q_maps = jnp.concatenate([jnp.where(m, qb, zero) for m in map_masks], axis=0)
            s = lax.dot_general(q_maps, kp_ref[...], contract_last, preferred_element_type=F32)
            s_main = s[:, :l_main]
            m = jnp.max(s_main, axis=-1, keepdims=True)
            if l_keys > l_main:
                s_tail = jnp.where(tail_valid, s[:, l_main:], NEG_BIG)
                m = jnp.maximum(m, jnp.max(s_tail, axis=-1, keepdims=True))
            acc = _dot(jnp.exp2(s_main - m).astype(BF16), vp_ref[hh, pl.ds(0, l_main), :])
            if l_keys > l_main:
                acc = acc + _dot(jnp.exp2(s_tail - m).astype(BF16), vp_ref[hh, pl.ds(l_main, l_keys - l_main), :])
            a0, a1 = acc[:bq], acc[bq:]
            o = a0[:, :LANES] * (1.0 / a0[:, LANES:LANES + 1]) - a1[:, :LANES] * (lam_full / a1[:, LANES:LANES + 1])
            of_ref[pl.ds(i * bq, bq), pl.ds(hh * LANES, LANES)] = _rmsnorm(o, sg_ref[...]) * (1.0 - lambda_init)
    o_ref[...] = of_ref[...].astype(o_ref.dtype)


def _attention(z3, lam, subln_g, *, n_heads, lambda_init):
    B, L, _ = z3.shape
    l_main = (L // LANES) * LANES
    l_keys = l_main if l_main == L else l_main + LANES
    bq = _largest_divisor(L, tuple(b for b in range(400, 7, -8)))
    n_pairs = n_heads // 2
    body = functools.partial(_attn_body, L=L, l_main=l_main, l_keys=l_keys, bq=bq, lambda_init=lambda_init)
    pair = lambda off: pl.BlockSpec((None, L, PAIR), lambda b, h: (b, 0, off + h))
    const = lambda shape: pl.BlockSpec(shape, lambda b, h: (0,) * len(shape))
    return pl.pallas_call(
        body,
        grid=(B, n_pairs),
        in_specs=[const(lam.shape), pair(0), pair(n_pairs), pair(2 * n_pairs), const(subln_g.shape)],
        out_specs=pl.BlockSpec((None, L, PAIR), lambda b, h: (b, 0, h)),
        out_shape=jax.ShapeDtypeStruct((B, L, n_heads * LANES), BF16),
        scratch_shapes=[pltpu.VMEM((l_keys, PAIR), BF16), pltpu.VMEM((2, l_keys, PAIR), BF16),
                        pltpu.VMEM((L, PAIR), F32)],
        compiler_params=_params("parallel", "parallel"),
        name="diff_attention",
    )(lam, z3, z3, z3, subln_g)


def _pool_body(u_ref, band_ref, pw_ref, ps_ref, o_ref, pad_ref, *, L, cg, br):
    S = L - N_META
    P = pad_ref.shape[1]
    pad_ref[pl.ds(0, POOL_PAD), :] = jnp.zeros((POOL_PAD, P), pad_ref.dtype)
    pad_ref[pl.ds(POOL_PAD + L, POOL_PAD), :] = jnp.zeros((POOL_PAD, P), pad_ref.dtype)
    pad_ref[pl.ds(POOL_PAD, N_META), :] = u_ref[pl.ds(S, N_META), :]
    pad_ref[pl.ds(POOL_PAD + N_META, S), :] = u_ref[pl.ds(0, S), :]
    for start in range(0, L, br):
        rows = min(br, L - start)
        t = lax.broadcasted_iota(jnp.int32, (rows, 1), 0) + start
        for g, w in enumerate(POOL_WINDOWS):
            cols = slice(g * cg, (g + 1) * cg)
            win = _dot(band_ref[g, pl.ds(0, rows), pl.ds(0, rows + 2 * POOL_PAD)],
                       pad_ref[pl.ds(start, rows + 2 * POOL_PAD), cols])
            cnt = (jnp.minimum(t + w // 2, L) - jnp.maximum(t - w // 2, 0)).astype(F32)
            m = win * (1.0 / cnt) - pad_ref[pl.ds(start + POOL_PAD, rows), cols].astype(F32)
            y = (_dot(m.astype(BF16), pw_ref[g]) * ps_ref[:, cols]).astype(o_ref.dtype)
            if start == 0:
                o_ref[pl.ds(S, N_META), cols] = y[:N_META]
                o_ref[pl.ds(0, rows - N_META), cols] = y[N_META:]
            else:
                o_ref[pl.ds(start - N_META, rows), cols] = y


def _pool(z3, pool_w, pool_scale, *, col_block):
    B, L, _ = z3.shape
    G, cg, _ = pool_w.shape
    P = G * cg
    br = min(L, POOL_BLOCK_ROWS)
    assert L % SUBLANES_BF16 == 0 and N_META % SUBLANES_BF16 == 0
    r = lax.broadcasted_iota(jnp.int32, (br, br + 2 * POOL_PAD), 0)
    c = lax.broadcasted_iota(jnp.int32, (br, br + 2 * POOL_PAD), 1) - POOL_PAD
    band = jnp.stack([((c >= r - w // 2) & (c < r + w // 2)).astype(BF16) for w in POOL_WINDOWS])
    body = functools.partial(_pool_body, L=L, cg=cg, br=br)
    return pl.pallas_call(
        body,
        grid=(B,),
        in_specs=[
            pl.BlockSpec((None, L, P), lambda b: (b, 0, col_block)),
            pl.BlockSpec(band.shape, lambda b: (0, 0, 0)),
            pl.BlockSpec((G, cg, cg), lambda b: (0, 0, 0)),
            pl.BlockSpec((1, P), lambda b: (0, 0)),
        ],
        out_specs=pl.BlockSpec((None, L, P), lambda b: (b, 0, 0)),
        out_shape=jax.ShapeDtypeStruct((B, L, P), BF16),
        scratch_shapes=[pltpu.VMEM((L + 2 * POOL_PAD, P), BF16)],
        compiler_params=_params("parallel"),
        name="pool_mixer",
    )(z3, band, pool_w, pool_scale)


def _mix_rows(a_ref, p_ref, gt_ref, b_ref, wa_ref, wp_ref, wo_ref, h_ref, D):
    gates = jax.nn.sigmoid(gt_ref[...].astype(F32) + b_ref[...])
    y = gates[:, :D] * _dot(a_ref[...], wa_ref[...]) + gates[:, D:] * _dot(p_ref[...], wp_ref[...])
    return h_ref[...] + _dot(y.astype(BF16), wo_ref[...])


def _mix_body(a_ref, p_ref, gt_ref, b_ref, wa_ref, wp_ref, wo_ref, h_ref, o_ref, *, D):
    o_ref[...] = _mix_rows(a_ref, p_ref, gt_ref, b_ref, wa_ref, wp_ref, wo_ref, h_ref, D)


def _mix_route_body(a_ref, p_ref, gt_ref, b_ref, wa_ref, wp_ref, wo_ref, h_ref, g_ref, r_ref, tri_ref,
                    o_ref, route_ref, route_t_ref, cnt_ref, carry_ref, *, D, n_experts):
    h = _mix_rows(a_ref, p_ref, gt_ref, b_ref, wa_ref, wp_ref, wo_ref, h_ref, D)
    o_ref[...] = h
    _route_rows(h, g_ref, r_ref, tri_ref, route_ref, route_t_ref, cnt_ref, carry_ref, n_experts)


def _mix(a, p, z, b_gate, wa, wp, wo, h, *, gate_col_block, routing=None):
    T, D = h.shape
    A, P = a.shape[1], p.shape[1]
    bm = _largest_divisor(T, (384, 256, 128, 64, 32, 16))
    const = lambda shape: pl.BlockSpec(shape, lambda i: (0, 0), pipeline_mode=pl.Buffered(1))
    rows = lambda width: pl.BlockSpec((bm, width), lambda i: (i, 0))
    in_specs = [rows(A), rows(P), pl.BlockSpec((bm, 2 * D), lambda i: (i, gate_col_block)),
                const((1, 2 * D)), const((A, D)), const((P, D)), const((D, D)), rows(D)]
    if routing is None:
        return pl.pallas_call(
            functools.partial(_mix_body, D=D),
            grid=(T // bm,),
            in_specs=in_specs,
            out_specs=rows(D),
            out_shape=jax.ShapeDtypeStruct((T, D), F32),
            compiler_params=_params("parallel"),
            name="mix_out",
        )(a, p, z, b_gate, wa, wp, wo, h)
    g_ffn, r_hi_lo, n_experts = routing
    tri = (lax.broadcasted_iota(jnp.int32, (bm, bm), 0) > lax.broadcasted_iota(jnp.int32, (bm, bm), 1)).astype(BF16)
    return pl.pallas_call(
        functools.partial(_mix_route_body, D=D, n_experts=n_experts),
        grid=(T // bm,),
        in_specs=in_specs + [const((1, D)), const((D, 2 * LANES)), const((bm, bm))],
        out_specs=[rows(D), rows(LANES), pl.BlockSpec((8, bm), lambda i: (0, i)),
                   pl.BlockSpec((8, LANES), lambda i: (0, 0))],
        out_shape=[jax.ShapeDtypeStruct((T, D), F32), jax.ShapeDtypeStruct((T, LANES), F32),
                   jax.ShapeDtypeStruct((8, T), F32), jax.ShapeDtypeStruct((8, LANES), F32)],
        scratch_shapes=[pltpu.VMEM((1, LANES), F32)],
        compiler_params=_params("arbitrary"),
        name="mix_out_route",
    )(a, p, z, b_gate, wa, wp, wo, h, g_ffn, r_hi_lo, tri)


def _ffn_body(x_ref, g_ref, wg_ref, wu_ref, wd_ref, o_ref, xn_ref):
    @pl.when(pl.program_id(1) == 0)
    def _():
        x = x_ref[...]
        xn_ref[...] = _rmsnorm(x, g_ref[...]).astype(BF16)
        o_ref[...] = x

    xn = xn_ref[...]
    hid = jax.nn.silu(_dot(xn, wg_ref[...])) * _dot(xn, wu_ref[...])
    o_ref[...] += _dot(hid.astype(BF16), wd_ref[...])


def _dense_ffn(h, g, wg, wu, wd):
    T, D = h.shape
    F = wg.shape[1]
    bm = _largest_divisor(T, (768, 512, 384, 256, 128, 64, 32, 16))
    fc = _largest_divisor(F, (512, 256, 128))
    return pl.pallas_call(
        _ffn_body,
        grid=(T // bm, F // fc),
        in_specs=[
            pl.BlockSpec((bm, D), lambda i, c: (i, 0)),
            pl.BlockSpec((1, D), lambda i, c: (0, 0)),
            pl.BlockSpec((D, fc), lambda i, c: (0, c)),
            pl.BlockSpec((D, fc), lambda i, c: (0, c)),
            pl.BlockSpec((fc, D), lambda i, c: (c, 0)),
        ],
        out_specs=pl.BlockSpec((bm, D), lambda i, c: (i, 0)),
        out_shape=jax.ShapeDtypeStruct((T, D), F32),
        scratch_shapes=[pltpu.VMEM((bm, D), BF16)],
        compiler_params=_params("parallel", "arbitrary"),
        name="dense_ffn",
    )(h, g, wg, wu, wd)


ROUTE_I1, ROUTE_I2, ROUTE_W1, ROUTE_W2, ROUTE_R1, ROUTE_R2 = range(6)


def _route_rows(x, g_ref, r_ref, tri_ref, route_ref, route_t_ref, cnt_ref, carry_ref, n_experts):
    @pl.when(pl.program_id(0) == 0)
    def _():
        carry_ref[...] = jnp.zeros_like(carry_ref)

    xn = _rmsnorm(x, g_ref[...])
    x_hi = xn.astype(BF16)
    x_lo = (xn - x_hi.astype(F32)).astype(BF16)
    hi_both = _dot(x_hi, r_ref[...])
    logits = hi_both[:, :LANES] + (hi_both[:, LANES:] + _dot(x_lo, r_ref[:, pl.ds(0, LANES)]))

    lane = lax.broadcasted_iota(jnp.int32, (1, LANES), 1)
    lane_f = lane.astype(F32)
    lg = jnp.where(lane < n_experts, logits, NEG_BIG)
    v1 = jnp.max(lg, axis=-1, keepdims=True)
    i1 = jnp.min(jnp.where(lg == v1, lane_f, float(LANES)), axis=-1, keepdims=True)
    lg2 = jnp.where(lane_f == i1, NEG_BIG, lg)
    v2 = jnp.max(lg2, axis=-1, keepdims=True)
    i2 = jnp.min(jnp.where(lg2 == v2, lane_f, float(LANES)), axis=-1, keepdims=True)
    e2 = jnp.exp(v2 - v1)
    w1 = 1.0 / (1.0 + e2)
    w2 = e2 / (1.0 + e2)

    sel1 = lane_f == i1
    sel2 = lane_f == i2
    onehot = jnp.where(sel1 | sel2, 1.0, 0.0)
    rank = _dot(tri_ref[...], onehot.astype(BF16)) + carry_ref[...]
    r1 = jnp.sum(jnp.where(sel1, rank, 0.0), axis=-1, keepdims=True)
    r2 = jnp.sum(jnp.where(sel2, rank, 0.0), axis=-1, keepdims=True)
    carry_ref[...] += jnp.sum(onehot, axis=0, keepdims=True)

    route = jnp.zeros(route_ref.shape, F32)
    for col, val in ((ROUTE_I1, i1), (ROUTE_I2, i2), (ROUTE_W1, w1), (ROUTE_W2, w2), (ROUTE_R1, r1), (ROUTE_R2, r2)):
        route = jnp.where(lane == col, val, route)
    route_ref[...] = route
    route_t_ref[...] = jnp.transpose(route)[:route_t_ref.shape[0]]
    cnt_ref[...] = jnp.broadcast_to(carry_ref[...], cnt_ref.shape)


ZERO_ROWS = 64
ROW_DMA_UNROLL = 8


def _dispatch_body(pos_ref, seg_ref, x_ref, g_ref, xs_hbm, xn_ref, zero_ref, sems, *, bm, T, n_experts, be, n_rows):
    i = pl.program_id(0)
    n_steps = pl.num_programs(0)
    slot = i % 2

    def zero_copy(first_row, c):
        row = pl.multiple_of(first_row + c * ZERO_ROWS, ZERO_ROWS)
        return pltpu.make_async_copy(zero_ref, xs_hbm.at[pl.ds(row, ZERO_ROWS), :], sems.at[2])

    def zero_block(first_row):
        for c in range(be // ZERO_ROWS):
            zero_copy(first_row, c).start()
        for c in range(be // ZERO_ROWS):
            zero_copy(first_row, c).wait()

    @pl.when(i == 0)
    def _():
        zero_ref[...] = jnp.zeros_like(zero_ref)
        for e in range(n_experts):
            @pl.when(seg_ref[n_experts + e] > 0)
            def _():
                zero_block(seg_ref[e] - be)

        def clear_unused(j, c):
            zero_block(j * be)
            return c

        lax.fori_loop(seg_ref[n_experts - 1] // be, n_rows // be, clear_unused, 0)

    def drain(s):
        for _ in range(TOP_K):
            pltpu.make_async_copy(xn_ref.at[s], xs_hbm.at[pl.ds(0, bm), :], sems.at[s]).wait()

    @pl.when(i >= 2)
    def _():
        drain(slot)

    xn_ref[slot] = _rmsnorm(x_ref[...], g_ref[...])

    def row_copy(r, k):
        dst = pos_ref[k * T + i * bm + r]
        return pltpu.make_async_copy(xn_ref.at[slot, pl.ds(r, 1), :], xs_hbm.at[pl.ds(dst, 1), :], sems.at[slot])

    def issue(grp, c):
        for u in range(ROW_DMA_UNROLL):
            row_copy(grp * ROW_DMA_UNROLL + u, 0).start(priority=u % 2)
            row_copy(grp * ROW_DMA_UNROLL + u, 1).start(priority=(u + 1) % 2)
        return c

    lax.fori_loop(0, bm // ROW_DMA_UNROLL, issue, 0)

    @pl.when(i == n_steps - 1)
    def _():
        drain(slot)

        @pl.when(i >= 1)
        def _():
            drain(1 - slot)


def _dispatch(pos, seg, h, g, *, n_rows, n_experts, be):
    T, D = h.shape
    bm = _largest_divisor(T, (256, 128, 64, 32, 16))
    body = functools.partial(_dispatch_body, bm=bm, T=T, n_experts=n_experts, be=be, n_rows=n_rows)
    return pl.pallas_call(
        body,
        grid_spec=pltpu.PrefetchScalarGridSpec(
            num_scalar_prefetch=2,
            grid=(T // bm,),
            in_specs=[pl.BlockSpec((bm, D), lambda i, pos, seg: (i, 0)),
                      pl.BlockSpec((1, D), lambda i, pos, seg: (0, 0))],
            out_specs=pl.BlockSpec(memory_space=pl.ANY),
            scratch_shapes=[pltpu.VMEM((2, bm, D), F32), pltpu.VMEM((ZERO_ROWS, D), F32),
                            pltpu.SemaphoreType.DMA((3,))],
        ),
        out_shape=jax.ShapeDtypeStruct((n_rows, D), F32),
        compiler_params=_params("arbitrary"),
        name="moe_dispatch",
    )(pos, seg, h, g)


def _expert_body(blk_ref, x_ref, wg_ref, wu_ref, wd_ref, o_ref, xb_ref, *, n_blocks):
    j = pl.program_id(0)

    @pl.when(pl.program_id(1) == 0)
    def _():
        o_ref[...] = jnp.zeros_like(o_ref)

    @pl.when(j < blk_ref[n_blocks])
    def _():
        @pl.when(pl.program_id(1) == 0)
        def _():
            xb_ref[...] = x_ref[...].astype(BF16)

        xb = xb_ref[...]
        hid = jax.nn.silu(_dot(xb, wg_ref[...])) * _dot(xb, wu_ref[...])
        o_ref[...] += _dot(hid.astype(BF16), wd_ref[...])


def _expert_ffn(blk, xs, wg, wu, wd, *, be):
    n_rows, D = xs.shape
    F = wg.shape[2]
    n_blocks = n_rows // be
    fc = _largest_divisor(F, (512, 256, 128))
    nc = F // fc
    body = functools.partial(_expert_body, n_blocks=n_blocks)

    def row_blk(j, blk):
        return jnp.minimum(j, blk[n_blocks] - 1)

    def chunk(j, c, blk):
        return jnp.where(j < blk[n_blocks], c, nc - 1)

    return pl.pallas_call(
        body,
        grid_spec=pltpu.PrefetchScalarGridSpec(
            num_scalar_prefetch=1,
            grid=(n_blocks, nc),
            in_specs=[
                pl.BlockSpec((be, D), lambda j, c, blk: (row_blk(j, blk), 0)),
                pl.BlockSpec((None, D, fc), lambda j, c, blk: (blk[row_blk(j, blk)], 0, chunk(j, c, blk))),
                pl.BlockSpec((None, D, fc), lambda j, c, blk: (blk[row_blk(j, blk)], 0, chunk(j, c, blk))),
                pl.BlockSpec((None, fc, D), lambda j, c, blk: (blk[row_blk(j, blk)], chunk(j, c, blk), 0)),
            ],
            out_specs=pl.BlockSpec((be, D), lambda j, c, blk: (j, 0)),
            scratch_shapes=[pltpu.VMEM((be, D), BF16)],
        ),
        out_shape=jax.ShapeDtypeStruct((n_rows, D), F32),
        compiler_params=_params("arbitrary", "arbitrary"),
        name="moe_experts",
    )(blk, xs, wg, wu, wd)


def _combine_body(pos_ref, h_ref, route_ref, g_ref, y_hbm, o_ref, y_ref, sems, *, bm, T):
    i = pl.program_id(0)
    n_steps = pl.num_programs(0)
    slot = i % 2

    def start_gathers(step, s):
        def row_copy(r, k):
            src = pos_ref[k * T + step * bm + r]
            return pltpu.make_async_copy(y_hbm.at[pl.ds(src, 1), :], y_ref.at[s, k, pl.ds(r, 1), :], sems.at[s])

        def issue(grp, c):
            for u in range(ROW_DMA_UNROLL):
                row_copy(grp * ROW_DMA_UNROLL + u, 0).start(priority=u % 2)
                row_copy(grp * ROW_DMA_UNROLL + u, 1).start(priority=(u + 1) % 2)
            return c

        lax.fori_loop(0, bm // ROW_DMA_UNROLL, issue, 0)

    @pl.when(i == 0)
    def _():
        start_gathers(i, slot)

    @pl.when(i + 1 < n_steps)
    def _():
        start_gathers(i + 1, 1 - slot)

    for k in range(TOP_K):
        pltpu.make_async_copy(y_hbm.at[pl.ds(0, bm), :], y_ref.at[slot, k], sems.at[slot]).wait()

    route = route_ref[...]
    w1 = route[:, ROUTE_W1:ROUTE_W1 + 1]
    w2 = route[:, ROUTE_W2:ROUTE_W2 + 1]
    h = h_ref[...] + (w1 * y_ref[slot, 0] + w2 * y_ref[slot, 1])
    o_ref[...] = _rmsnorm(h, g_ref[...])


def _combine(pos, h, route, g, y, *, B, L):
    T, D = h.shape
    S = L - N_META
    bm = _largest_divisor(L, tuple(b for b in range(512, 7, -8)))
    blocks_per_seq = L // bm
    assert pl.cdiv(S, bm) == blocks_per_seq
    body = functools.partial(_combine_body, bm=bm, T=T)
    return pl.pallas_call(
        body,
        grid_spec=pltpu.PrefetchScalarGridSpec(
            num_scalar_prefetch=1,
            grid=(T // bm,),
            in_specs=[pl.BlockSpec((bm, D), lambda i, pos: (i, 0)),
                      pl.BlockSpec((bm, LANES), lambda i, pos: (i, 0)),
                      pl.BlockSpec((1, D), lambda i, pos: (0, 0)),
                      pl.BlockSpec(memory_space=pl.ANY)],
            out_specs=pl.BlockSpec((None, bm, D), lambda i, pos: (i // blocks_per_seq, i % blocks_per_seq, 0)),
            scratch_shapes=[pltpu.VMEM((2, TOP_K, bm, D), F32), pltpu.SemaphoreType.DMA((2,))],
        ),
        out_shape=jax.ShapeDtypeStruct((B, S, D), F32),
        compiler_params=_params("arbitrary"),
        name="moe_combine",
    )(pos, h, route, g, y)


def _rope_tables(B, L):
    half = HEAD_DIM // 2
    inv = ROPE_THETA ** (-2.0 * jnp.arange(half, dtype=F32) / HEAD_DIM)
    pos = jnp.concatenate([jnp.arange(N_META, L), jnp.arange(N_META)]).astype(F32)
    ang = pos[:, None] * inv[None, :]
    cs = jnp.tile(jnp.stack([jnp.cos(ang), jnp.sin(ang)]), (1, 1, LANES // half))
    cs = jnp.stack([cs * (HEAD_DIM ** -0.5 * LOG2_E), cs])
    return jnp.broadcast_to(cs[:, :, None], (2, 2, B, L, LANES)).reshape(2, 2, B * L, LANES)


def _pair_rotary_halves(w_qk, n_heads):
    D = w_qk.shape[0]
    w = w_qk.reshape(D, 2, n_heads // 2, 2, 2, 2, HEAD_DIM // 2)
    return w.transpose(0, 1, 2, 5, 3, 4, 6).reshape(D, -1)


def _token_mixer(h, layer, B, L, w_in, b_gate, lam, subln_g, pool_w, pool_scale, w_ba, w_bp, w_out, norm_g, rope_cs,
                 routing=None):
    A, P = w_ba.shape[0], w_bp.shape[0]
    n_heads = A // (2 * HEAD_DIM)
    lambda_init = 0.8 - 0.6 * math.exp(-0.3 * layer)
    w_qk = _pair_rotary_halves(w_in[:, :2 * A], n_heads).astype(BF16)
    if isinstance(h, tuple):
        z, h = _inproj_first(h[0], h[1], norm_g, w_qk, w_in.astype(BF16), rope_cs)
    else:
        z = _inproj(h, norm_g, w_qk, w_in.astype(BF16), rope_cs)
    T, D = h.shape
    z3 = z.reshape(B, L, z.shape[1])
    a = _attention(z3, lam, subln_g, n_heads=n_heads, lambda_init=lambda_init)
    p = _pool(z3, pool_w, pool_scale, col_block=(3 * A) // P)
    return _mix(a.reshape(T, A), p.reshape(T, P), z, b_gate, w_ba, w_bp, w_out, h,
                gate_col_block=(3 * A + P) // (2 * D), routing=routing)


def _router_hi_lo(router):
    D, E = router.shape
    r_pad = jnp.zeros((D, LANES), F32).at[:, :E].set(router)
    r_hi = r_pad.astype(BF16)
    r_lo = (r_pad - r_hi.astype(F32)).astype(BF16)
    return jnp.concatenate([r_hi, r_lo], axis=1)


def _moe_and_final_norm(h, route, route_t, cnt, norm_g, E, wg, wu, wd, final_g, *, B, L):
    T, D = h.shape
    be = 1024 if T >= 8192 else 64

    counts = cnt[0, :E].astype(jnp.int32)
    n_blk = (counts + be - 1) // be
    seg_end = jnp.cumsum(n_blk * be)
    seg_start = seg_end - n_blk * be
    n_blocks = (TOP_K * T + E * (be - 1)) // be
    blk_end = jnp.cumsum(n_blk)
    blk_expert = jnp.minimum(jnp.sum(jnp.arange(n_blocks)[:, None] >= blk_end[None, :], axis=1), E - 1)
    blk = jnp.concatenate([blk_expert.astype(jnp.int32), blk_end[-1:].astype(jnp.int32)])
    seg = jnp.concatenate([seg_end, n_blk]).astype(jnp.int32)

    def slot_of(expert_f, rank_f):
        start = jnp.sum(jnp.where(expert_f[:, None] == jnp.arange(E, dtype=F32)[None, :], seg_start[None, :], 0), axis=1)
        return start + rank_f.astype(jnp.int32)

    pos = jnp.concatenate([slot_of(route_t[ROUTE_I1], route_t[ROUTE_R1]), slot_of(route_t[ROUTE_I2], route_t[ROUTE_R2])])

    xs = _dispatch(pos, seg, h, norm_g, n_rows=n_blocks * be, n_experts=E, be=be)
    y = _expert_ffn(blk, xs, wg, wu, wd, be=be)
    return _combine(pos, h, route, final_g, y, B=B, L=L)


def kernel(x, meta_tokens, norm_mix, w_in, b_gate, lambda_q1, lambda_k1, lambda_q2, lambda_k2, subln, pool_w, pool_scale, w_branch_attn, w_branch_pool, w_out, norm_ffn, dense_w_gate, dense_w_up, dense_w_down, router, moe_w_gate, moe_w_up, moe_w_down, norm_final):
    B, S, D = x.shape
    depth = w_in.shape[0]
    assert depth == 2, "the final RMSNorm is fused into the routed layer, which must be the last one"
    L = N_META + S
    T = B * L
    h = (x, meta_tokens.astype(x.dtype))
    rope = _rope_tables(B, L)
    row = lambda v: v.reshape(1, -1)

    for i in range(depth):
        lam = jnp.stack([lambda_q1[i], lambda_k1[i], lambda_q2[i], lambda_k2[i]])
        j = i // 2
        routed = i % 2 == 1
        n_experts = router.shape[2]
        mixed = _token_mixer(h, i, B, L, w_in[i], row(b_gate[i]), lam, row(subln[i]),
                             pool_w[i].astype(BF16), row(pool_scale[i]), w_branch_attn[i].astype(BF16),
                             w_branch_pool[i].astype(BF16), w_out[i].astype(BF16), row(norm_mix[i]), rope,
                             routing=(row(norm_ffn[i]), _router_hi_lo(router[j]), n_experts) if routed else None)
        if not routed:
            h = _dense_ffn(mixed, row(norm_ffn[i]), dense_w_gate[j].astype(BF16), dense_w_up[j].astype(BF16),
                           dense_w_down[j].astype(BF16))
        else:
            h, route, route_t, cnt = mixed
            h = _moe_and_final_norm(h, route, route_t, cnt, row(norm_ffn[i]), n_experts, moe_w_gate[j].astype(BF16),
                                    moe_w_up[j].astype(BF16), moe_w_down[j].astype(BF16), row(norm_final),
                                    B=B, L=L)
    return h
```

```python
import functools
import math

import jax
import jax.numpy as jnp
from jax import lax
from jax.experimental import pallas as pl
from jax.experimental.pallas import tpu as pltpu

N_META = 16
HEAD_DIM = 64
POOL_WINDOWS = (2, 4, 8, 16)
TOP_K = 2
ROPE_THETA = 10000.0
EPS = 1e-5

LANES = 128
SUBLANES_BF16 = 16
VMEM_CAP_BYTES = 64 * 1024 * 1024
VMEM_LIMIT_BYTES = VMEM_CAP_BYTES - 6 * 1024 * 1024

NEG_BIG = -1e30
LOG2_E = 1.4426950408889634
POOL_PAD = max(POOL_WINDOWS)
MXU_DEPTH = 256
POOL_BLOCK_ROWS = MXU_DEPTH - 2 * POOL_PAD

F32 = jnp.float32
BF16 = jnp.bfloat16


def _largest_divisor(n, candidates):
    for c in candidates:
        if c <= n and n % c == 0:
            return c
    raise ValueError(f"no block size in {candidates} divides {n}")


def _params(*semantics, vmem_limit_bytes=VMEM_LIMIT_BYTES):
    return pltpu.CompilerParams(dimension_semantics=semantics, vmem_limit_bytes=vmem_limit_bytes)


def _rmsnorm(x, g):
    ms = jnp.mean(x * x, axis=-1, keepdims=True)
    return x * lax.rsqrt(ms + EPS) * g


def _dot(a, b):
    return jnp.dot(a, b, preferred_element_type=F32)


def _inproj_body(x_ref, g_ref, wqk_ref, wr_ref, cs_ref, o_ref, xn_ref, *, n_qk):
    _inproj_step(lambda: x_ref[...], g_ref, wqk_ref, wr_ref, cs_ref, o_ref, xn_ref, n_qk)


def _inproj_first_body(x_ref, meta_ref, g_ref, wqk_ref, wr_ref, cs_ref, o_ref, h_ref, xn_ref, *, n_qk, blocks_per_seq):
    def assemble():
        h_ref[...] = x_ref[...]

        @pl.when(pl.program_id(0) % blocks_per_seq == blocks_per_seq - 1)
        def _():
            h_ref[pl.ds(h_ref.shape[0] - N_META, N_META), :] = meta_ref[...]

        return h_ref[...]

    _inproj_step(assemble, g_ref, wqk_ref, wr_ref, cs_ref, o_ref, xn_ref, n_qk)


def _inproj_step(load_rows, g_ref, wqk_ref, wr_ref, cs_ref, o_ref, xn_ref, n_qk):
    j = pl.program_id(1)

    @pl.when(j == 0)
    def _():
        xn_ref[...] = _rmsnorm(load_rows(), g_ref[...]).astype(BF16)
        acc = _dot(xn_ref[...], wqk_ref[...])
        for c in range(0, n_qk, 2 * LANES):
            table = 0 if c < n_qk // 2 else 1
            cos, sin = cs_ref[table, 0], cs_ref[table, 1]
            t1 = acc[:, c:c + LANES]
            t2 = acc[:, c + LANES:c + 2 * LANES]
            o_ref[:, c:c + LANES] = (t1 * cos - t2 * sin).astype(o_ref.dtype)
            o_ref[:, c + LANES:c + 2 * LANES] = (t2 * cos + t1 * sin).astype(o_ref.dtype)

    @pl.when(j > 0)
    def _():
        o_ref[...] = _dot(xn_ref[...], wr_ref[...]).astype(o_ref.dtype)


def _inproj(h, g, w_qk, w_all, rope_cs):
    T, D = h.shape
    n_qk = w_qk.shape[1]
    n_rest = w_all.shape[1] - n_qk
    assert n_rest % n_qk == 0
    bm = _largest_divisor(T, (768, 512, 384, 256, 128, 64, 32, 16))
    body = functools.partial(_inproj_body, n_qk=n_qk)
    return pl.pallas_call(
        body,
        grid=(T // bm, 1 + n_rest // n_qk),
        in_specs=[
            pl.BlockSpec((bm, D), lambda i, j: (i, 0)),
            pl.BlockSpec((1, D), lambda i, j: (0, 0)),
            pl.BlockSpec((D, n_qk), lambda i, j: (0, 0), pipeline_mode=pl.Buffered(1)),
            pl.BlockSpec((D, n_qk), lambda i, j: (0, jnp.maximum(j, 1))),
            pl.BlockSpec((2, 2, bm, LANES), lambda i, j: (0, 0, i, 0)),
        ],
        out_specs=pl.BlockSpec((bm, n_qk), lambda i, j: (i, j)),
        out_shape=jax.ShapeDtypeStruct((T, n_qk + n_rest), BF16),
        scratch_shapes=[pltpu.VMEM((bm, D), BF16)],
        compiler_params=_params("parallel", "arbitrary"),
        name="inproj",
    )(h, g, w_qk, w_all, rope_cs)


def _inproj_first(x, meta, g, w_qk, w_all, rope_cs):
    B, S, D = x.shape
    L = S + N_META
    T = B * L
    n_qk = w_qk.shape[1]
    n_rest = w_all.shape[1] - n_qk
    assert n_rest % n_qk == 0
    bm = _largest_divisor(L, tuple(b for b in range(768, 15, -16)))
    blocks_per_seq = L // bm
    assert pl.cdiv(S, bm) == blocks_per_seq and blocks_per_seq * bm - S == N_META
    body = functools.partial(_inproj_first_body, n_qk=n_qk, blocks_per_seq=blocks_per_seq)
    return pl.pallas_call(
        body,
        grid=(T // bm, 1 + n_rest // n_qk),
        in_specs=[
            pl.BlockSpec((None, bm, D), lambda i, j: (i // blocks_per_seq, i % blocks_per_seq, 0)),
            pl.BlockSpec((N_META, D), lambda i, j: (0, 0)),
            pl.BlockSpec((1, D), lambda i, j: (0, 0)),
            pl.BlockSpec((D, n_qk), lambda i, j: (0, 0), pipeline_mode=pl.Buffered(1)),
            pl.BlockSpec((D, n_qk), lambda i, j: (0, jnp.maximum(j, 1))),
            pl.BlockSpec((2, 2, bm, LANES), lambda i, j: (0, 0, i, 0)),
        ],
        out_specs=[pl.BlockSpec((bm, n_qk), lambda i, j: (i, j)), pl.BlockSpec((bm, D), lambda i, j: (i, 0))],
        out_shape=[jax.ShapeDtypeStruct((T, n_qk + n_rest), BF16), jax.ShapeDtypeStruct((T, D), F32)],
        scratch_shapes=[pltpu.VMEM((bm, D), BF16)],
        compiler_params=_params("parallel", "arbitrary", vmem_limit_bytes=VMEM_CAP_BYTES - 2 * 1024 * 1024),
        name="inproj_first",
    )(x, meta, g, w_qk, w_all, rope_cs)


PAIR = 2 * LANES


def _attn_body(lam_ref, q_ref, k_ref, v_ref, sg_ref, o_ref, kp_ref, vp_ref, of_ref,
               *, L, l_main, l_keys, bq, lambda_init):
    lane = lax.broadcasted_iota(jnp.int32, (1, PAIR), 1)
    map_of_lane = (lane % HEAD_DIM) // (HEAD_DIM // 2)

    kp_ref[pl.ds(0, L), :] = k_ref[...]
    if l_keys > L:
        kp_ref[pl.ds(L, l_keys - L), :] = jnp.zeros((l_keys - L, PAIR), BF16)
    for hh in range(2):
        vp_ref[hh, pl.ds(0, L), pl.ds(0, LANES)] = v_ref[:, hh * LANES:(hh + 1) * LANES]
        vp_ref[hh, :, pl.ds(LANES, LANES)] = jnp.ones((l_keys, LANES), BF16)
        if l_keys > L:
            vp_ref[hh, pl.ds(L, l_keys - L), pl.ds(0, LANES)] = jnp.zeros((l_keys - L, LANES), BF16)

    lam = lam_ref[...]
    lam_full = (jnp.exp(jnp.sum(lam[0:1] * lam[1:2], axis=-1, keepdims=True))
                - jnp.exp(jnp.sum(lam[2:3] * lam[3:4], axis=-1, keepdims=True)) + lambda_init)

    tail_valid = (lax.broadcasted_iota(jnp.int32, (1, LANES), 1) + l_main) < L
    contract_last = (((1,), (1,)), ((), ()))
    zero = jnp.zeros((), BF16)
    q_all = q_ref[...]

    for hh in range(2):
        in_head = ((lane % LANES) // HEAD_DIM) == hh
        map_masks = (in_head & (map_of_lane == 0), in_head & (map_of_lane == 1))
        for i in range(L // bq):
            qb = q_all[i * bq:(i + 1) * bq]
            q_maps = jnp.concatenate([jnp.where(m, qb, zero) for m in map_masks], axis=0)
            s = lax.dot_general(q_maps, kp_ref[...], contract_last, preferred_element_type=F32)
            s_main = s[:, :l_main]
            m = jnp.max(s_main, axis=-1, keepdims=True)
            if l_keys > l_main:
                s_tail = jnp.where(tail_valid, s[:, l_main:], NEG_BIG)
                m = jnp.maximum(m, jnp.max(s_tail, axis=-1, keepdims=True))
            acc = _dot(jnp.exp2(s_main - m).astype(BF16), vp_ref[hh, pl.ds(0, l_main), :])
            if l_keys > l_main:
                acc = acc + _dot(jnp.exp2(s_tail - m).astype(BF16), vp_ref[hh, pl.ds(l_main, l_keys - l_main), :])
            a0, a1 = acc[:bq], acc[bq:]
            o = a0[:, :LANES] * (1.0 / a0[:, LANES:LANES + 1]) - a1[:, :LANES] * (lam_full / a1[:, LANES:LANES + 1])
            of_ref[pl.ds(i * bq, bq), pl.ds(hh * LANES, LANES)] = _rmsnorm(o, sg_ref[...]) * (1.0 - lambda_init)
    o_ref[...] = of_ref[...].astype(o_ref.dtype)


def _attention(z3, lam, subln_g, *, n_heads, lambda_init):
    B, L, _ = z3.shape
    l_main = (L // LANES) * LANES
    l_keys = l_main if l_main == L else l_main + LANES
    bq = _largest_divisor(L, tuple(b for b in range(400, 7, -8)))
    n_pairs = n_heads // 2
    body = functools.partial(_attn_body, L=L, l_main=l_main, l_keys=l_keys, bq=bq, lambda_init=lambda_init)
    pair = lambda off: pl.BlockSpec((None, L, PAIR), lambda b, h: (b, 0, off + h))
    const = lambda shape: pl.BlockSpec(shape, lambda b, h: (0,) * len(shape))
    return pl.pallas_call(
        body,
        grid=(B, n_pairs),
        in_specs=[const(lam.shape), pair(0), pair(n_pairs), pair(2 * n_pairs), const(subln_g.shape)],
        out_specs=pl.BlockSpec((None, L, PAIR), lambda b, h: (b, 0, h)),
        out_shape=jax.ShapeDtypeStruct((B, L, n_heads * LANES), BF16),
        scratch_shapes=[pltpu.VMEM((l_keys, PAIR), BF16), pltpu.VMEM((2, l_keys, PAIR), BF16),
                        pltpu.VMEM((L, PAIR), F32)],
        compiler_params=_params("parallel", "parallel"),
        name="diff_attention",
    )(lam, z3, z3, z3, subln_g)


def _pool_body(u_ref, band_ref, pw_ref, ps_ref, o_ref, pad_ref, *, L, cg, br):
    S = L - N_META
    P = pad_ref.shape[1]
    pad_ref[pl.ds(0, POOL_PAD), :] = jnp.zeros((POOL_PAD, P), pad_ref.dtype)
    pad_ref[pl.ds(POOL_PAD + L, POOL_PAD), :] = jnp.zeros((POOL_PAD, P), pad_ref.dtype)
    pad_ref[pl.ds(POOL_PAD, N_META), :] = u_ref[pl.ds(S, N_META), :]
    pad_ref[pl.ds(POOL_PAD + N_META, S), :] = u_ref[pl.ds(0, S), :]
    for start in range(0, L, br):
        rows = min(br, L - start)
        t = lax.broadcasted_iota(jnp.int32, (rows, 1), 0) + start
        for g, w in enumerate(POOL_WINDOWS):
            cols = slice(g * cg, (g + 1) * cg)
            win = _dot(band_ref[g, pl.ds(0, rows), pl.ds(0, rows + 2 * POOL_PAD)],
                       pad_ref[pl.ds(start, rows + 2 * POOL_PAD), cols])
            cnt = (jnp.minimum(t + w // 2, L) - jnp.maximum(t - w // 2, 0)).astype(F32)
            m = win * (1.0 / cnt) - pad_ref[pl.ds(start + POOL_PAD, rows), cols].astype(F32)
            y = (_dot(m.astype(BF16), pw_ref[g]) * ps_ref[:, cols]).astype(o_ref.dtype)
            if start == 0:
                o_ref[pl.ds(S, N_META), cols] = y[:N_META]
                o_ref[pl.ds(0, rows - N_META), cols] = y[N_META:]
            else:
                o_ref[pl.ds(start - N_META, rows), cols] = y


def _pool(z3, pool_w, pool_scale, *, col_block):
    B, L, _ = z3.shape
    G, cg, _ = pool_w.shape
    P = G * cg
    br = min(L, POOL_BLOCK_ROWS)
    assert L % SUBLANES_BF16 == 0 and N_META % SUBLANES_BF16 == 0
    r = lax.broadcasted_iota(jnp.int32, (br, br + 2 * POOL_PAD), 0)
    c = lax.broadcasted_iota(jnp.int32, (br, br + 2 * POOL_PAD), 1) - POOL_PAD
    band = jnp.stack([((c >= r - w // 2) & (c < r + w // 2)).astype(BF16) for w in POOL_WINDOWS])
    body = functools.partial(_pool_body, L=L, cg=cg, br=br)
    return pl.pallas_call(
        body,
        grid=(B,),
        in_specs=[
            pl.BlockSpec((None, L, P), lambda b: (b, 0, col_block)),
            pl.BlockSpec(band.shape, lambda b: (0, 0, 0)),
            pl.BlockSpec((G, cg, cg), lambda b: (0, 0, 0)),
            pl.BlockSpec((1, P), lambda b: (0, 0)),
        ],
        out_specs=pl.BlockSpec((None, L, P), lambda b: (b, 0, 0)),
        out_shape=jax.ShapeDtypeStruct((B, L, P), BF16),
        scratch_shapes=[pltpu.VMEM((L + 2 * POOL_PAD, P), BF16)],
        compiler_params=_params("parallel"),
        name="pool_mixer",
    )(z3, band, pool_w, pool_scale)


def _mix_rows(a_ref, p_ref, gt_ref, b_ref, wa_ref, wp_ref, wo_ref, h_ref, D):
    gates = jax.nn.sigmoid(gt_ref[...].astype(F32) + b_ref[...])
    y = gates[:, :D] * _dot(a_ref[...], wa_ref[...]) + gates[:, D:] * _dot(p_ref[...], wp_ref[...])
    return h_ref[...] + _dot(y.astype(BF16), wo_ref[...])


def _mix_body(a_ref, p_ref, gt_ref, b_ref, wa_ref, wp_ref, wo_ref, h_ref, o_ref, *, D):
    o_ref[...] = _mix_rows(a_ref, p_ref, gt_ref, b_ref, wa_ref, wp_ref, wo_ref, h_ref, D)


def _mix_route_body(a_ref, p_ref, gt_ref, b_ref, wa_ref, wp_ref, wo_ref, h_ref, g_ref, r_ref, tri_ref,
                    o_ref, route_ref, route_t_ref, cnt_ref, carry_ref, *, D, n_experts):
    h = _mix_rows(a_ref, p_ref, gt_ref, b_ref, wa_ref, wp_ref, wo_ref, h_ref, D)
    o_ref[...] = h
    _route_rows(h, g_ref, r_ref, tri_ref, route_ref, route_t_ref, cnt_ref, carry_ref, n_experts)


def _mix(a, p, z, b_gate, wa, wp, wo, h, *, gate_col_block, routing=None):
    T, D = h.shape
    A, P = a.shape[1], p.shape[1]
    bm = _largest_divisor(T, (384, 256, 128, 64, 32, 16))
    const = lambda shape: pl.BlockSpec(shape, lambda i: (0, 0), pipeline_mode=pl.Buffered(1))
    rows = lambda width: pl.BlockSpec((bm, width), lambda i: (i, 0))
    in_specs = [rows(A), rows(P), pl.BlockSpec((bm, 2 * D), lambda i: (i, gate_col_block)),
                const((1, 2 * D)), const((A, D)), const((P, D)), const((D, D)), rows(D)]
    if routing is None:
        return pl.pallas_call(
            functools.partial(_mix_body, D=D),
            grid=(T // bm,),
            in_specs=in_specs,
            out_specs=rows(D),
            out_shape=jax.ShapeDtypeStruct((T, D), F32),
            compiler_params=_params("parallel"),
            name="mix_out",
        )(a, p, z, b_gate, wa, wp, wo, h)
    g_ffn, r_hi_lo, n_experts = routing
    tri = (lax.broadcasted_iota(jnp.int32, (bm, bm), 0) > lax.broadcasted_iota(jnp.int32, (bm, bm), 1)).astype(BF16)
    return pl.pallas_call(
        functools.partial(_mix_route_body, D=D, n_experts=n_experts),
        grid=(T // bm,),
        in_specs=in_specs + [const((1, D)), const((D, 2 * LANES)), const((bm, bm))],
        out_specs=[rows(D), rows(LANES), pl.BlockSpec((8, bm), lambda i: (0, i)),
                   pl.BlockSpec((8, LANES), lambda i: (0, 0))],
        out_shape=[jax.ShapeDtypeStruct((T, D), F32), jax.ShapeDtypeStruct((T, LANES), F32),
                   jax.ShapeDtypeStruct((8, T), F32), jax.ShapeDtypeStruct((8, LANES), F32)],
        scratch_shapes=[pltpu.VMEM((1, LANES), F32)],
        compiler_params=_params("arbitrary"),
        name="mix_out_route",
    )(a, p, z, b_gate, wa, wp, wo, h, g_ffn, r_hi_lo, tri)


def _ffn_body(x_ref, g_ref, wg_ref, wu_ref, wd_ref, o_ref, xn_ref):
    @pl.when(pl.program_id(1) == 0)
    def _():
        x = x_ref[...]
        xn_ref[...] = _rmsnorm(x, g_ref[...]).astype(BF16)
        o_ref[...] = x

    xn = xn_ref[...]
    hid = jax.nn.silu(_dot(xn, wg_ref[...])) * _dot(xn, wu_ref[...])
    o_ref[...] += _dot(hid.astype(BF16), wd_ref[...])


def _dense_ffn(h, g, wg, wu, wd):
    T, D = h.shape
    F = wg.shape[1]
    bm = _largest_divisor(T, (768, 512, 384, 256, 128, 64, 32, 16))
    fc = _largest_divisor(F, (512, 256, 128))
    return pl.pallas_call(
        _ffn_body,
        grid=(T // bm, F // fc),
        in_specs=[
            pl.BlockSpec((bm, D), lambda i, c: (i, 0)),
            pl.BlockSpec((1, D), lambda i, c: (0, 0)),
            pl.BlockSpec((D, fc), lambda i, c: (0, c)),
            pl.BlockSpec((D, fc), lambda i, c: (0, c)),
            pl.BlockSpec((fc, D), lambda i, c: (c, 0)),
        ],
        out_specs=pl.BlockSpec((bm, D), lambda i, c: (i, 0)),
        out_shape=jax.ShapeDtypeStruct((T, D), F32),
        scratch_shapes=[pltpu.VMEM((bm, D), BF16)],
        compiler_params=_params("parallel", "arbitrary"),
        name="dense_ffn",
    )(h, g, wg, wu, wd)


ROUTE_I1, ROUTE_I2, ROUTE_W1, ROUTE_W2, ROUTE_R1, ROUTE_R2 = range(6)


def _route_rows(x, g_ref, r_ref, tri_ref, route_ref, route_t_ref, cnt_ref, carry_ref, n_experts):
    @pl.when(pl.program_id(0) == 0)
    def _():
        carry_ref[...] = jnp.zeros_like(carry_ref)

    xn = _rmsnorm(x, g_ref[...])
    x_hi = xn.astype(BF16)
    x_lo = (xn - x_hi.astype(F32)).astype(BF16)
    hi_both = _dot(x_hi, r_ref[...])
    logits = hi_both[:, :LANES] + (hi_both[:, LANES:] + _dot(x_lo, r_ref[:, pl.ds(0, LANES)]))

    lane = lax.broadcasted_iota(jnp.int32, (1, LANES), 1)
    lane_f = lane.astype(F32)
    lg = jnp.where(lane < n_experts, logits, NEG_BIG)
    v1 = jnp.max(lg, axis=-1, keepdims=True)
    i1 = jnp.min(jnp.where(lg == v1, lane_f, float(LANES)), axis=-1, keepdims=True)
    lg2 = jnp.where(lane_f == i1, NEG_BIG, lg)
    v2 = jnp.max(lg2, axis=-1, keepdims=True)
    i2 = jnp.min(jnp.where(lg2 == v2, lane_f, float(LANES)), axis=-1, keepdims=True)
    e2 = jnp.exp(v2 - v1)
    w1 = 1.0 / (1.0 + e2)
    w2 = e2 / (1.0 + e2)

    sel1 = lane_f == i1
    sel2 = lane_f == i2
    onehot = jnp.where(sel1 | sel2, 1.0, 0.0)
    rank = _dot(tri_ref[...], onehot.astype(BF16)) + carry_ref[...]
    r1 = jnp.sum(jnp.where(sel1, rank, 0.0), axis=-1, keepdims=True)
    r2 = jnp.sum(jnp.where(sel2, rank, 0.0), axis=-1, keepdims=True)
    carry_ref[...] += jnp.sum(onehot, axis=0, keepdims=True)

    route = jnp.zeros(route_ref.shape, F32)
    for col, val in ((ROUTE_I1, i1), (ROUTE_I2, i2), (ROUTE_W1, w1), (ROUTE_W2, w2), (ROUTE_R1, r1), (ROUTE_R2, r2)):
        route = jnp.where(lane == col, val, route)
    route_ref[...] = route
    route_t_ref[...] = jnp.transpose(route)[:route_t_ref.shape[0]]
    cnt_ref[...] = jnp.broadcast_to(carry_ref[...], cnt_ref.shape)


ZERO_ROWS = 64
ROW_DMA_UNROLL = 8


def _dispatch_body(pos_ref, seg_ref, x_ref, g_ref, xs_hbm, xn_ref, zero_ref, sems, *, bm, T, n_experts, be, n_rows):
    i = pl.program_id(0)
    n_steps = pl.num_programs(0)
    slot = i % 2

    def zero_copy(first_row, c):
        row = pl.multiple_of(first_row + c * ZERO_ROWS, ZERO_ROWS)
        return pltpu.make_async_copy(zero_ref, xs_hbm.at[pl.ds(row, ZERO_ROWS), :], sems.at[2])

    def zero_block(first_row):
        for c in range(be // ZERO_ROWS):
            zero_copy(first_row, c).start()
        for c in range(be // ZERO_ROWS):
            zero_copy(first_row, c).wait()

    @pl.when(i == 0)
    def _():
        zero_ref[...] = jnp.zeros_like(zero_ref)
        for e in range(n_experts):
            @pl.when(seg_ref[n_experts + e] > 0)
            def _():
                zero_block(seg_ref[e] - be)

        def clear_unused(j, c):
            zero_block(j * be)
            return c

        lax.fori_loop(seg_ref[n_experts - 1] // be, n_rows // be, clear_unused, 0)

    def drain(s):
        for _ in range(TOP_K):
            pltpu.make_async_copy(xn_ref.at[s], xs_hbm.at[pl.ds(0, bm), :], sems.at[s]).wait()

    @pl.when(i >= 2)
    def _():
        drain(slot)

    xn_ref[slot] = _rmsnorm(x_ref[...], g_ref[...])

    def row_copy(r, k):
        dst = pos_ref[k * T + i * bm + r]
        return pltpu.make_async_copy(xn_ref.at[slot, pl.ds(r, 1), :], xs_hbm.at[pl.ds(dst, 1), :], sems.at[slot])

    def issue(grp, c):
        for u in range(ROW_DMA_UNROLL):
            row_copy(grp * ROW_DMA_UNROLL + u, 0).start(priority=u % 2)
            row_copy(grp * ROW_DMA_UNROLL + u, 1).start(priority=(u + 1) % 2)
        return c

    lax.fori_loop(0, bm // ROW_DMA_UNROLL, issue, 0)

    @pl.when(i == n_steps - 1)
    def _():
        drain(slot)

        @pl.when(i >= 1)
        def _():
            drain(1 - slot)


def _dispatch(pos, seg, h, g, *, n_rows, n_experts, be):
    T, D = h.shape
    bm = _largest_divisor(T, (768, 512, 256, 128, 64, 32, 16))
    body = functools.partial(_dispatch_body, bm=bm, T=T, n_experts=n_experts, be=be, n_rows=n_rows)
    return pl.pallas_call(
        body,
        grid_spec=pltpu.PrefetchScalarGridSpec(
            num_scalar_prefetch=2,
            grid=(T // bm,),
            in_specs=[pl.BlockSpec((bm, D), lambda i, pos, seg: (i, 0)),
                      pl.BlockSpec((1, D), lambda i, pos, seg: (0, 0))],
            out_specs=pl.BlockSpec(memory_space=pl.ANY),
            scratch_shapes=[pltpu.VMEM((2, bm, D), F32), pltpu.VMEM((ZERO_ROWS, D), F32),
                            pltpu.SemaphoreType.DMA((3,))],
        ),
        out_shape=jax.ShapeDtypeStruct((n_rows, D), F32),
        compiler_params=_params("arbitrary"),
        name="moe_dispatch",
    )(pos, seg, h, g)


def _expert_body(blk_ref, x_ref, wg_ref, wu_ref, wd_ref, o_ref, xb_ref, *, n_blocks):
    j = pl.program_id(0)

    @pl.when(pl.program_id(1) == 0)
    def _():
        o_ref[...] = jnp.zeros_like(o_ref)

    @pl.when(j < blk_ref[n_blocks])
    def _():
        @pl.when(pl.program_id(1) == 0)
        def _():
            xb_ref[...] = x_ref[...].astype(BF16)

        xb = xb_ref[...]
        hid = jax.nn.silu(_dot(xb, wg_ref[...])) * _dot(xb, wu_ref[...])
        o_ref[...] += _dot(hid.astype(BF16), wd_ref[...])


def _expert_ffn(blk, xs, wg, wu, wd, *, be):
    n_rows, D = xs.shape
    F = wg.shape[2]
    n_blocks = n_rows // be
    fc = _largest_divisor(F, (512, 256, 128))
    nc = F // fc
    body = functools.partial(_expert_body, n_blocks=n_blocks)

    def row_blk(j, blk):
        return jnp.minimum(j, blk[n_blocks] - 1)

    def chunk(j, c, blk):
        return jnp.where(j < blk[n_blocks], c, nc - 1)

    return pl.pallas_call(
        body,
        grid_spec=pltpu.PrefetchScalarGridSpec(
            num_scalar_prefetch=1,
            grid=(n_blocks, nc),
            in_specs=[
                pl.BlockSpec((be, D), lambda j, c, blk: (row_blk(j, blk), 0)),
                pl.BlockSpec((None, D, fc), lambda j, c, blk: (blk[row_blk(j, blk)], 0, chunk(j, c, blk))),
                pl.BlockSpec((None, D, fc), lambda j, c, blk: (blk[row_blk(j, blk)], 0, chunk(j, c, blk))),
                pl.BlockSpec((None, fc, D), lambda j, c, blk: (blk[row_blk(j, blk)], chunk(j, c, blk), 0)),
            ],
            out_specs=pl.BlockSpec((be, D), lambda j, c, blk: (j, 0)),
            scratch_shapes=[pltpu.VMEM((be, D), BF16)],
        ),
        out_shape=jax.ShapeDtypeStruct((n_rows, D), F32),
        compiler_params=_params("arbitrary", "arbitrary"),
        name="moe_experts",
    )(blk, xs, wg, wu, wd)


def _combine_body(pos_ref, h_ref, route_ref, g_ref, y_hbm, o_ref, y_ref, sems, *, bm, T):
    i = pl.program_id(0)
    n_steps = pl.num_programs(0)
    slot = i % 2

    def start_gathers(step, s):
        def row_copy(r, k):
            src = pos_ref[k * T + step * bm + r]
            return pltpu.make_async_copy(y_hbm.at[pl.ds(src, 1), :], y_ref.at[s, k, pl.ds(r, 1), :], sems.at[s])

        def issue(grp, c):
            for u in range(ROW_DMA_UNROLL):
                row_copy(grp * ROW_DMA_UNROLL + u, 0).start(priority=u % 2)
                row_copy(grp * ROW_DMA_UNROLL + u, 1).start(priority=(u + 1) % 2)
            return c

        lax.fori_loop(0, bm // ROW_DMA_UNROLL, issue, 0)

    @pl.when(i == 0)
    def _():
        start_gathers(i, slot)

    @pl.when(i + 1 < n_steps)
    def _():
        start_gathers(i + 1, 1 - slot)

    for k in range(TOP_K):
        pltpu.make_async_copy(y_hbm.at[pl.ds(0, bm), :], y_ref.at[slot, k], sems.at[slot]).wait()

    route = route_ref[...]
    w1 = route[:, ROUTE_W1:ROUTE_W1 + 1]
    w2 = route[:, ROUTE_W2:ROUTE_W2 + 1]
    h = h_ref[...] + (w1 * y_ref[slot, 0] + w2 * y_ref[slot, 1])
    o_ref[...] = _rmsnorm(h, g_ref[...])


def _combine(pos, h, route, g, y, *, B, L):
    T, D = h.shape
    S = L - N_META
    bm = _largest_divisor(L, tuple(b for b in range(768, 7, -8)))
    blocks_per_seq = L // bm
    assert pl.cdiv(S, bm) == blocks_per_seq
    body = functools.partial(_combine_body, bm=bm, T=T)
    return pl.pallas_call(
        body,
        grid_spec=pltpu.PrefetchScalarGridSpec(
            num_scalar_prefetch=1,
            grid=(T // bm,),
            in_specs=[pl.BlockSpec((bm, D), lambda i, pos: (i, 0)),
                      pl.BlockSpec((bm, LANES), lambda i, pos: (i, 0)),
                      pl.BlockSpec((1, D), lambda i, pos: (0, 0)),
                      pl.BlockSpec(memory_space=pl.ANY)],
            out_specs=pl.BlockSpec((None, bm, D), lambda i, pos: (i // blocks_per_seq, i % blocks_per_seq, 0)),
            scratch_shapes=[pltpu.VMEM((2, TOP_K, bm, D), F32), pltpu.SemaphoreType.DMA((2,))],
        ),
        out_shape=jax.ShapeDtypeStruct((B, S, D), F32),
        compiler_params=_params("arbitrary"),
        name="moe_combine",
    )(pos, h, route, g, y)


def _rope_tables(B, L):
    half = HEAD_DIM // 2
    inv = ROPE_THETA ** (-2.0 * jnp.arange(half, dtype=F32) / HEAD_DIM)
    pos = jnp.concatenate([jnp.arange(N_META, L), jnp.arange(N_META)]).astype(F32)
    ang = pos[:, None] * inv[None, :]
    cs = jnp.tile(jnp.stack([jnp.cos(ang), jnp.sin(ang)]), (1, 1, LANES // half))
    cs = jnp.stack([cs * (HEAD_DIM ** -0.5 * LOG2_E), cs])
    return jnp.broadcast_to(cs[:, :, None], (2, 2, B, L, LANES)).reshape(2, 2, B * L, LANES)


def _pair_rotary_halves(w_qk, n_heads):
    D = w_qk.shape[0]
    w = w_qk.reshape(D, 2, n_heads // 2, 2, 2, 2, HEAD_DIM // 2)
    return w.transpose(0, 1, 2, 5, 3, 4, 6).reshape(D, -1)


def _token_mixer(h, layer, B, L, w_in, b_gate, lam, subln_g, pool_w, pool_scale, w_ba, w_bp, w_out, norm_g, rope_cs,
                 routing=None):
    A, P = w_ba.shape[0], w_bp.shape[0]
    n_heads = A // (2 * HEAD_DIM)
    lambda_init = 0.8 - 0.6 * math.exp(-0.3 * layer)
    w_qk = _pair_rotary_halves(w_in[:, :2 * A], n_heads).astype(BF16)
    if isinstance(h, tuple):
        z, h = _inproj_first(h[0], h[1], norm_g, w_qk, w_in.astype(BF16), rope_cs)
    else:
        z = _inproj(h, norm_g, w_qk, w_in.astype(BF16), rope_cs)
    T, D = h.shape
    z3 = z.reshape(B, L, z.shape[1])
    a = _attention(z3, lam, subln_g, n_heads=n_heads, lambda_init=lambda_init)
    p = _pool(z3, pool_w, pool_scale, col_block=(3 * A) // P)
    return _mix(a.reshape(T, A), p.reshape(T, P), z, b_gate, w_ba, w_bp, w_out, h,
                gate_col_block=(3 * A + P) // (2 * D), routing=routing)


def _router_hi_lo(router):
    D, E = router.shape
    r_pad = jnp.zeros((D, LANES), F32).at[:, :E].set(router)
    r_hi = r_pad.astype(BF16)
    r_lo = (r_pad - r_hi.astype(F32)).astype(BF16)
    return jnp.concatenate([r_hi, r_lo], axis=1)


def _moe_and_final_norm(h, route, route_t, cnt, norm_g, E, wg, wu, wd, final_g, *, B, L):
    T, D = h.shape
    be = 1024 if T >= 8192 else 64

    counts = cnt[0, :E].astype(jnp.int32)
    n_blk = (counts + be - 1) // be
    seg_end = jnp.cumsum(n_blk * be)
    seg_start = seg_end - n_blk * be
    n_blocks = (TOP_K * T + E * (be - 1)) // be
    blk_end = jnp.cumsum(n_blk)
    blk_expert = jnp.minimum(jnp.sum(jnp.arange(n_blocks)[:, None] >= blk_end[None, :], axis=1), E - 1)
    blk = jnp.concatenate([blk_expert.astype(jnp.int32), blk_end[-1:].astype(jnp.int32)])
    seg = jnp.concatenate([seg_end, n_blk]).astype(jnp.int32)

    def slot_of(expert_f, rank_f):
        start = jnp.sum(jnp.where(expert_f[:, None] == jnp.arange(E, dtype=F32)[None, :], seg_start[None, :], 0), axis=1)
        return start + rank_f.astype(jnp.int32)

    pos = jnp.concatenate([slot_of(route_t[ROUTE_I1], route_t[ROUTE_R1]), slot_of(route_t[ROUTE_I2], route_t[ROUTE_R2])])

    xs = _dispatch(pos, seg, h, norm_g, n_rows=n_blocks * be, n_experts=E, be=be)
    y = _expert_ffn(blk, xs, wg, wu, wd, be=be)
    return _combine(pos, h, route, final_g, y, B=B, L=L)


def kernel(x, meta_tokens, norm_mix, w_in, b_gate, lambda_q1, lambda_k1, lambda_q2, lambda_k2, subln, pool_w, pool_scale, w_branch_attn, w_branch_pool, w_out, norm_ffn, dense_w_gate, dense_w_up, dense_w_down, router, moe_w_gate, moe_w_up, moe_w_down, norm_final):
    B, S, D = x.shape
    depth = w_in.shape[0]
    assert depth == 2, "the final RMSNorm is fused into the routed layer, which must be the last one"
    L = N_META + S
    T = B * L
    h = (x, meta_tokens.astype(x.dtype))
    rope = _rope_tables(B, L)
    row = lambda v: v.reshape(1, -1)

    for i in range(depth):
        lam = jnp.stack([lambda_q1[i], lambda_k1[i], lambda_q2[i], lambda_k2[i]])
        j = i // 2
        routed = i % 2 == 1
        n_experts = router.shape[2]
        mixed = _token_mixer(h, i, B, L, w_in[i], row(b_gate[i]), lam, row(subln[i]),
                             pool_w[i].astype(BF16), row(pool_scale[i]), w_branch_attn[i].astype(BF16),
                             w_branch_pool[i].astype(BF16), w_out[i].astype(BF16), row(norm_mix[i]), rope,
                             routing=(row(norm_ffn[i]), _router_hi_lo(router[j]), n_experts) if routed else None)
        if not routed:
            h = _dense_ffn(mixed, row(norm_ffn[i]), dense_w_gate[j].astype(BF16), dense_w_up[j].astype(BF16),
                           dense_w_down[j].astype(BF16))
        else:
            h, route, route_t, cnt = mixed
            h = _moe_and_final_norm(h, route, route_t, cnt, row(norm_ffn[i]), n_experts, moe_w_gate[j].astype(BF16),
                                    moe_w_up[j].astype(BF16), moe_w_down[j].astype(BF16), row(norm_final),
                                    B=B, L=L)
    return h
```

```python
import functools
import math

import jax
import jax.numpy as jnp
from jax import lax
from jax.experimental import pallas as pl
from jax.experimental.pallas import tpu as pltpu

N_META = 16
HEAD_DIM = 64
POOL_WINDOWS = (2, 4, 8, 16)
TOP_K = 2
ROPE_THETA = 10000.0
EPS = 1e-5

LANES = 128
SUBLANES_BF16 = 16
VMEM_CAP_BYTES = 64 * 1024 * 1024
VMEM_LIMIT_BYTES = VMEM_CAP_BYTES - 6 * 1024 * 1024

NEG_BIG = -1e30
LOG2_E = 1.4426950408889634
POOL_PAD = max(POOL_WINDOWS)
MXU_DEPTH = 256
POOL_BLOCK_ROWS = MXU_DEPTH - 2 * POOL_PAD

F32 = jnp.float32
BF16 = jnp.bfloat16


def _largest_divisor(n, candidates):
    for c in candidates:
        if c <= n and n % c == 0:
            return c
    raise ValueError(f"no block size in {candidates} divides {n}")


def _params(*semantics, vmem_limit_bytes=VMEM_LIMIT_BYTES):
    return pltpu.CompilerParams(dimension_semantics=semantics, vmem_limit_bytes=vmem_limit_bytes)


def _rmsnorm(x, g):
    ms = jnp.mean(x * x, axis=-1, keepdims=True)
    return x * lax.rsqrt(ms + EPS) * g


def _dot(a, b):
    return jnp.dot(a, b, preferred_element_type=F32)


def _inproj_body(x_ref, g_ref, wqk_ref, wr_ref, cs_ref, o_ref, xn_ref, *, n_qk):
    _inproj_step(lambda: x_ref[...], g_ref, wqk_ref, wr_ref, cs_ref, o_ref, xn_ref, n_qk)


def _inproj_first_body(x_ref, meta_ref, g_ref, wqk_ref, wr_ref, cs_ref, o_ref, h_ref, xn_ref, *, n_qk, blocks_per_seq):
    def assemble():
        h_ref[...] = x_ref[...]

        @pl.when(pl.program_id(0) % blocks_per_seq == blocks_per_seq - 1)
        def _():
            h_ref[pl.ds(h_ref.shape[0] - N_META, N_META), :] = meta_ref[...]

        return h_ref[...]

    _inproj_step(assemble, g_ref, wqk_ref, wr_ref, cs_ref, o_ref, xn_ref, n_qk)


def _inproj_step(load_rows, g_ref, wqk_ref, wr_ref, cs_ref, o_ref, xn_ref, n_qk):
    j = pl.program_id(1)

    @pl.when(j == 0)
    def _():
        xn_ref[...] = _rmsnorm(load_rows(), g_ref[...]).astype(BF16)
        acc = _dot(xn_ref[...], wqk_ref[...])
        for c in range(0, n_qk, 2 * LANES):
            table = 0 if c < n_qk // 2 else 1
            cos, sin = cs_ref[table, 0], cs_ref[table, 1]
            t1 = acc[:, c:c + LANES]
            t2 = acc[:, c + LANES:c + 2 * LANES]
            o_ref[:, c:c + LANES] = (t1 * cos - t2 * sin).astype(o_ref.dtype)
            o_ref[:, c + LANES:c + 2 * LANES] = (t2 * cos + t1 * sin).astype(o_ref.dtype)

    @pl.when(j > 0)
    def _():
        o_ref[...] = _dot(xn_ref[...], wr_ref[...]).astype(o_ref.dtype)


def _inproj(h, g, w_qk, w_all, rope_cs):
    T, D = h.shape
    n_qk = w_qk.shape[1]
    n_rest = w_all.shape[1] - n_qk
    assert n_rest % n_qk == 0
    bm = _largest_divisor(T, (768, 512, 384, 256, 128, 64, 32, 16))
    body = functools.partial(_inproj_body, n_qk=n_qk)
    return pl.pallas_call(
        body,
        grid=(T // bm, 1 + n_rest // n_qk),
        in_specs=[
            pl.BlockSpec((bm, D), lambda i, j: (i, 0)),
            pl.BlockSpec((1, D), lambda i, j: (0, 0)),
            pl.BlockSpec((D, n_qk), lambda i, j: (0, 0), pipeline_mode=pl.Buffered(1)),
            pl.BlockSpec((D, n_qk), lambda i, j: (0, jnp.maximum(j, 1))),
            pl.BlockSpec((2, 2, bm, LANES), lambda i, j: (0, 0, i, 0)),
        ],
        out_specs=pl.BlockSpec((bm, n_qk), lambda i, j: (i, j)),
        out_shape=jax.ShapeDtypeStruct((T, n_qk + n_rest), BF16),
        scratch_shapes=[pltpu.VMEM((bm, D), BF16)],
        compiler_params=_params("parallel", "arbitrary"),
        name="inproj",
    )(h, g, w_qk, w_all, rope_cs)


def _inproj_first(x, meta, g, w_qk, w_all, rope_cs):
    B, S, D = x.shape
    L = S + N_META
    T = B * L
    n_qk = w_qk.shape[1]
    n_rest = w_all.shape[1] - n_qk
    assert n_rest % n_qk == 0
    bm = _largest_divisor(L, tuple(b for b in range(768, 15, -16)))
    blocks_per_seq = L // bm
    assert pl.cdiv(S, bm) == blocks_per_seq and blocks_per_seq * bm - S == N_META
    body = functools.partial(_inproj_first_body, n_qk=n_qk, blocks_per_seq=blocks_per_seq)
    return pl.pallas_call(
        body,
        grid=(T // bm, 1 + n_rest // n_qk),
        in_specs=[
            pl.BlockSpec((None, bm, D), lambda i, j: (i // blocks_per_seq, i % blocks_per_seq, 0)),
            pl.BlockSpec((N_META, D), lambda i, j: (0, 0)),
            pl.BlockSpec((1, D), lambda i, j: (0, 0)),
            pl.BlockSpec((D, n_qk), lambda i, j: (0, 0), pipeline_mode=pl.Buffered(1)),
            pl.BlockSpec((D, n_qk), lambda i, j: (0, jnp.maximum(j, 1))),
            pl.BlockSpec((2, 2, bm, LANES), lambda i, j: (0, 0, i, 0)),
        ],
        out_specs=[pl.BlockSpec((bm, n_qk), lambda i, j: (i, j)), pl.BlockSpec((bm, D), lambda i, j: (i, 0))],
        out_shape=[jax.ShapeDtypeStruct((T, n_qk + n_rest), BF16), jax.ShapeDtypeStruct((T, D), F32)],
        scratch_shapes=[pltpu.VMEM((bm, D), BF16)],
        compiler_params=_params("parallel", "arbitrary", vmem_limit_bytes=VMEM_CAP_BYTES - 2 * 1024 * 1024),
        name="inproj_first",
    )(x, meta, g, w_qk, w_all, rope_cs)


PAIR = 2 * LANES


def _attn_body(lam_ref, q_ref, k_ref, v_ref, sg_ref, o_ref, kp_ref, vp_ref, of_ref,
               *, L, l_main, l_keys, bq, lambda_init):
    lane = lax.broadcasted_iota(jnp.int32, (1, PAIR), 1)
    map_of_lane = (lane % HEAD_DIM) // (HEAD_DIM // 2)

    kp_ref[pl.ds(0, L), :] = k_ref[...]
    if l_keys > L:
        kp_ref[pl.ds(L, l_keys - L), :] = jnp.zeros((l_keys - L, PAIR), BF16)
    for hh in range(2):
        vp_ref[hh, pl.ds(0, L), pl.ds(0, LANES)] = v_ref[:, hh * LANES:(hh + 1) * LANES]
        vp_ref[hh, :, pl.ds(LANES, LANES)] = jnp.ones((l_keys, LANES), BF16)
        if l_keys > L:
            vp_ref[hh, pl.ds(L, l_keys - L), pl.ds(0, LANES)] = jnp.zeros((l_keys - L, LANES), BF16)

    lam = lam_ref[...]
    lam_full = (jnp.exp(jnp.sum(lam[0:1] * lam[1:2], axis=-1, keepdims=True))
                - jnp.exp(jnp.sum(lam[2:3] * lam[3:4], axis=-1, keepdims=True)) + lambda_init)

    tail_valid = (lax.broadcasted_iota(jnp.int32, (1, LANES), 1) + l_main) < L
    contract_last = (((1,), (1,)), ((), ()))
    zero = jnp.zeros((), BF16)
    q_all = q_ref[...]

    for hh in range(2):
        in_head = ((lane % LANES) // HEAD_DIM) == hh
        map_masks = (in_head & (map_of_lane == 0), in_head & (map_of_lane == 1))
        for i in range(L // bq):
            qb = q_all[i * bq:(i + 1) * bq]
            q_maps = jnp.concatenate([jnp.where(m, qb, zero) for m in map_masks], axis=0)
            s = lax.dot_general(q_maps, kp_ref[...], contract_last, preferred_element_type=F32)
            s_main = s[:, :l_main]
            m = jnp.max(s_main, axis=-1, keepdims=True)
            if l_keys > l_main:
                s_tail = jnp.where(tail_valid, s[:, l_main:], NEG_BIG)
                m = jnp.maximum(m, jnp.max(s_tail, axis=-1, keepdims=True))
            acc = _dot(jnp.exp2(s_main - m).astype(BF16), vp_ref[hh, pl.ds(0, l_main), :])
            if l_keys > l_main:
                acc = acc + _dot(jnp.exp2(s_tail - m).astype(BF16), vp_ref[hh, pl.ds(l_main, l_keys - l_main), :])
            a0, a1 = acc[:bq], acc[bq:]
            o = a0[:, :LANES] * (1.0 / a0[:, LANES:LANES + 1]) - a1[:, :LANES] * (lam_full / a1[:, LANES:LANES + 1])
            of_ref[pl.ds(i * bq, bq), pl.ds(hh * LANES, LANES)] = _rmsnorm(o, sg_ref[...]) * (1.0 - lambda_init)
    o_ref[...] = of_ref[...].astype(o_ref.dtype)


def _attention(z3, lam, subln_g, *, n_heads, lambda_init):
    B, L, _ = z3.shape
    l_main = (L // LANES) * LANES
    l_keys = l_main if l_main == L else l_main + LANES
    bq = _largest_divisor(L, tuple(b for b in range(400, 7, -8)))
    n_pairs = n_heads // 2
    body = functools.partial(_attn_body, L=L, l_main=l_main, l_keys=l_keys, bq=bq, lambda_init=lambda_init)
    pair = lambda off: pl.BlockSpec((None, L, PAIR), lambda b, h: (b, 0, off + h))
    const = lambda shape: pl.BlockSpec(shape, lambda b, h: (0,) * len(shape))
    return pl.pallas_call(
        body,
        grid=(B, n_pairs),
        in_specs=[const(lam.shape), pair(0), pair(n_pairs), pair(2 * n_pairs), const(subln_g.shape)],
        out_specs=pl.BlockSpec((None, L, PAIR), lambda b, h: (b, 0, h)),
        out_shape=jax.ShapeDtypeStruct((B, L, n_heads * LANES), BF16),
        scratch_shapes=[pltpu.VMEM((l_keys, PAIR), BF16), pltpu.VMEM((2, l_keys, PAIR), BF16),
                        pltpu.VMEM((L, PAIR), F32)],
        compiler_params=_params("parallel", "parallel"),
        name="diff_attention",
    )(lam, z3, z3, z3, subln_g)


def _pool_body(u_ref, band_ref, pw_ref, ps_ref, o_ref, pad_ref, *, L, cg, br):
    S = L - N_META
    P = pad_ref.shape[1]
    pad_ref[pl.ds(0, POOL_PAD), :] = jnp.zeros((POOL_PAD, P), pad_ref.dtype)
    pad_ref[pl.ds(POOL_PAD + L, POOL_PAD), :] = jnp.zeros((POOL_PAD, P), pad_ref.dtype)
    pad_ref[pl.ds(POOL_PAD, N_META), :] = u_ref[pl.ds(S, N_META), :]
    pad_ref[pl.ds(POOL_PAD + N_META, S), :] = u_ref[pl.ds(0, S), :]
    for start in range(0, L, br):
        rows = min(br, L - start)
        t = lax.broadcasted_iota(jnp.int32, (rows, 1), 0) + start
        for g, w in enumerate(POOL_WINDOWS):
            cols = slice(g * cg, (g + 1) * cg)
            win = _dot(band_ref[g, pl.ds(0, rows), pl.ds(0, rows + 2 * POOL_PAD)],
                       pad_ref[pl.ds(start, rows + 2 * POOL_PAD), cols])
            cnt = (jnp.minimum(t + w // 2, L) - jnp.maximum(t - w // 2, 0)).astype(F32)
            m = win * (1.0 / cnt) - pad_ref[pl.ds(start + POOL_PAD, rows), cols].astype(F32)
            y = (_dot(m.astype(BF16), pw_ref[g]) * ps_ref[:, cols]).astype(o_ref.dtype)
            if start == 0:
                o_ref[pl.ds(S, N_META), cols] = y[:N_META]
                o_ref[pl.ds(0, rows - N_META), cols] = y[N_META:]
            else:
                o_ref[pl.ds(start - N_META, rows), cols] = y


def _pool(z3, pool_w, pool_scale, *, col_block):
    B, L, _ = z3.shape
    G, cg, _ = pool_w.shape
    P = G * cg
    br = min(L, POOL_BLOCK_ROWS)
    assert L % SUBLANES_BF16 == 0 and N_META % SUBLANES_BF16 == 0
    r = lax.broadcasted_iota(jnp.int32, (br, br + 2 * POOL_PAD), 0)
    c = lax.broadcasted_iota(jnp.int32, (br, br + 2 * POOL_PAD), 1) - POOL_PAD
    band = jnp.stack([((c >= r - w // 2) & (c < r + w // 2)).astype(BF16) for w in POOL_WINDOWS])
    body = functools.partial(_pool_body, L=L, cg=cg, br=br)
    return pl.pallas_call(
        body,
        grid=(B,),
        in_specs=[
            pl.BlockSpec((None, L, P), lambda b: (b, 0, col_block)),
            pl.BlockSpec(band.shape, lambda b: (0, 0, 0)),
            pl.BlockSpec((G, cg, cg), lambda b: (0, 0, 0)),
            pl.BlockSpec((1, P), lambda b: (0, 0)),
        ],
        out_specs=pl.BlockSpec((None, L, P), lambda b: (b, 0, 0)),
        out_shape=jax.ShapeDtypeStruct((B, L, P), BF16),
        scratch_shapes=[pltpu.VMEM((L + 2 * POOL_PAD, P), BF16)],
        compiler_params=_params("parallel"),
        name="pool_mixer",
    )(z3, band, pool_w, pool_scale)


def _mix_rows(a_ref, p_ref, gt_ref, b_ref, wa_ref, wp_ref, wo_ref, h_ref, D):
    gates = jax.nn.sigmoid(gt_ref[...].astype(F32) + b_ref[...])
    y = gates[:, :D] * _dot(a_ref[...], wa_ref[...]) + gates[:, D:] * _dot(p_ref[...], wp_ref[...])
    return h_ref[...] + _dot(y.astype(BF16), wo_ref[...])


def _mix_body(a_ref, p_ref, gt_ref, b_ref, wa_ref, wp_ref, wo_ref, h_ref, o_ref, *, D):
    o_ref[...] = _mix_rows(a_ref, p_ref, gt_ref, b_ref, wa_ref, wp_ref, wo_ref, h_ref, D)


def _mix_route_body(a_ref, p_ref, gt_ref, b_ref, wa_ref, wp_ref, wo_ref, h_ref, g_ref, r_ref, tri_ref,
                    o_ref, route_ref, route_t_ref, cnt_ref, carry_ref, *, D, n_experts):
    h = _mix_rows(a_ref, p_ref, gt_ref, b_ref, wa_ref, wp_ref, wo_ref, h_ref, D)
    o_ref[...] = h
    _route_rows(h, g_ref, r_ref, tri_ref, route_ref, route_t_ref, cnt_ref, carry_ref, n_experts)


def _mix(a, p, z, b_gate, wa, wp, wo, h, *, gate_col_block, routing=None):
    T, D = h.shape
    A, P = a.shape[1], p.shape[1]
    bm = _largest_divisor(T, (384, 256, 128, 64, 32, 16))
    const = lambda shape: pl.BlockSpec(shape, lambda i: (0, 0), pipeline_mode=pl.Buffered(1))
    rows = lambda width: pl.BlockSpec((bm, width), lambda i: (i, 0))
    in_specs = [rows(A), rows(P), pl.BlockSpec((bm, 2 * D), lambda i: (i, gate_col_block)),
                const((1, 2 * D)), const((A, D)), const((P, D)), const((D, D)), rows(D)]
    if routing is None:
        return pl.pallas_call(
            functools.partial(_mix_body, D=D),
            grid=(T // bm,),
            in_specs=in_specs,
            out_specs=rows(D),
            out_shape=jax.ShapeDtypeStruct((T, D), F32),
            compiler_params=_params("parallel"),
            name="mix_out",
        )(a, p, z, b_gate, wa, wp, wo, h)
    g_ffn, r_hi_lo, n_experts = routing
    tri = (lax.broadcasted_iota(jnp.int32, (bm, bm), 0) > lax.broadcasted_iota(jnp.int32, (bm, bm), 1)).astype(BF16)
    return pl.pallas_call(
        functools.partial(_mix_route_body, D=D, n_experts=n_experts),
        grid=(T // bm,),
        in_specs=in_specs + [const((1, D)), const((D, 2 * LANES)), const((bm, bm))],
        out_specs=[rows(D), rows(LANES), pl.BlockSpec((8, bm), lambda i: (0, i)),
                   pl.BlockSpec((8, LANES), lambda i: (0, 0))],
        out_shape=[jax.ShapeDtypeStruct((T, D), F32), jax.ShapeDtypeStruct((T, LANES), F32),
                   jax.ShapeDtypeStruct((8, T), F32), jax.ShapeDtypeStruct((8, LANES), F32)],
        scratch_shapes=[pltpu.VMEM((1, LANES), F32)],
        compiler_params=_params("arbitrary"),
        name="mix_out_route",
    )(a, p, z, b_gate, wa, wp, wo, h, g_ffn, r_hi_lo, tri)


def _ffn_body(x_ref, g_ref, wg_ref, wu_ref, wd_ref, o_ref, xn_ref):
    @pl.when(pl.program_id(1) == 0)
    def _():
        x = x_ref[...]
        xn_ref[...] = _rmsnorm(x, g_ref[...]).astype(BF16)
        o_ref[...] = x

    xn = xn_ref[...]
    hid = jax.nn.silu(_dot(xn, wg_ref[...])) * _dot(xn, wu_ref[...])
    o_ref[...] += _dot(hid.astype(BF16), wd_ref[...])


def _dense_ffn(h, g, wg, wu, wd):
    T, D = h.shape
    F = wg.shape[1]
    bm = _largest_divisor(T, (768, 512, 384, 256, 128, 64, 32, 16))
    fc = _largest_divisor(F, (512, 256, 128))
    return pl.pallas_call(
        _ffn_body,
        grid=(T // bm, F // fc),
        in_specs=[
            pl.BlockSpec((bm, D), lambda i, c: (i, 0)),
            pl.BlockSpec((1, D), lambda i, c: (0, 0)),
            pl.BlockSpec((D, fc), lambda i, c: (0, c)),
            pl.BlockSpec((D, fc), lambda i, c: (0, c)),
            pl.BlockSpec((fc, D), lambda i, c: (c, 0)),
        ],
        out_specs=pl.BlockSpec((bm, D), lambda i, c: (i, 0)),
        out_shape=jax.ShapeDtypeStruct((T, D), F32),
        scratch_shapes=[pltpu.VMEM((bm, D), BF16)],
        compiler_params=_params("parallel", "arbitrary"),
        name="dense_ffn",
    )(h, g, wg, wu, wd)


ROUTE_I1, ROUTE_I2, ROUTE_W1, ROUTE_W2, ROUTE_R1, ROUTE_R2 = range(6)


def _route_rows(x, g_ref, r_ref, tri_ref, route_ref, route_t_ref, cnt_ref, carry_ref, n_experts):
    @pl.when(pl.program_id(0) == 0)
    def _():
        carry_ref[...] = jnp.zeros_like(carry_ref)

    xn = _rmsnorm(x, g_ref[...])
    x_hi = xn.astype(BF16)
    x_lo = (xn - x_hi.astype(F32)).astype(BF16)
    hi_both = _dot(x_hi, r_ref[...])
    logits = hi_both[:, :LANES] + (hi_both[:, LANES:] + _dot(x_lo, r_ref[:, pl.ds(0, LANES)]))

    lane = lax.broadcasted_iota(jnp.int32, (1, LANES), 1)
    lane_f = lane.astype(F32)
    lg = jnp.where(lane < n_experts, logits, NEG_BIG)
    v1 = jnp.max(lg, axis=-1, keepdims=True)
    i1 = jnp.min(jnp.where(lg == v1, lane_f, float(LANES)), axis=-1, keepdims=True)
    lg2 = jnp.where(lane_f == i1, NEG_BIG, lg)
    v2 = jnp.max(lg2, axis=-1, keepdims=True)
    i2 = jnp.min(jnp.where(lg2 == v2, lane_f, float(LANES)), axis=-1, keepdims=True)
    e2 = jnp.exp(v2 - v1)
    w1 = 1.0 / (1.0 + e2)
    w2 = e2 / (1.0 + e2)

    sel1 = lane_f == i1
    sel2 = lane_f == i2
    onehot = jnp.where(sel1 | sel2, 1.0, 0.0)
    rank = _dot(tri_ref[...], onehot.astype(BF16)) + carry_ref[...]
    r1 = jnp.sum(jnp.where(sel1, rank, 0.0), axis=-1, keepdims=True)
    r2 = jnp.sum(jnp.where(sel2, rank, 0.0), axis=-1, keepdims=True)
    carry_ref[...] += jnp.sum(onehot, axis=0, keepdims=True)

    route = jnp.zeros(route_ref.shape, F32)
    for col, val in ((ROUTE_I1, i1), (ROUTE_I2, i2), (ROUTE_W1, w1), (ROUTE_W2, w2), (ROUTE_R1, r1), (ROUTE_R2, r2)):
        route = jnp.where(lane == col, val, route)
    route_ref[...] = route
    route_t_ref[...] = jnp.transpose(route)[:route_t_ref.shape[0]]
    cnt_ref[...] = jnp.broadcast_to(carry_ref[...], cnt_ref.shape)


ZERO_ROWS = 64
ROW_DMA_UNROLL = 8


def _dispatch_body(pos_ref, seg_ref, x_ref, g_ref, xs_hbm, xn_ref, zero_ref, sems, *, bm, T, n_experts, be, n_rows):
    i = pl.program_id(0)
    n_steps = pl.num_programs(0)
    slot = i % 2

    def zero_copy(first_row, c):
        row = pl.multiple_of(first_row + c * ZERO_ROWS, ZERO_ROWS)
        return pltpu.make_async_copy(zero_ref, xs_hbm.at[pl.ds(row, ZERO_ROWS), :], sems.at[2])

    def zero_block(first_row):
        for c in range(be // ZERO_ROWS):
            zero_copy(first_row, c).start()
        for c in range(be // ZERO_ROWS):
            zero_copy(first_row, c).wait()

    @pl.when(i == 0)
    def _():
        zero_ref[...] = jnp.zeros_like(zero_ref)
        for e in range(n_experts):
            @pl.when(seg_ref[n_experts + e] > 0)
            def _():
                zero_block(seg_ref[e] - be)

        def clear_unused(j, c):
            zero_block(j * be)
            return c

        lax.fori_loop(seg_ref[n_experts - 1] // be, n_rows // be, clear_unused, 0)

    def drain(s):
        for _ in range(TOP_K):
            pltpu.make_async_copy(xn_ref.at[s], xs_hbm.at[pl.ds(0, bm), :], sems.at[s]).wait()

    @pl.when(i >= 2)
    def _():
        drain(slot)

    xn_ref[slot] = _rmsnorm(x_ref[...], g_ref[...])

    def row_copy(r, k):
        dst = pos_ref[k * T + i * bm + r]
        return pltpu.make_async_copy(xn_ref.at[slot, pl.ds(r, 1), :], xs_hbm.at[pl.ds(dst, 1), :], sems.at[slot])

    def issue(grp, c):
        for u in range(ROW_DMA_UNROLL):
            row_copy(grp * ROW_DMA_UNROLL + u, 0).start(priority=u % 2)
            row_copy(grp * ROW_DMA_UNROLL + u, 1).start(priority=(u + 1) % 2)
        return c

    lax.fori_loop(0, bm // ROW_DMA_UNROLL, issue, 0)

    @pl.when(i == n_steps - 1)
    def _():
        drain(slot)

        @pl.when(i >= 1)
        def _():
            drain(1 - slot)


def _dispatch(pos, seg, h, g, *, n_rows, n_experts, be):
    T, D = h.shape
    bm = _largest_divisor(T, (256, 128, 64, 32, 16))
    body = functools.partial(_dispatch_body, bm=bm, T=T, n_experts=n_experts, be=be, n_rows=n_rows)
    return pl.pallas_call(
        body,
        grid_spec=pltpu.PrefetchScalarGridSpec(
            num_scalar_prefetch=2,
            grid=(T // bm,),
            in_specs=[pl.BlockSpec((bm, D), lambda i, pos, seg: (i, 0)),
                      pl.BlockSpec((1, D), lambda i, pos, seg: (0, 0))],
            out_specs=pl.BlockSpec(memory_space=pl.ANY),
            scratch_shapes=[pltpu.VMEM((2, bm, D), F32), pltpu.VMEM((ZERO_ROWS, D), F32),
                            pltpu.SemaphoreType.DMA((3,))],
        ),
        out_shape=jax.ShapeDtypeStruct((n_rows, D), F32),
        compiler_params=_params("arbitrary"),
        name="moe_dispatch",
    )(pos, seg, h, g)


def _expert_body(blk_ref, x_ref, wg_ref, wu_ref, wd_ref, o_ref, xb_ref, *, n_blocks):
    j = pl.program_id(0)

    @pl.when(pl.program_id(1) == 0)
    def _():
        o_ref[...] = jnp.zeros_like(o_ref)

    @pl.when(j < blk_ref[n_blocks])
    def _():
        @pl.when(pl.program_id(1) == 0)
        def _():
            xb_ref[...] = x_ref[...].astype(BF16)

        xb = xb_ref[...]
        hid = jax.nn.silu(_dot(xb, wg_ref[...])) * _dot(xb, wu_ref[...])
        o_ref[...] += _dot(hid.astype(BF16), wd_ref[...])


def _expert_ffn(blk, xs, wg, wu, wd, *, be):
    n_rows, D = xs.shape
    F = wg.shape[2]
    n_blocks = n_rows // be
    fc = _largest_divisor(F, (512, 256, 128))
    nc = F // fc
    body = functools.partial(_expert_body, n_blocks=n_blocks)

    def row_blk(j, blk):
        return jnp.minimum(j, blk[n_blocks] - 1)

    def chunk(j, c, blk):
        return jnp.where(j < blk[n_blocks], c, nc - 1)

    return pl.pallas_call(
        body,
        grid_spec=pltpu.PrefetchScalarGridSpec(
            num_scalar_prefetch=1,
            grid=(n_blocks, nc),
            in_specs=[
                pl.BlockSpec((be, D), lambda j, c, blk: (row_blk(j, blk), 0)),
                pl.BlockSpec((None, D, fc), lambda j, c, blk: (blk[row_blk(j, blk)], 0, chunk(j, c, blk))),
                pl.BlockSpec((None, D, fc), lambda j, c, blk: (blk[row_blk(j, blk)], 0, chunk(j, c, blk))),
                pl.BlockSpec((None, fc, D), lambda j, c, blk: (blk[row_blk(j, blk)], chunk(j, c, blk), 0)),
            ],
            out_specs=pl.BlockSpec((be, D), lambda j, c, blk: (j, 0)),
            scratch_shapes=[pltpu.VMEM((be, D), BF16)],
        ),
        out_shape=jax.ShapeDtypeStruct((n_rows, D), F32),
        compiler_params=_params("arbitrary", "arbitrary"),
        name="moe_experts",
    )(blk, xs, wg, wu, wd)


def _combine_body(pos_ref, h_ref, route_ref, g_ref, y_hbm, o_ref, y_ref, sems, *, bm, T):
    i = pl.program_id(0)
    n_steps = pl.num_programs(0)
    slot = i % 2

    def start_gathers(step, s):
        def row_copy(r, k):
            src = pos_ref[k * T + step * bm + r]
            return pltpu.make_async_copy(y_hbm.at[pl.ds(src, 1), :], y_ref.at[s, k, pl.ds(r, 1), :], sems.at[s])

        def issue(grp, c):
            for u in range(ROW_DMA_UNROLL):
                row_copy(grp * ROW_DMA_UNROLL + u, 0).start(priority=u % 2)
                row_copy(grp * ROW_DMA_UNROLL + u, 1).start(priority=(u + 1) % 2)
            return c

        lax.fori_loop(0, bm // ROW_DMA_UNROLL, issue, 0)

    @pl.when(i == 0)
    def _():
        start_gathers(i, slot)

    @pl.when(i + 1 < n_steps)
    def _():
        start_gathers(i + 1, 1 - slot)

    for k in range(TOP_K):
        pltpu.make_async_copy(y_hbm.at[pl.ds(0, bm), :], y_ref.at[slot, k], sems.at[slot]).wait()

    route = route_ref[...]
    w1 = route[:, ROUTE_W1:ROUTE_W1 + 1]
    w2 = route[:, ROUTE_W2:ROUTE_W2 + 1]
    h = h_ref[...] + (w1 * y_ref[slot, 0] + w2 * y_ref[slot, 1])
    o_ref[...] = _rmsnorm(h, g_ref[...])


def _combine(pos, h, route, g, y, *, B, L):
    T, D = h.shape
    S = L - N_META
    bm = _largest_divisor(L, tuple(b for b in range(512, 7, -8)))
    blocks_per_seq = L // bm
    assert pl.cdiv(S, bm) == blocks_per_seq
    body = functools.partial(_combine_body, bm=bm, T=T)
    return pl.pallas_call(
        body,
        grid_spec=pltpu.PrefetchScalarGridSpec(
            num_scalar_prefetch=1,
            grid=(T // bm,),
            in_specs=[pl.BlockSpec((bm, D), lambda i, pos: (i, 0)),
                      pl.BlockSpec((bm, LANES), lambda i, pos: (i, 0)),
                      pl.BlockSpec((1, D), lambda i, pos: (0, 0)),
                      pl.BlockSpec(memory_space=pl.ANY)],
            out_specs=pl.BlockSpec((None, bm, D), lambda i, pos: (i // blocks_per_seq, i % blocks_per_seq, 0)),
            scratch_shapes=[pltpu.VMEM((2, TOP_K, bm, D), F32), pltpu.SemaphoreType.DMA((2,))],
        ),
        out_shape=jax.ShapeDtypeStruct((B, S, D), F32),
        compiler_params=_params("arbitrary"),
        name="moe_combine",
    )(pos, h, route, g, y)


def _rope_tables(B, L):
    half = HEAD_DIM // 2
    inv = ROPE_THETA ** (-2.0 * jnp.arange(half, dtype=F32) / HEAD_DIM)
    pos = jnp.concatenate([jnp.arange(N_META, L), jnp.arange(N_META)]).astype(F32)
    ang = pos[:, None] * inv[None, :]
    cs = jnp.tile(jnp.stack([jnp.cos(ang), jnp.sin(ang)]), (1, 1, LANES // half))
    cs = jnp.stack([cs * (HEAD_DIM ** -0.5 * LOG2_E), cs])
    return jnp.broadcast_to(cs[:, :, None], (2, 2, B, L, LANES)).reshape(2, 2, B * L, LANES)


def _pair_rotary_halves(w_qk, n_heads):
    D = w_qk.shape[0]
    w = w_qk.reshape(D, 2, n_heads // 2, 2, 2, 2, HEAD_DIM // 2)
    return w.transpose(0, 1, 2, 5, 3, 4, 6).reshape(D, -1)


def _token_mixer(h, layer, B, L, w_in, b_gate, lam, subln_g, pool_w, pool_scale, w_ba, w_bp, w_out, norm_g, rope_cs,
                 routing=None):
    A, P = w_ba.shape[0], w_bp.shape[0]
    n_heads = A // (2 * HEAD_DIM)
    lambda_init = 0.8 - 0.6 * math.exp(-0.3 * layer)
    w_qk = _pair_rotary_halves(w_in[:, :2 * A], n_heads).astype(BF16)
    if isinstance(h, tuple):
        z, h = _inproj_first(h[0], h[1], norm_g, w_qk, w_in.astype(BF16), rope_cs)
    else:
        z = _inproj(h, norm_g, w_qk, w_in.astype(BF16), rope_cs)
    T, D = h.shape
    z3 = z.reshape(B, L, z.shape[1])
    a = _attention(z3, lam, subln_g, n_heads=n_heads, lambda_init=lambda_init)
    p = _pool(z3, pool_w, pool_scale, col_block=(3 * A) // P)
    return _mix(a.reshape(T, A), p.reshape(T, P), z, b_gate, w_ba, w_bp, w_out, h,
                gate_col_block=(3 * A + P) // (2 * D), routing=routing)


def _router_hi_lo(router):
    D, E = router.shape
    r_pad = jnp.zeros((D, LANES), F32).at[:, :E].set(router)
    r_hi = r_pad.astype(BF16)
    r_lo = (r_pad - r_hi.astype(F32)).astype(BF16)
    return jnp.concatenate([r_hi, r_lo], axis=1)


def _moe_and_final_norm(h, route, route_t, cnt, norm_g, E, wg, wu, wd, final_g, *, B, L):
    T, D = h.shape
    be = 1024 if T >= 8192 else 64

    counts = cnt[0, :E].astype(jnp.int32)
    n_blk = (counts + be - 1) // be
    seg_end = jnp.cumsum(n_blk * be)
    seg_start = seg_end - n_blk * be
    n_blocks = (TOP_K * T + E * (be - 1)) // be
    blk_end = jnp.cumsum(n_blk)
    blk_expert = jnp.minimum(jnp.sum(jnp.arange(n_blocks)[:, None] >= blk_end[None, :], axis=1), E - 1)
    blk = jnp.concatenate([blk_expert.astype(jnp.int32), blk_end[-1:].astype(jnp.int32)])
    seg = jnp.concatenate([seg_end, n_blk]).astype(jnp.int32)

    def slot_of(expert_f, rank_f):
        start = jnp.sum(jnp.where(expert_f[:, None] == jnp.arange(E, dtype=F32)[None, :], seg_start[None, :], 0), axis=1)
        return start + rank_f.astype(jnp.int32)

    pos = jnp.concatenate([slot_of(route_t[ROUTE_I1], route_t[ROUTE_R1]), slot_of(route_t[ROUTE_I2], route_t[ROUTE_R2])])

    xs = _dispatch(pos, seg, h, norm_g, n_rows=n_blocks * be, n_experts=E, be=be)
    y = _expert_ffn(blk, xs, wg, wu, wd, be=be)
    return _combine(pos, h, route, final_g, y, B=B, L=L)


def kernel(x, meta_tokens, norm_mix, w_in, b_gate, lambda_q1, lambda_k1, lambda_q2, lambda_k2, subln, pool_w, pool_scale, w_branch_attn, w_branch_pool, w_out, norm_ffn, dense_w_gate, dense_w_up, dense_w_down, router, moe_w_gate, moe_w_up, moe_w_down, norm_final):
    B, S, D = x.shape
    depth = w_in.shape[0]
    assert depth == 2, "the final RMSNorm is fused into the routed layer, which must be the last one"
    L = N_META + S
    T = B * L
    h = (x, meta_tokens.astype(x.dtype))
    rope = _rope_tables(B, L)
    row = lambda v: v.reshape(1, -1)

    for i in range(depth):
        lam = jnp.stack([lambda_q1[i], lambda_k1[i], lambda_q2[i], lambda_k2[i]])
        j = i // 2
        routed = i % 2 == 1
        n_experts = router.shape[2]
        mixed = _token_mixer(h, i, B, L, w_in[i], row(b_gate[i]), lam, row(subln[i]),
                             pool_w[i].astype(BF16), row(pool_scale[i]), w_branch_attn[i].astype(BF16),
                             w_branch_pool[i].astype(BF16), w_out[i].astype(BF16), row(norm_mix[i]), rope,
                             routing=(row(norm_ffn[i]), _router_hi_lo(router[j]), n_experts) if routed else None)
        if not routed:
            h = _dense_ffn(mixed, row(norm_ffn[i]), dense_w_gate[j].astype(BF16), dense_w_up[j].astype(BF16),
                           dense_w_down[j].astype(BF16))
        else:
            h, route, route_t, cnt = mixed
            h = _moe_and_final_norm(h, route, route_t, cnt, row(norm_ffn[i]), n_experts, moe_w_gate[j].astype(BF16),
                                    moe_w_up[j].astype(BF16), moe_w_down[j].astype(BF16), row(norm_final),
                                    B=B, L=L)
    return h
```

```python
import functools
import math

import jax
import jax.numpy as jnp
from jax import lax
from jax.experimental import pallas as pl
from jax.experimental.pallas import tpu as pltpu

N_META = 16
HEAD_DIM = 64
POOL_WINDOWS = (2, 4, 8, 16)
TOP_K = 2
ROPE_THETA = 10000.0
EPS = 1e-5

LANES = 128
SUBLANES_BF16 = 16
VMEM_CAP_BYTES = 64 * 1024 * 1024
VMEM_LIMIT_BYTES = VMEM_CAP_BYTES - 6 * 1024 * 1024

NEG_BIG = -1e30
LOG2_E = 1.4426950408889634
POOL_PAD = max(POOL_WINDOWS)
MXU_DEPTH = 256
POOL_BLOCK_ROWS = MXU_DEPTH - 2 * POOL_PAD

F32 = jnp.float32
BF16 = jnp.bfloat16


def _largest_divisor(n, candidates):
    for c in candidates:
        if c <= n and n % c == 0:
            return c
    raise ValueError(f"no block size in {candidates} divides {n}")


def _params(*semantics, vmem_limit_bytes=VMEM_LIMIT_BYTES):
    return pltpu.CompilerParams(dimension_semantics=semantics, vmem_limit_bytes=vmem_limit_bytes)


def _rmsnorm(x, g):
    ms = jnp.mean(x * x, axis=-1, keepdims=True)
    return x * lax.rsqrt(ms + EPS) * g


def _dot(a, b):
    return jnp.dot(a, b, preferred_element_type=F32)


def _inproj_body(x_ref, g_ref, wqk_ref, wr_ref, cs_ref, o_ref, xn_ref, *, n_qk):
    _inproj_step(lambda: x_ref[...], g_ref, wqk_ref, wr_ref, cs_ref, o_ref, xn_ref, n_qk)


def _inproj_first_body(x_ref, meta_ref, g_ref, wqk_ref, wr_ref, cs_ref, o_ref, h_ref, xn_ref, *, n_qk, blocks_per_seq):
    def assemble():
        h_ref[...] = x_ref[...]

        @pl.when(pl.program_id(0) % blocks_per_seq == blocks_per_seq - 1)
        def _():
            h_ref[pl.ds(h_ref.shape[0] - N_META, N_META), :] = meta_ref[...]

        return h_ref[...]

    _inproj_step(assemble, g_ref, wqk_ref, wr_ref, cs_ref, o_ref, xn_ref, n_qk)


def _inproj_step(load_rows, g_ref, wqk_ref, wr_ref, cs_ref, o_ref, xn_ref, n_qk):
    j = pl.program_id(1)

    @pl.when(j == 0)
    def _():
        xn_ref[...] = _rmsnorm(load_rows(), g_ref[...]).astype(BF16)
        acc = _dot(xn_ref[...], wqk_ref[...])
        for c in range(0, n_qk, 2 * LANES):
            table = 0 if c < n_qk // 2 else 1
            cos, sin = cs_ref[table, 0], cs_ref[table, 1]
            t1 = acc[:, c:c + LANES]
            t2 = acc[:, c + LANES:c + 2 * LANES]
            o_ref[:, c:c + LANES] = (t1 * cos - t2 * sin).astype(o_ref.dtype)
            o_ref[:, c + LANES:c + 2 * LANES] = (t2 * cos + t1 * sin).astype(o_ref.dtype)

    @pl.when(j > 0)
    def _():
        o_ref[...] = _dot(xn_ref[...], wr_ref[...]).astype(o_ref.dtype)


def _inproj(h, g, w_qk, w_all, rope_cs):
    T, D = h.shape
    n_qk = w_qk.shape[1]
    n_rest = w_all.shape[1] - n_qk
    assert n_rest % n_qk == 0
    bm = _largest_divisor(T, (768, 512, 384, 256, 128, 64, 32, 16))
    body = functools.partial(_inproj_body, n_qk=n_qk)
    return pl.pallas_call(
        body,
        grid=(T // bm, 1 + n_rest // n_qk),
        in_specs=[
            pl.BlockSpec((bm, D), lambda i, j: (i, 0)),
            pl.BlockSpec((1, D), lambda i, j: (0, 0)),
            pl.BlockSpec((D, n_qk), lambda i, j: (0, 0), pipeline_mode=pl.Buffered(1)),
            pl.BlockSpec((D, n_qk), lambda i, j: (0, jnp.maximum(j, 1))),
            pl.BlockSpec((2, 2, bm, LANES), lambda i, j: (0, 0, i, 0)),
        ],
        out_specs=pl.BlockSpec((bm, n_qk), lambda i, j: (i, j)),
        out_shape=jax.ShapeDtypeStruct((T, n_qk + n_rest), BF16),
        scratch_shapes=[pltpu.VMEM((bm, D), BF16)],
        compiler_params=_params("parallel", "arbitrary"),
        name="inproj",
    )(h, g, w_qk, w_all, rope_cs)


def _inproj_first(x, meta, g, w_qk, w_all, rope_cs):
    B, S, D = x.shape
    L = S + N_META
    T = B * L
    n_qk = w_qk.shape[1]
    n_rest = w_all.shape[1] - n_qk
    assert n_rest % n_qk == 0
    bm = _largest_divisor(L, tuple(b for b in range(768, 15, -16)))
    blocks_per_seq = L // bm
    assert pl.cdiv(S, bm) == blocks_per_seq and blocks_per_seq * bm - S == N_META
    body = functools.partial(_inproj_first_body, n_qk=n_qk, blocks_per_seq=blocks_per_seq)
    return pl.pallas_call(
        body,
        grid=(T // bm, 1 + n_rest // n_qk),
        in_specs=[
            pl.BlockSpec((None, bm, D), lambda i, j: (i // blocks_per_seq, i % blocks_per_seq, 0)),
            pl.BlockSpec((N_META, D), lambda i, j: (0, 0)),
            pl.BlockSpec((1, D), lambda i, j: (0, 0)),
            pl.BlockSpec((D, n_qk), lambda i, j: (0, 0), pipeline_mode=pl.Buffered(1)),
            pl.BlockSpec((D, n_qk), lambda i, j: (0, jnp.maximum(j, 1))),
            pl.BlockSpec((2, 2, bm, LANES), lambda i, j: (0, 0, i, 0)),
        ],
        out_specs=[pl.BlockSpec((bm, n_qk), lambda i, j: (i, j)), pl.BlockSpec((bm, D), lambda i, j: (i, 0))],
        out_shape=[jax.ShapeDtypeStruct((T, n_qk + n_rest), BF16), jax.ShapeDtypeStruct((T, D), F32)],
        scratch_shapes=[pltpu.VMEM((bm, D), BF16)],
        compiler_params=_params("parallel", "arbitrary", vmem_limit_bytes=VMEM_CAP_BYTES - 2 * 1024 * 1024),
        name="inproj_first",
    )(x, meta, g, w_qk, w_all, rope_cs)


PAIR = 2 * LANES


def _attn_body(lam_ref, q_ref, k_ref, v_ref, sg_ref, o_ref, kp_ref, vp_ref, of_ref,
               *, L, l_main, l_keys, bq, lambda_init):
    lane = lax.broadcasted_iota(jnp.int32, (1, PAIR), 1)
    map_of_lane = (lane % HEAD_DIM) // (HEAD_DIM // 2)

    kp_ref[pl.ds(0, L), :] = k_ref[...]
    if l_keys > L:
        kp_ref[pl.ds(L, l_keys - L), :] = jnp.zeros((l_keys - L, PAIR), BF16)
    for hh in range(2):
        vp_ref[hh, pl.ds(0, L), pl.ds(0, LANES)] = v_ref[:, hh * LANES:(hh + 1) * LANES]
        vp_ref[hh, :, pl.ds(LANES, LANES)] = jnp.ones((l_keys, LANES), BF16)
        if l_keys > L:
            vp_ref[hh, pl.ds(L, l_keys - L), pl.ds(0, LANES)] = jnp.zeros((l_keys - L, LANES), BF16)

    lam = lam_ref[...]
    lam_full = (jnp.exp(jnp.sum(lam[0:1] * lam[1:2], axis=-1, keepdims=True))
                - jnp.exp(jnp.sum(lam[2:3] * lam[3:4], axis=-1, keepdims=True)) + lambda_init)

    tail_valid = (lax.broadcasted_iota(jnp.int32, (1, LANES), 1) + l_main) < L
    contract_last = (((1,), (1,)), ((), ()))
    zero = jnp.zeros((), BF16)
    q_all = q_ref[...]

    for hh in range(2):
        in_head = ((lane % LANES) // HEAD_DIM) == hh
        map_masks = (in_head & (map_of_lane == 0), in_head & (map_of_lane == 1))
        for i in range(L // bq):
            qb = q_all[i * bq:(i + 1) * bq]
            q_maps = jnp.concatenate([jnp.where(m, qb, zero) for m in map_masks], axis=0)
            s = lax.dot_general(q_maps, kp_ref[...], contract_last, preferred_element_type=F32)
            s_main = s[:, :l_main]
            m = jnp.max(s_main, axis=-1, keepdims=True)
            if l_keys > l_main:
                s_tail = jnp.where(tail_valid, s[:, l_main:], NEG_BIG)
                m = jnp.maximum(m, jnp.max(s_tail, axis=-1, keepdims=True))
            acc = _dot(jnp.exp2(s_main - m).astype(BF16), vp_ref[hh, pl.ds(0, l_main), :])
            if l_keys > l_main:
                acc = acc + _dot(jnp.exp2(s_tail - m).astype(BF16), vp_ref[hh, pl.ds(l_main, l_keys - l_main), :])
            a0, a1 = acc[:bq], acc[bq:]
            o = a0[:, :LANES] * (1.0 / a0[:, LANES:LANES + 1]) - a1[:, :LANES] * (lam_full / a1[:, LANES:LANES + 1])
            of_ref[pl.ds(i * bq, bq), pl.ds(hh * LANES, LANES)] = _rmsnorm(o, sg_ref[...]) * (1.0 - lambda_init)
    o_ref[...] = of_ref[...].astype(o_ref.dtype)


def _attention(z3, lam, subln_g, *, n_heads, lambda_init):
    B, L, _ = z3.shape
    l_main = (L // LANES) * LANES
    l_keys = l_main if l_main == L else l_main + LANES
    bq = _largest_divisor(L, tuple(b for b in range(400, 7, -8)))
    n_pairs = n_heads // 2
    body = functools.partial(_attn_body, L=L, l_main=l_main, l_keys=l_keys, bq=bq, lambda_init=lambda_init)
    pair = lambda off: pl.BlockSpec((None, L, PAIR), lambda b, h: (b, 0, off + h))
    const = lambda shape: pl.BlockSpec(shape, lambda b, h: (0,) * len(shape))
    return pl.pallas_call(
        body,
        grid=(B, n_pairs),
        in_specs=[const(lam.shape), pair(0), pair(n_pairs), pair(2 * n_pairs), const(subln_g.shape)],
        out_specs=pl.BlockSpec((None, L, PAIR), lambda b, h: (b, 0, h)),
        out_shape=jax.ShapeDtypeStruct((B, L, n_heads * LANES), BF16),
        scratch_shapes=[pltpu.VMEM((l_keys, PAIR), BF16), pltpu.VMEM((2, l_keys, PAIR), BF16),
                        pltpu.VMEM((L, PAIR), F32)],
        compiler_params=_params("parallel", "parallel"),
        name="diff_attention",
    )(lam, z3, z3, z3, subln_g)


def _pool_body(u_ref, band_ref, pw_ref, ps_ref, o_ref, pad_ref, *, L, cg, br):
    S = L - N_META
    P = pad_ref.shape[1]
    pad_ref[pl.ds(0, POOL_PAD), :] = jnp.zeros((POOL_PAD, P), pad_ref.dtype)
    pad_ref[pl.ds(POOL_PAD + L, POOL_PAD), :] = jnp.zeros((POOL_PAD, P), pad_ref.dtype)
    pad_ref[pl.ds(POOL_PAD, N_META), :] = u_ref[pl.ds(S, N_META), :]
    pad_ref[pl.ds(POOL_PAD + N_META, S), :] = u_ref[pl.ds(0, S), :]
    for start in range(0, L, br):
        rows = min(br, L - start)
        t = lax.broadcasted_iota(jnp.int32, (rows, 1), 0) + start
        for g, w in enumerate(POOL_WINDOWS):
            cols = slice(g * cg, (g + 1) * cg)
            win = _dot(band_ref[g, pl.ds(0, rows), pl.ds(0, rows + 2 * POOL_PAD)],
                       pad_ref[pl.ds(start, rows + 2 * POOL_PAD), cols])
            cnt = (jnp.minimum(t + w // 2, L) - jnp.maximum(t - w // 2, 0)).astype(F32)
            m = win * (1.0 / cnt) - pad_ref[pl.ds(start + POOL_PAD, rows), cols].astype(F32)
            y = (_dot(m.astype(BF16), pw_ref[g]) * ps_ref[:, cols]).astype(o_ref.dtype)
            if start == 0:
                o_ref[pl.ds(S, N_META), cols] = y[:N_META]
                o_ref[pl.ds(0, rows - N_META), cols] = y[N_META:]
            else:
                o_ref[pl.ds(start - N_META, rows), cols] = y


def _pool(z3, pool_w, pool_scale, *, col_block):
    B, L, _ = z3.shape
    G, cg, _ = pool_w.shape
    P = G * cg
    br = min(L, POOL_BLOCK_ROWS)
    assert L % SUBLANES_BF16 == 0 and N_META % SUBLANES_BF16 == 0
    r = lax.broadcasted_iota(jnp.int32, (br, br + 2 * POOL_PAD), 0)
    c = lax.broadcasted_iota(jnp.int32, (br, br + 2 * POOL_PAD), 1) - POOL_PAD
    band = jnp.stack([((c >= r - w // 2) & (c < r + w // 2)).astype(BF16) for w in POOL_WINDOWS])
    body = functools.partial(_pool_body, L=L, cg=cg, br=br)
    return pl.pallas_call(
        body,
        grid=(B,),
        in_specs=[
            pl.BlockSpec((None, L, P), lambda b: (b, 0, col_block)),
            pl.BlockSpec(band.shape, lambda b: (0, 0, 0)),
            pl.BlockSpec((G, cg, cg), lambda b: (0, 0, 0)),
            pl.BlockSpec((1, P), lambda b: (0, 0)),
        ],
        out_specs=pl.BlockSpec((None, L, P), lambda b: (b, 0, 0)),
        out_shape=jax.ShapeDtypeStruct((B, L, P), BF16),
        scratch_shapes=[pltpu.VMEM((L + 2 * POOL_PAD, P), BF16)],
        compiler_params=_params("parallel"),
        name="pool_mixer",
    )(z3, band, pool_w, pool_scale)


def _mix_rows(a_ref, p_ref, gt_ref, b_ref, wa_ref, wp_ref, wo_ref, h_ref, D):
    gates = jax.nn.sigmoid(gt_ref[...].astype(F32) + b_ref[...])
    y = gates[:, :D] * _dot(a_ref[...], wa_ref[...]) + gates[:, D:] * _dot(p_ref[...], wp_ref[...])
    return h_ref[...] + _dot(y.astype(BF16), wo_ref[...])


def _mix_body(a_ref, p_ref, gt_ref, b_ref, wa_ref, wp_ref, wo_ref, h_ref, o_ref, *, D):
    o_ref[...] = _mix_rows(a_ref, p_ref, gt_ref, b_ref, wa_ref, wp_ref, wo_ref, h_ref, D)


def _mix_route_body(a_ref, p_ref, gt_ref, b_ref, wa_ref, wp_ref, wo_ref, h_ref, g_ref, r_ref, tri_ref,
                    o_ref, route_ref, route_t_ref, cnt_ref, carry_ref, hprev_ref, *, D, n_experts):
    i = pl.program_id(0)

    @pl.when(i == 0)
    def _():
        hprev_ref[...] = jnp.zeros_like(hprev_ref)

    _route_rows(hprev_ref[...], g_ref, r_ref, tri_ref, route_ref, route_t_ref, cnt_ref, carry_ref, n_experts,
                count_scale=jnp.minimum(i, 1).astype(F32))
    h = _mix_rows(a_ref, p_ref, gt_ref, b_ref, wa_ref, wp_ref, wo_ref, h_ref, D)
    o_ref[...] = h
    hprev_ref[...] = h


def _mix(a, p, z, b_gate, wa, wp, wo, h, *, gate_col_block, routing=None):
    T, D = h.shape
    A, P = a.shape[1], p.shape[1]
    bm = _largest_divisor(T, (384, 256, 128, 64, 32, 16))
    const = lambda shape: pl.BlockSpec(shape, lambda i: (0, 0), pipeline_mode=pl.Buffered(1))
    rows = lambda width: pl.BlockSpec((bm, width), lambda i: (i, 0))
    in_specs = [rows(A), rows(P), pl.BlockSpec((bm, 2 * D), lambda i: (i, gate_col_block)),
                const((1, 2 * D)), const((A, D)), const((P, D)), const((D, D)), rows(D)]
    if routing is None:
        return pl.pallas_call(
            functools.partial(_mix_body, D=D),
            grid=(T // bm,),
            in_specs=in_specs,
            out_specs=rows(D),
            out_shape=jax.ShapeDtypeStruct((T, D), F32),
            compiler_params=_params("parallel"),
            name="mix_out",
        )(a, p, z, b_gate, wa, wp, wo, h)
    g_ffn, r_hi_lo, n_experts = routing
    tri = (lax.broadcasted_iota(jnp.int32, (bm, bm), 0) > lax.broadcasted_iota(jnp.int32, (bm, bm), 1)).astype(BF16)
    n = T // bm
    cur = lambda width: pl.BlockSpec((bm, width), lambda i: (jnp.minimum(i, n - 1), 0))
    prev = lambda i: jnp.maximum(i - 1, 0)
    return pl.pallas_call(
        functools.partial(_mix_route_body, D=D, n_experts=n_experts),
        grid=(n + 1,),
        in_specs=[cur(A), cur(P), pl.BlockSpec((bm, 2 * D), lambda i: (jnp.minimum(i, n - 1), gate_col_block)),
                  const((1, 2 * D)), const((A, D)), const((P, D)), const((D, D)), cur(D),
                  const((1, D)), const((D, 2 * LANES)), const((bm, bm))],
        out_specs=[cur(D), pl.BlockSpec((bm, LANES), lambda i: (prev(i), 0)),
                   pl.BlockSpec((8, bm), lambda i: (0, prev(i))), pl.BlockSpec((8, LANES), lambda i: (0, 0))],
        out_shape=[jax.ShapeDtypeStruct((T, D), F32), jax.ShapeDtypeStruct((T, LANES), F32),
                   jax.ShapeDtypeStruct((8, T), F32), jax.ShapeDtypeStruct((8, LANES), F32)],
        scratch_shapes=[pltpu.VMEM((1, LANES), F32), pltpu.VMEM((bm, D), F32)],
        compiler_params=_params("arbitrary"),
        name="mix_out_route",
    )(a, p, z, b_gate, wa, wp, wo, h, g_ffn, r_hi_lo, tri)


def _ffn_body(x_ref, g_ref, wg_ref, wu_ref, wd_ref, o_ref, xn_ref):
    @pl.when(pl.program_id(1) == 0)
    def _():
        x = x_ref[...]
        xn_ref[...] = _rmsnorm(x, g_ref[...]).astype(BF16)
        o_ref[...] = x

    xn = xn_ref[...]
    hid = jax.nn.silu(_dot(xn, wg_ref[...])) * _dot(xn, wu_ref[...])
    o_ref[...] += _dot(hid.astype(BF16), wd_ref[...])


def _dense_ffn(h, g, wg, wu, wd):
    T, D = h.shape
    F = wg.shape[1]
    bm = _largest_divisor(T, (768, 512, 384, 256, 128, 64, 32, 16))
    fc = _largest_divisor(F, (512, 256, 128))
    return pl.pallas_call(
        _ffn_body,
        grid=(T // bm, F // fc),
        in_specs=[
            pl.BlockSpec((bm, D), lambda i, c: (i, 0)),
            pl.BlockSpec((1, D), lambda i, c: (0, 0)),
            pl.BlockSpec((D, fc), lambda i, c: (0, c)),
            pl.BlockSpec((D, fc), lambda i, c: (0, c)),
            pl.BlockSpec((fc, D), lambda i, c: (c, 0)),
        ],
        out_specs=pl.BlockSpec((bm, D), lambda i, c: (i, 0)),
        out_shape=jax.ShapeDtypeStruct((T, D), F32),
        scratch_shapes=[pltpu.VMEM((bm, D), BF16)],
        compiler_params=_params("parallel", "arbitrary"),
        name="dense_ffn",
    )(h, g, wg, wu, wd)


ROUTE_I1, ROUTE_I2, ROUTE_W1, ROUTE_W2, ROUTE_R1, ROUTE_R2 = range(6)


def _route_rows(x, g_ref, r_ref, tri_ref, route_ref, route_t_ref, cnt_ref, carry_ref, n_experts, count_scale):
    @pl.when(pl.program_id(0) == 0)
    def _():
        carry_ref[...] = jnp.zeros_like(carry_ref)

    xn = _rmsnorm(x, g_ref[...])
    x_hi = xn.astype(BF16)
    x_lo = (xn - x_hi.astype(F32)).astype(BF16)
    hi_both = _dot(x_hi, r_ref[...])
    logits = hi_both[:, :LANES] + (hi_both[:, LANES:] + _dot(x_lo, r_ref[:, pl.ds(0, LANES)]))

    lane = lax.broadcasted_iota(jnp.int32, (1, LANES), 1)
    lane_f = lane.astype(F32)
    lg = jnp.where(lane < n_experts, logits, NEG_BIG)
    v1 = jnp.max(lg, axis=-1, keepdims=True)
    i1 = jnp.min(jnp.where(lg == v1, lane_f, float(LANES)), axis=-1, keepdims=True)
    lg2 = jnp.where(lane_f == i1, NEG_BIG, lg)
    v2 = jnp.max(lg2, axis=-1, keepdims=True)
    i2 = jnp.min(jnp.where(lg2 == v2, lane_f, float(LANES)), axis=-1, keepdims=True)
    e2 = jnp.exp(v2 - v1)
    w1 = 1.0 / (1.0 + e2)
    w2 = e2 / (1.0 + e2)

    sel1 = lane_f == i1
    sel2 = lane_f == i2
    onehot = jnp.where(sel1 | sel2, 1.0, 0.0)
    rank = _dot(tri_ref[...], onehot.astype(BF16)) + carry_ref[...]
    r1 = jnp.sum(jnp.where(sel1, rank, 0.0), axis=-1, keepdims=True)
    r2 = jnp.sum(jnp.where(sel2, rank, 0.0), axis=-1, keepdims=True)
    carry_ref[...] += jnp.sum(onehot, axis=0, keepdims=True) * count_scale

    route = jnp.zeros(route_ref.shape, F32)
    for col, val in ((ROUTE_I1, i1), (ROUTE_I2, i2), (ROUTE_W1, w1), (ROUTE_W2, w2), (ROUTE_R1, r1), (ROUTE_R2, r2)):
        route = jnp.where(lane == col, val, route)
    route_ref[...] = route
    route_t_ref[...] = jnp.transpose(route)[:route_t_ref.shape[0]]
    cnt_ref[...] = jnp.broadcast_to(carry_ref[...], cnt_ref.shape)


ZERO_ROWS = 64
ROW_DMA_UNROLL = 8


def _dispatch_body(pos_ref, seg_ref, x_ref, g_ref, xs_hbm, xn_ref, zero_ref, sems, *, bm, T, n_experts, be, n_rows):
    i = pl.program_id(0)
    n_steps = pl.num_programs(0)
    slot = i % 2

    def zero_copy(first_row, c):
        row = pl.multiple_of(first_row + c * ZERO_ROWS, ZERO_ROWS)
        return pltpu.make_async_copy(zero_ref, xs_hbm.at[pl.ds(row, ZERO_ROWS), :], sems.at[2])

    def zero_block(first_row):
        for c in range(be // ZERO_ROWS):
            zero_copy(first_row, c).start()
        for c in range(be // ZERO_ROWS):
            zero_copy(first_row, c).wait()

    @pl.when(i == 0)
    def _():
        zero_ref[...] = jnp.zeros_like(zero_ref)
        for e in range(n_experts):
            @pl.when(seg_ref[n_experts + e] > 0)
            def _():
                zero_block(seg_ref[e] - be)

        def clear_unused(j, c):
            zero_block(j * be)
            return c

        lax.fori_loop(seg_ref[n_experts - 1] // be, n_rows // be, clear_unused, 0)

    def drain(s):
        for _ in range(TOP_K):
            pltpu.make_async_copy(xn_ref.at[s], xs_hbm.at[pl.ds(0, bm), :], sems.at[s]).wait()

    @pl.when(i >= 2)
    def _():
        drain(slot)

    xn_ref[slot] = _rmsnorm(x_ref[...], g_ref[...])

    def row_copy(r, k):
        dst = pos_ref[k * T + i * bm + r]
        return pltpu.make_async_copy(xn_ref.at[slot, pl.ds(r, 1), :], xs_hbm.at[pl.ds(dst, 1), :], sems.at[slot])

    def issue(grp, c):
        for u in range(ROW_DMA_UNROLL):
            row_copy(grp * ROW_DMA_UNROLL + u, 0).start(priority=u % 2)
            row_copy(grp * ROW_DMA_UNROLL + u, 1).start(priority=(u + 1) % 2)
        return c

    lax.fori_loop(0, bm // ROW_DMA_UNROLL, issue, 0)

    @pl.when(i == n_steps - 1)
    def _():
        drain(slot)

        @pl.when(i >= 1)
        def _():
            drain(1 - slot)


def _dispatch(pos, seg, h, g, *, n_rows, n_experts, be):
    T, D = h.shape
    bm = _largest_divisor(T, (256, 128, 64, 32, 16))
    body = functools.partial(_dispatch_body, bm=bm, T=T, n_experts=n_experts, be=be, n_rows=n_rows)
    return pl.pallas_call(
        body,
        grid_spec=pltpu.PrefetchScalarGridSpec(
            num_scalar_prefetch=2,
            grid=(T // bm,),
            in_specs=[pl.BlockSpec((bm, D), lambda i, pos, seg: (i, 0)),
                      pl.BlockSpec((1, D), lambda i, pos, seg: (0, 0))],
            out_specs=pl.BlockSpec(memory_space=pl.ANY),
            scratch_shapes=[pltpu.VMEM((2, bm, D), F32), pltpu.VMEM((ZERO_ROWS, D), F32),
                            pltpu.SemaphoreType.DMA((3,))],
        ),
        out_shape=jax.ShapeDtypeStruct((n_rows, D), F32),
        compiler_params=_params("arbitrary"),
        name="moe_dispatch",
    )(pos, seg, h, g)


def _expert_body(blk_ref, x_ref, wg_ref, wu_ref, wd_ref, o_ref, xb_ref, *, n_blocks):
    j = pl.program_id(0)

    @pl.when(pl.program_id(1) == 0)
    def _():
        o_ref[...] = jnp.zeros_like(o_ref)

    @pl.when(j < blk_ref[n_blocks])
    def _():
        @pl.when(pl.program_id(1) == 0)
        def _():
            xb_ref[...] = x_ref[...].astype(BF16)

        xb = xb_ref[...]
        hid = jax.nn.silu(_dot(xb, wg_ref[...])) * _dot(xb, wu_ref[...])
        o_ref[...] += _dot(hid.astype(BF16), wd_ref[...])


def _expert_ffn(blk, xs, wg, wu, wd, *, be):
    n_rows, D = xs.shape
    F = wg.shape[2]
    n_blocks = n_rows // be
    fc = _largest_divisor(F, (512, 256, 128))
    nc = F // fc
    body = functools.partial(_expert_body, n_blocks=n_blocks)

    def row_blk(j, blk):
        return jnp.minimum(j, blk[n_blocks] - 1)

    def chunk(j, c, blk):
        return jnp.where(j < blk[n_blocks], c, nc - 1)

    return pl.pallas_call(
        body,
        grid_spec=pltpu.PrefetchScalarGridSpec(
            num_scalar_prefetch=1,
            grid=(n_blocks, nc),
            in_specs=[
                pl.BlockSpec((be, D), lambda j, c, blk: (row_blk(j, blk), 0)),
                pl.BlockSpec((None, D, fc), lambda j, c, blk: (blk[row_blk(j, blk)], 0, chunk(j, c, blk))),
                pl.BlockSpec((None, D, fc), lambda j, c, blk: (blk[row_blk(j, blk)], 0, chunk(j, c, blk))),
                pl.BlockSpec((None, fc, D), lambda j, c, blk: (blk[row_blk(j, blk)], chunk(j, c, blk), 0)),
            ],
            out_specs=pl.BlockSpec((be, D), lambda j, c, blk: (j, 0)),
            scratch_shapes=[pltpu.VMEM((be, D), BF16)],
        ),
        out_shape=jax.ShapeDtypeStruct((n_rows, D), F32),
        compiler_params=_params("arbitrary", "arbitrary"),
        name="moe_experts",
    )(blk, xs, wg, wu, wd)


def _combine_body(pos_ref, h_ref, route_ref, g_ref, y_hbm, o_ref, y_ref, sems, *, bm, T):
    i = pl.program_id(0)
    n_steps = pl.num_programs(0)
    slot = i % 2

    def start_gathers(step, s):
        def row_copy(r, k):
            src = pos_ref[k * T + step * bm + r]
            return pltpu.make_async_copy(y_hbm.at[pl.ds(src, 1), :], y_ref.at[s, k, pl.ds(r, 1), :], sems.at[s])

        def issue(grp, c):
            for u in range(ROW_DMA_UNROLL):
                row_copy(grp * ROW_DMA_UNROLL + u, 0).start(priority=u % 2)
                row_copy(grp * ROW_DMA_UNROLL + u, 1).start(priority=(u + 1) % 2)
            return c

        lax.fori_loop(0, bm // ROW_DMA_UNROLL, issue, 0)

    @pl.when(i == 0)
    def _():
        start_gathers(i, slot)

    @pl.when(i + 1 < n_steps)
    def _():
        start_gathers(i + 1, 1 - slot)

    for k in range(TOP_K):
        pltpu.make_async_copy(y_hbm.at[pl.ds(0, bm), :], y_ref.at[slot, k], sems.at[slot]).wait()

    route = route_ref[...]
    w1 = route[:, ROUTE_W1:ROUTE_W1 + 1]
    w2 = route[:, ROUTE_W2:ROUTE_W2 + 1]
    h = h_ref[...] + (w1 * y_ref[slot, 0] + w2 * y_ref[slot, 1])
    o_ref[...] = _rmsnorm(h, g_ref[...])


def _combine(pos, h, route, g, y, *, B, L):
    T, D = h.shape
    S = L - N_META
    bm = _largest_divisor(L, tuple(b for b in range(512, 7, -8)))
    blocks_per_seq = L // bm
    assert pl.cdiv(S, bm) == blocks_per_seq
    body = functools.partial(_combine_body, bm=bm, T=T)
    return pl.pallas_call(
        body,
        grid_spec=pltpu.PrefetchScalarGridSpec(
            num_scalar_prefetch=1,
            grid=(T // bm,),
            in_specs=[pl.BlockSpec((bm, D), lambda i, pos: (i, 0)),
                      pl.BlockSpec((bm, LANES), lambda i, pos: (i, 0)),
                      pl.BlockSpec((1, D), lambda i, pos: (0, 0)),
                      pl.BlockSpec(memory_space=pl.ANY)],
            out_specs=pl.BlockSpec((None, bm, D), lambda i, pos: (i // blocks_per_seq, i % blocks_per_seq, 0)),
            scratch_shapes=[pltpu.VMEM((2, TOP_K, bm, D), F32), pltpu.SemaphoreType.DMA((2,))],
        ),
        out_shape=jax.ShapeDtypeStruct((B, S, D), F32),
        compiler_params=_params("arbitrary"),
        name="moe_combine",
    )(pos, h, route, g, y)


def _rope_tables(B, L):
    half = HEAD_DIM // 2
    inv = ROPE_THETA ** (-2.0 * jnp.arange(half, dtype=F32) / HEAD_DIM)
    pos = jnp.concatenate([jnp.arange(N_META, L), jnp.arange(N_META)]).astype(F32)
    ang = pos[:, None] * inv[None, :]
    cs = jnp.tile(jnp.stack([jnp.cos(ang), jnp.sin(ang)]), (1, 1, LANES // half))
    cs = jnp.stack([cs * (HEAD_DIM ** -0.5 * LOG2_E), cs])
    return jnp.broadcast_to(cs[:, :, None], (2, 2, B, L, LANES)).reshape(2, 2, B * L, LANES)


def _pair_rotary_halves(w_qk, n_heads):
    D = w_qk.shape[0]
    w = w_qk.reshape(D, 2, n_heads // 2, 2, 2, 2, HEAD_DIM // 2)
    return w.transpose(0, 1, 2, 5, 3, 4, 6).reshape(D, -1)


def _token_mixer(h, layer, B, L, w_in, b_gate, lam, subln_g, pool_w, pool_scale, w_ba, w_bp, w_out, norm_g, rope_cs,
                 routing=None):
    A, P = w_ba.shape[0], w_bp.shape[0]
    n_heads = A // (2 * HEAD_DIM)
    lambda_init = 0.8 - 0.6 * math.exp(-0.3 * layer)
    w_qk = _pair_rotary_halves(w_in[:, :2 * A], n_heads).astype(BF16)
    if isinstance(h, tuple):
        z, h = _inproj_first(h[0], h[1], norm_g, w_qk, w_in.astype(BF16), rope_cs)
    else:
        z = _inproj(h, norm_g, w_qk, w_in.astype(BF16), rope_cs)
    T, D = h.shape
    z3 = z.reshape(B, L, z.shape[1])
    a = _attention(z3, lam, subln_g, n_heads=n_heads, lambda_init=lambda_init)
    p = _pool(z3, pool_w, pool_scale, col_block=(3 * A) // P)
    return _mix(a.reshape(T, A), p.reshape(T, P), z, b_gate, w_ba, w_bp, w_out, h,
                gate_col_block=(3 * A + P) // (2 * D), routing=routing)


def _router_hi_lo(router):
    D, E = router.shape
    r_pad = jnp.zeros((D, LANES), F32).at[:, :E].set(router)
    r_hi = r_pad.astype(BF16)
    r_lo = (r_pad - r_hi.astype(F32)).astype(BF16)
    return jnp.concatenate([r_hi, r_lo], axis=1)


def _moe_and_final_norm(h, route, route_t, cnt, norm_g, E, wg, wu, wd, final_g, *, B, L):
    T, D = h.shape
    be = 1024 if T >= 8192 else 64

    counts = cnt[0, :E].astype(jnp.int32)
    n_blk = (counts + be - 1) // be
    seg_end = jnp.cumsum(n_blk * be)
    seg_start = seg_end - n_blk * be
    n_blocks = (TOP_K * T + E * (be - 1)) // be
    blk_end = jnp.cumsum(n_blk)
    blk_expert = jnp.minimum(jnp.sum(jnp.arange(n_blocks)[:, None] >= blk_end[None, :], axis=1), E - 1)
    blk = jnp.concatenate([blk_expert.astype(jnp.int32), blk_end[-1:].astype(jnp.int32)])
    seg = jnp.concatenate([seg_end, n_blk]).astype(jnp.int32)

    def slot_of(expert_f, rank_f):
        start = jnp.sum(jnp.where(expert_f[:, None] == jnp.arange(E, dtype=F32)[None, :], seg_start[None, :], 0), axis=1)
        return start + rank_f.astype(jnp.int32)

    pos = jnp.concatenate([slot_of(route_t[ROUTE_I1], route_t[ROUTE_R1]), slot_of(route_t[ROUTE_I2], route_t[ROUTE_R2])])

    xs = _dispatch(pos, seg, h, norm_g, n_rows=n_blocks * be, n_experts=E, be=be)
    y = _expert_ffn(blk, xs, wg, wu, wd, be=be)
    return _combine(pos, h, route, final_g, y, B=B, L=L)


def kernel(x, meta_tokens, norm_mix, w_in, b_gate, lambda_q1, lambda_k1, lambda_q2, lambda_k2, subln, pool_w, pool_scale, w_branch_attn, w_branch_pool, w_out, norm_ffn, dense_w_gate, dense_w_up, dense_w_down, router, moe_w_gate, moe_w_up, moe_w_down, norm_final):
    B, S, D = x.shape
    depth = w_in.shape[0]
    assert depth == 2, "the final RMSNorm is fused into the routed layer, which must be the last one"
    L = N_META + S
    T = B * L
    h = (x, meta_tokens.astype(x.dtype))
    rope = _rope_tables(B, L)
    row = lambda v: v.reshape(1, -1)

    for i in range(depth):
        lam = jnp.stack([lambda_q1[i], lambda_k1[i], lambda_q2[i], lambda_k2[i]])
        j = i // 2
        routed = i % 2 == 1
        n_experts = router.shape[2]
        mixed = _token_mixer(h, i, B, L, w_in[i], row(b_gate[i]), lam, row(subln[i]),
                             pool_w[i].astype(BF16), row(pool_scale[i]), w_branch_attn[i].astype(BF16),
                             w_branch_pool[i].astype(BF16), w_out[i].astype(BF16), row(norm_mix[i]), rope,
                             routing=(row(norm_ffn[i]), _router_hi_lo(router[j]), n_experts) if routed else None)
        if not routed:
            h = _dense_ffn(mixed, row(norm_ffn[i]), dense_w_gate[j].astype(BF16), dense_w_up[j].astype(BF16),
                           dense_w_down[j].astype(BF16))
        else:
            h, route, route_t, cnt = mixed
            h = _moe_and_final_norm(h, route, route_t, cnt, row(norm_ffn[i]), n_experts, moe_w_gate[j].astype(BF16),
                                    moe_w_up[j].astype(BF16), moe_w_down[j].astype(BF16), row(norm_final),
                                    B=B, L=L)
    return h
```
